```python
import math
import jax, jax.numpy as jnp
from jax import lax
import numpy as np

D_MODEL = 1024
BATCH = 16
SEQ = 2048
DEPTH = 2
DEC_BATCH = 1
DEC_SEQ = 16384
PAST_LEN = 128

D_MIX = D_MODEL
HEAD_DIM = 64
N_HEADS = 8
N_KV_HEADS = 2
GQA_GROUP = N_HEADS // N_KV_HEADS
ATTN_W = N_HEADS * HEAD_DIM
KV_W = N_KV_HEADS * HEAD_DIM
SSM_W = D_MIX // 4
SSM_GROUP = 16
SSM_GROUPS = SSM_W // SSM_GROUP
SSM_STATE = 64
CONV_W = D_MIX // 4
CONV_KERNEL = 31
CONV_PAD = CONV_KERNEL // 2
IN_W = ATTN_W + 2 * KV_W + SSM_W + 2 * CONV_W
GRID_W = 64
ROPE_AXIS_DIM = HEAD_DIM // 2
ROPE_THETA = 10000.0
Q_BLOCK = 128
N_EXPERTS = 32
TOP_K = 4
D_EXPERT = D_MODEL
SWIGLU_LIMIT = 7.0
SWIGLU_ALPHA = 1.702
MOE_BLOCK = 128
EPS = 1e-6

kernel_name = 'hymba_s5_gqa2drope_conformer_moe_encoder'


def _rms(x, g):
    xf = x.astype(jnp.float32)
    y = xf * lax.rsqrt(jnp.mean(xf * xf, axis=-1, keepdims=True) + EPS)
    return (y * g.astype(jnp.float32)).astype(x.dtype)


def _layer_norm(x, g, b):
    xf = x.astype(jnp.float32)
    mu = jnp.mean(xf, axis=-1, keepdims=True)
    var = jnp.mean(jnp.square(xf - mu), axis=-1, keepdims=True)
    y = (xf - mu) * lax.rsqrt(var + EPS)
    return (y * g.astype(jnp.float32) + b.astype(jnp.float32)).astype(x.dtype)


def _grid_angles(l):
    rows = l // GRID_W
    row_idx = jnp.repeat(jnp.arange(rows, dtype=jnp.float32), GRID_W)
    col_idx = jnp.tile(jnp.arange(GRID_W, dtype=jnp.float32), rows)
    inv = 1.0 / (ROPE_THETA ** (jnp.arange(0, ROPE_AXIS_DIM, 2, dtype=jnp.float32) / ROPE_AXIS_DIM))
    return row_idx[:, None] * inv[None, :], col_idx[:, None] * inv[None, :]


def _rotate(xp, ang):
    half = ROPE_AXIS_DIM // 2
    c = jnp.cos(ang)[None, :, None, :]
    s = jnp.sin(ang)[None, :, None, :]
    x1, x2 = xp[..., :half], xp[..., half:]
    return jnp.concatenate([x1 * c - x2 * s, x2 * c + x1 * s], axis=-1)


def _rope_2d(x, ang_row, ang_col):
    xf = x.astype(jnp.float32)
    out = jnp.concatenate([_rotate(xf[..., :ROPE_AXIS_DIM], ang_row),
                           _rotate(xf[..., ROPE_AXIS_DIM:], ang_col)], axis=-1)
    return out.astype(x.dtype)


def _attention(q, k, v):
    b, l = q.shape[0], q.shape[1]
    nb = l // Q_BLOCK
    qb = q.reshape(b, nb, Q_BLOCK, N_KV_HEADS, GQA_GROUP, HEAD_DIM).transpose(1, 0, 2, 3, 4, 5)
    scale = HEAD_DIM ** -0.5

    def one_block(qblk):
        s = jnp.einsum('bqhgd,bkhd->bhgqk', qblk, k, preferred_element_type=jnp.float32) * scale
        p = jax.nn.softmax(s, axis=-1).astype(v.dtype)
        return jnp.einsum('bhgqk,bkhd->bqhgd', p, v)

    o = lax.map(one_block, qb)
    return o.transpose(1, 0, 2, 3, 4, 5).reshape(b, l, ATTN_W)


def _lin_combine(left, right):
    a1, b1 = left
    a2, b2 = right
    return a1 * a2, a2 * b1 + b2


def _s5_bidir(u, lam_re, lam_im, log_dt, b_re, b_im, c_re, c_im, d_skip, w_glu, b_glu):
    bsz, l, _ = u.shape
    f32 = jnp.float32
    uf = u.astype(f32).reshape(bsz, l, SSM_GROUPS, SSM_GROUP)
    lam = lax.complex(lam_re.astype(f32), lam_im.astype(f32))
    dt = jnp.exp(log_dt.astype(f32))[..., None]
    lam_bar = jnp.exp(lam * dt)
    bmat = lax.complex(b_re.astype(f32), b_im.astype(f32))
    b_bar = ((lam_bar - 1.0) / lam)[..., None] * bmat
    cmat = lax.complex(c_re.astype(f32), c_im.astype(f32))

    def scan_dir(d, reverse):
        bu = jnp.einsum('gph,blgh->blgp', b_bar[d], uf)
        a = jnp.broadcast_to(lam_bar[d], bu.shape)
        _, xs = lax.associative_scan(_lin_combine, (a, bu), reverse=reverse, axis=1)
        return jnp.einsum('ghp,blgp->blgh', cmat[d], xs).real

    y = scan_dir(0, False) + scan_dir(1, True) + d_skip.astype(f32).reshape(SSM_GROUPS, SSM_GROUP) * uf
    y = jax.nn.gelu(y.reshape(bsz, l, SSM_W), approximate=False)
    y = y * jax.nn.sigmoid(y @ w_glu.astype(f32) + b_glu.astype(f32))
    return y.astype(u.dtype)


def _conv_module(a, gate, w_dw, b_dw, ln_g, ln_b, w_pw):
    h = a * jax.nn.sigmoid(gate)
    h = lax.conv_general_dilated(h, w_dw[:, None, :], window_strides=(1,), padding=[(CONV_PAD, CONV_PAD)],
                                 dimension_numbers=('NWC', 'WIO', 'NWC'), feature_group_count=CONV_W) + b_dw
    h = _layer_norm(h, ln_g, ln_b)
    h = h * jax.nn.sigmoid(h)
    return h @ w_pw


def _moe(h, router_w, router_b, w1, b1, w2, b2):
    t = h.shape[0]
    n_assign = t * TOP_K
    logits = jnp.matmul(h, router_w, preferred_element_type=jnp.float32) + router_b.astype(jnp.float32)
    top_vals, top_idx = lax.top_k(logits, TOP_K)
    gates = jax.nn.softmax(top_vals, axis=-1)
    flat_e = top_idx.reshape(-1).astype(jnp.int32)
    flat_tok = jnp.repeat(jnp.arange(t, dtype=jnp.int32), TOP_K)
    flat_g = gates.reshape(-1)
    order = jnp.argsort(flat_e, stable=True)
    se, stok, sg = flat_e[order], flat_tok[order], flat_g[order]
    counts = jnp.bincount(flat_e, length=N_EXPERTS).astype(jnp.int32)
    starts = jnp.cumsum(counts) - counts
    padded = ((counts + MOE_BLOCK - 1) // MOE_BLOCK) * MOE_BLOCK
    ends_p = jnp.cumsum(padded)
    starts_p = ends_p - padded
    dest = starts_p[se] + (jnp.arange(n_assign, dtype=jnp.int32) - starts[se])
    n_blocks = -(-n_assign // MOE_BLOCK) + N_EXPERTS
    cap = n_blocks * MOE_BLOCK
    buf_tok = jnp.zeros((cap,), jnp.int32).at[dest].set(stok)
    buf_g = jnp.zeros((cap,), jnp.float32).at[dest].set(sg)
    block_e = jnp.clip(jnp.searchsorted(ends_p, jnp.arange(n_blocks, dtype=jnp.int32) * MOE_BLOCK, side='right'),
                       0, N_EXPERTS - 1)

    def run_block(args):
        tok, g, e = args
        xb = h[tok]
        a = xb @ w1[e] + b1[e]
        gt, up = a[:, :D_EXPERT], a[:, D_EXPERT:]
        gt = jnp.minimum(gt, SWIGLU_LIMIT)
        up = jnp.clip(up, -SWIGLU_LIMIT, SWIGLU_LIMIT)
        act = (up + 1.0) * (gt * jax.nn.sigmoid(SWIGLU_ALPHA * gt))
        out = act @ w2[e] + b2[e]
        return out * g[:, None].astype(out.dtype)

    out = lax.map(run_block, (buf_tok.reshape(n_blocks, MOE_BLOCK), buf_g.reshape(n_blocks, MOE_BLOCK), block_e))
    return jax.ops.segment_sum(out.reshape(cap, D_MODEL), buf_tok, num_segments=t)


def _layer(x, c, w_ada, b_ada, norm_pre_mix, norm_post_mix, norm_pre_ffn, norm_post_ffn, w_in, q_norm, k_norm,
           ssm_lam_re, ssm_lam_im, ssm_log_dt, ssm_b_re, ssm_b_im, ssm_c_re, ssm_c_im, ssm_d, ssm_w_glu, ssm_b_glu,
           conv_w_dw, conv_b_dw, conv_ln_g, conv_ln_b, conv_w_pw, branch_norm, w_out,
           router_w, router_b, expert_w1, expert_b1, expert_w2, expert_b2):
    b, l, _ = x.shape
    mod = jax.nn.silu(c) @ w_ada + b_ada
    sh1, sc1, g1, sh2, sc2, g2 = [m[:, None, :] for m in jnp.split(mod, 6, axis=-1)]

    h = _rms(x, norm_pre_mix) * (1.0 + sc1) + sh1
    proj = h @ w_in
    o1 = ATTN_W
    o2 = o1 + KV_W
    o3 = o2 + KV_W
    o4 = o3 + SSM_W
    o5 = o4 + CONV_W
    q = _rms(proj[..., :o1].reshape(b, l, N_HEADS, HEAD_DIM), q_norm)
    k = _rms(proj[..., o1:o2].reshape(b, l, N_KV_HEADS, HEAD_DIM), k_norm)
    v = proj[..., o2:o3].reshape(b, l, N_KV_HEADS, HEAD_DIM)
    ang_row, ang_col = _grid_angles(l)
    q = _rope_2d(q, ang_row, ang_col)
    k = _rope_2d(k, ang_row, ang_col)
    attn = _attention(q, k, v)
    ssm = _s5_bidir(proj[..., o3:o4], ssm_lam_re, ssm_lam_im, ssm_log_dt, ssm_b_re, ssm_b_im,
                    ssm_c_re, ssm_c_im, ssm_d, ssm_w_glu, ssm_b_glu)
    conv = _conv_module(proj[..., o4:o5], proj[..., o5:], conv_w_dw, conv_b_dw, conv_ln_g, conv_ln_b, conv_w_pw)
    mix = jnp.concatenate([_rms(attn, branch_norm[:ATTN_W]),
                           _rms(ssm, branch_norm[ATTN_W:ATTN_W + SSM_W]),
                           _rms(conv, branch_norm[ATTN_W + SSM_W:])], axis=-1) @ w_out
    x = x + g1 * _rms(mix, norm_post_mix)

    h2 = _rms(x, norm_pre_ffn) * (1.0 + sc2) + sh2
    ff = _moe(h2.reshape(b * l, D_MODEL), router_w, router_b, expert_w1, expert_b1, expert_w2, expert_b2)
    x = x + g2 * _rms(ff.reshape(b, l, D_MODEL), norm_post_ffn)
    return x


def setup_inputs(seed: int = 0) -> dict:
    key = jax.random.key(seed)
    ks = iter(jax.random.split(key, 64))
    f32 = jnp.float32

    def nrm(shape, scale):
        return scale * jax.random.normal(next(ks), shape, f32)

    def gain(shape):
        return 1.0 + nrm(shape, 0.01)

    L, G, P, H = DEPTH, SSM_GROUPS, SSM_STATE, SSM_GROUP
    n_idx = jnp.arange(P, dtype=f32)
    return {
        'x_prompt': nrm((BATCH, SEQ, D_MODEL), 1.0),
        'x_sample': nrm((DEC_BATCH, DEC_SEQ, D_MODEL), 1.0),
        'c_prompt': nrm((BATCH, D_MODEL), 1.0),
        'c_sample': nrm((DEC_BATCH, D_MODEL), 1.0),
        'w_ada': nrm((L, D_MODEL, 6 * D_MODEL), 0.5 * D_MODEL ** -0.5),
        'b_ada': nrm((L, 6 * D_MODEL), 0.01),
        'norm_pre_mix': gain((L, D_MODEL)),
        'norm_post_mix': gain((L, D_MODEL)),
        'norm_pre_ffn': gain((L, D_MODEL)),
        'norm_post_ffn': gain((L, D_MODEL)),
        'w_in': nrm((L, D_MODEL, IN_W), D_MODEL ** -0.5),
        'q_norm': gain((L, HEAD_DIM)),
        'k_norm': gain((L, HEAD_DIM)),
        'ssm_lam_re': -0.5 + nrm((L, 2, G, P), 0.01),
        'ssm_lam_im': jnp.pi * n_idx + nrm((L, 2, G, P), 0.01),
        'ssm_log_dt': jax.random.uniform(next(ks), (L, 2, G), f32, math.log(1e-3), math.log(1e-1)),
        'ssm_b_re': nrm((L, 2, G, P, H), (2.0 * H) ** -0.5),
        'ssm_b_im': nrm((L, 2, G, P, H), (2.0 * H) ** -0.5),
        'ssm_c_re': nrm((L, 2, G, H, P), (2.0 * P) ** -0.5),
        'ssm_c_im': nrm((L, 2, G, H, P), (2.0 * P) ** -0.5),
        'ssm_d': nrm((L, SSM_W), 1.0),
        'ssm_w_glu': nrm((L, SSM_W, SSM_W), SSM_W ** -0.5),
        'ssm_b_glu': nrm((L, SSM_W), 0.01),
        'conv_w_dw': nrm((L, CONV_KERNEL, CONV_W), CONV_KERNEL ** -0.5),
        'conv_b_dw': nrm((L, CONV_W), 0.01),
        'conv_ln_g': gain((L, CONV_W)),
        'conv_ln_b': nrm((L, CONV_W), 0.01),
        'conv_w_pw': nrm((L, CONV_W, CONV_W), CONV_W ** -0.5),
        'branch_norm': gain((L, D_MIX)),
        'w_out': nrm((L, D_MIX, D_MODEL), D_MIX ** -0.5),
        'router_w': nrm((L, D_MODEL, N_EXPERTS), D_MODEL ** -0.5),
        'router_b': nrm((L, N_EXPERTS), 0.01),
        'expert_w1': nrm((L, N_EXPERTS, D_MODEL, 2 * D_EXPERT), D_MODEL ** -0.5),
        'expert_b1': nrm((L, N_EXPERTS, 2 * D_EXPERT), 0.01),
        'expert_w2': nrm((L, N_EXPERTS, D_EXPERT, D_MODEL), D_EXPERT ** -0.5),
        'expert_b2': nrm((L, N_EXPERTS, D_MODEL), 0.01),
    }


def reference(x_prompt, x_sample, c_prompt, c_sample, w_ada, b_ada, norm_pre_mix, norm_post_mix, norm_pre_ffn,
              norm_post_ffn, w_in, q_norm, k_norm, ssm_lam_re, ssm_lam_im, ssm_log_dt, ssm_b_re, ssm_b_im,
              ssm_c_re, ssm_c_im, ssm_d, ssm_w_glu, ssm_b_glu, conv_w_dw, conv_b_dw, conv_ln_g, conv_ln_b,
              conv_w_pw, branch_norm, w_out, router_w, router_b, expert_w1, expert_b1, expert_w2, expert_b2):
    xp = x_prompt
    xs = x_sample
    for i in range(DEPTH):
        lp = (w_ada[i], b_ada[i], norm_pre_mix[i], norm_post_mix[i], norm_pre_ffn[i], norm_post_ffn[i], w_in[i],
              q_norm[i], k_norm[i], ssm_lam_re[i], ssm_lam_im[i], ssm_log_dt[i], ssm_b_re[i], ssm_b_im[i],
              ssm_c_re[i], ssm_c_im[i], ssm_d[i], ssm_w_glu[i], ssm_b_glu[i], conv_w_dw[i], conv_b_dw[i],
              conv_ln_g[i], conv_ln_b[i], conv_w_pw[i], branch_norm[i], w_out[i], router_w[i], router_b[i],
              expert_w1[i], expert_b1[i], expert_w2[i], expert_b2[i])
        xp = _layer(xp, c_prompt, *lp)
        xs = _layer(xs, c_sample, *lp)
    return (xp, xs)
```

```python
import functools
import math

import jax
import jax.numpy as jnp
from jax import lax
from jax.experimental import pallas as pl
from jax.experimental.pallas import tpu as pltpu

F32 = jnp.float32
BF16 = jnp.bfloat16
I32 = jnp.int32

HEAD_DIM = 64
N_HEADS = 8
N_KV_HEADS = 2
GQA_GROUP = N_HEADS // N_KV_HEADS
ATTN_W = N_HEADS * HEAD_DIM
KV_W = N_KV_HEADS * HEAD_DIM
SSM_GROUP = 16
SSM_STATE = 64
CONV_KERNEL = 31
CONV_PAD = CONV_KERNEL // 2
GRID_W = 64
ROPE_AXIS_DIM = HEAD_DIM // 2
ROPE_THETA = 10000.0
N_EXPERTS = 32
TOP_K = 4
SWIGLU_LIMIT = 7.0
SWIGLU_ALPHA = 1.702
EPS = 1e-6

LANES = 128
HALO = 16
S5_CHUNK = 64
MOE_BLOCK = 512
NEG_BIG = -1e30
VMEM_LIMIT = 48 * 1024 * 1024


def _cparams(sem):
    return pltpu.CompilerParams(dimension_semantics=sem, vmem_limit_bytes=VMEM_LIMIT)


def _rsqrt_mean_sq(x):
    return lax.rsqrt(jnp.mean(x * x, axis=-1, keepdims=True) + EPS)


def _mod_kernel(c_ref, w_ref, b_ref, o_ref):
    c = c_ref[...]
    s = c * jax.nn.sigmoid(c)
    o_ref[...] = jnp.dot(s, w_ref[...], preferred_element_type=F32,
                         precision=lax.Precision.HIGHEST) + b_ref[...]


def _modulation(c_all, w_ada, b_ada):
    s, d = c_all.shape
    n = w_ada.shape[1]
    tn = n // 4
    return pl.pallas_call(
        _mod_kernel,
        grid=(n // tn,),
        in_specs=[pl.BlockSpec((s, d), lambda j: (0, 0)),
                  pl.BlockSpec((d, tn), lambda j: (0, j)),
                  pl.BlockSpec((1, tn), lambda j: (0, j))],
        out_specs=pl.BlockSpec((s, tn), lambda j: (0, j)),
        out_shape=jax.ShapeDtypeStruct((s, n), F32),
        compiler_params=_cparams(("arbitrary",)),
        name="adaln_mod",
    )(c_all, w_ada, b_ada.reshape(1, n))


def _inproj_kernel(x_ref, mod_ref, npre_ref, w_ref, qkn_ref, cos_ref, sin_ref, bd_ref,
                   q_ref, k_ref, v_ref, u_ref, hc_ref):
    x = x_ref[...]
    sh = mod_ref[0, 0:1, :]
    sc = mod_ref[0, 1:2, :]
    h = x * _rsqrt_mean_sq(x) * npre_ref[...] * (1.0 + sc) + sh
    proj = jnp.dot(h.astype(BF16), w_ref[...], preferred_element_type=F32)
    tm = x.shape[0]
    cos = cos_ref[...]
    sin = sin_ref[...]
    lane = lax.broadcasted_iota(I32, (tm, LANES), 1)
    first_half = (lane % ROPE_AXIS_DIM) < (ROPE_AXIS_DIM // 2)
    n_qk = (ATTN_W + KV_W) // LANES
    for c in range(n_qk):
        xc = proj[:, c * LANES:(c + 1) * LANES]
        ms = jnp.dot((xc * xc).astype(BF16), bd_ref[...], preferred_element_type=F32) * (1.0 / HEAD_DIM)
        xn = xc * lax.rsqrt(ms + EPS) * qkn_ref[c:c + 1, :]
        half = ROPE_AXIS_DIM // 2
        swapped = jnp.where(first_half, pltpu.roll(xn, LANES - half, 1), pltpu.roll(xn, half, 1))
        rot = (xn * cos + swapped * sin).astype(BF16)
        if c < ATTN_W // LANES:
            q_ref[2 * c] = rot[:, :HEAD_DIM]
            q_ref[2 * c + 1] = rot[:, HEAD_DIM:]
        else:
            k_ref[0] = rot[:, :HEAD_DIM]
            k_ref[1] = rot[:, HEAD_DIM:]
    o2 = ATTN_W + KV_W
    v = proj[:, o2:o2 + KV_W].astype(BF16)
    v_ref[0] = v[:, :HEAD_DIM]
    v_ref[1] = v[:, HEAD_DIM:]
    o3 = o2 + KV_W
    ssm_w = u_ref.shape[1]
    u_ref[...] = proj[:, o3:o3 + ssm_w].astype(BF16)
    o4 = o3 + ssm_w
    conv_w = hc_ref.shape[1]
    a = proj[:, o4:o4 + conv_w]
    gate = proj[:, o4 + conv_w:o4 + 2 * conv_w]
    hc_ref[...] = a * jax.nn.sigmoid(gate)


def _inproj(x, mod, norm_pre, w_in_bf, qkn, cos_t, sin_t, bd, *, n_prompt, lc, tm, ssm_w, conv_w):
    t, d = x.shape
    nb = lc // tm
    s = t // lc
    in_w = w_in_bf.shape[1]

    def row(si, i):
        return si * nb + i

    def pos(si, i):
        return jnp.where(si < n_prompt, i, (si - n_prompt) * nb + i)

    return pl.pallas_call(
        _inproj_kernel,
        grid=(s, nb),
        in_specs=[pl.BlockSpec((tm, d), lambda si, i: (row(si, i), 0)),
                  pl.BlockSpec((1, 6, d), lambda si, i: (si, 0, 0)),
                  pl.BlockSpec((1, d), lambda si, i: (0, 0)),
                  pl.BlockSpec((d, in_w), lambda si, i: (0, 0)),
                  pl.BlockSpec((8, LANES), lambda si, i: (0, 0)),
                  pl.BlockSpec((tm, LANES), lambda si, i: (pos(si, i), 0)),
                  pl.BlockSpec((tm, LANES), lambda si, i: (pos(si, i), 0)),
                  pl.BlockSpec((LANES, LANES), lambda si, i: (0, 0))],
        out_specs=[pl.BlockSpec((N_HEADS, tm, HEAD_DIM), lambda si, i: (0, row(si, i), 0)),
                   pl.BlockSpec((N_KV_HEADS, tm, HEAD_DIM), lambda si, i: (0, row(si, i), 0)),
                   pl.BlockSpec((N_KV_HEADS, tm, HEAD_DIM), lambda si, i: (0, row(si, i), 0)),
                   pl.BlockSpec((tm, ssm_w), lambda si, i: (row(si, i), 0)),
                   pl.BlockSpec((tm, conv_w), lambda si, i: (row(si, i), 0))],
        out_shape=[jax.ShapeDtypeStruct((N_HEADS, t, HEAD_DIM), BF16),
                   jax.ShapeDtypeStruct((N_KV_HEADS, t, HEAD_DIM), BF16),
                   jax.ShapeDtypeStruct((N_KV_HEADS, t, HEAD_DIM), BF16),
                   jax.ShapeDtypeStruct((t, ssm_w), BF16),
                   jax.ShapeDtypeStruct((t, conv_w), F32)],
        compiler_params=_cparams(("parallel", "arbitrary")),
        name="inproj",
    )(x, mod, norm_pre, w_in_bf, qkn, cos_t, sin_t, bd)


def _attn_kernel(q_ref, k_ref, v_ref, o_ref, m_ref, l_ref, acc_ref, *, tk):
    g, tq, hd = q_ref.shape
    rows = g * tq
    q = q_ref[...].reshape(rows, hd)
    lkv = k_ref.shape[1]
    m_ref[...] = jnp.full(m_ref.shape, NEG_BIG, F32)
    l_ref[...] = jnp.zeros(l_ref.shape, F32)
    acc_ref[...] = jnp.zeros(acc_ref.shape, F32)

    def body(j, carry):
        start = pl.multiple_of(j * tk, tk)
        kc = k_ref[0, pl.ds(start, tk), :]
        vc = v_ref[0, pl.ds(start, tk), :]
        s = lax.dot_general(q, kc, (((1,), (1,)), ((), ())), preferred_element_type=F32)
        m_prev = m_ref[...]
        m_next = jnp.maximum(m_prev, jnp.max(s, axis=1, keepdims=True))
        alpha = jnp.exp(m_prev - m_next)
        p = jnp.exp(s - jnp.tile(m_next, (1, tk // LANES)))
        l_ref[...] = alpha * l_ref[...] + jnp.sum(p, axis=1, keepdims=True)
        acc_ref[...] = acc_ref[...] * alpha[:, :hd] + jnp.dot(p.astype(BF16), vc, preferred_element_type=F32)
        m_ref[...] = m_next
        return carry

    lax.fori_loop(0, lkv // tk, body, 0)
    inv = 1.0 / l_ref[...]
    o = (acc_ref[...] * inv[:, :hd]).reshape(g, tq, hd)
    o_ref[...] = jnp.concatenate([o[i] for i in range(g)], axis=-1).astype(o_ref.dtype)


def _attention(q, k, v, *, row0, n_seq, lkv, tq, tk):
    nq = lkv // tq
    qb0 = row0 // tq
    kb0 = row0 // lkv
    rows = GQA_GROUP * tq
    return pl.pallas_call(
        functools.partial(_attn_kernel, tk=tk),
        grid=(n_seq, N_KV_HEADS, nq),
        in_specs=[pl.BlockSpec((GQA_GROUP, tq, HEAD_DIM), lambda b, g, i: (g, qb0 + b * nq + i, 0)),
                  pl.BlockSpec((1, lkv, HEAD_DIM), lambda b, g, i: (g, kb0 + b, 0)),
                  pl.BlockSpec((1, lkv, HEAD_DIM), lambda b, g, i: (g, kb0 + b, 0))],
        out_specs=pl.BlockSpec((tq, GQA_GROUP * HEAD_DIM), lambda b, g, i: (b * nq + i, g)),
        out_shape=jax.ShapeDtypeStruct((n_seq * lkv, ATTN_W), BF16),
        scratch_shapes=[pltpu.VMEM((rows, LANES), F32), pltpu.VMEM((rows, LANES), F32),
                        pltpu.VMEM((rows, HEAD_DIM), F32)],
        compiler_params=_cparams(("parallel", "parallel", "arbitrary")),
        name="attention",
    )(q, k, v)


def _s5_weights(lam_re, lam_im, log_dt, b_re, b_im, c_re, c_im, d_skip, tc):
    hp = lax.Precision.HIGHEST
    lam = lax.complex(lam_re.astype(F32), lam_im.astype(F32))
    dt = jnp.exp(log_dt.astype(F32))[..., None]
    lam_dt = lam * dt
    lam_bar = jnp.exp(lam_dt)
    b_bar = ((lam_bar - 1.0) / lam)[..., None] * lax.complex(b_re.astype(F32), b_im.astype(F32))
    cmat = lax.complex(c_re.astype(F32), c_im.astype(F32))
    g, p = lam.shape[1], lam.shape[2]
    h = b_bar.shape[-1]
    taus = jnp.arange(tc + 1, dtype=F32)
    pw = jnp.exp(lam_dt[..., None] * taus)
    kern = jnp.einsum('dgop,dgpt,dgpi->dgtoi', cmat, pw, b_bar, precision=hp).real
    jj = jnp.arange(tc)[:, None]
    ii = jnp.arange(tc)[None, :]
    lag = ii - jj
    kf = kern[0][:, jnp.clip(lag, 0, tc)]
    kb = kern[1][:, jnp.clip(-lag, 0, tc)]
    eye_h = jnp.eye(h, dtype=F32)
    kd = kern[0][:, 0] + kern[1][:, 0] + d_skip.astype(F32).reshape(g, h)[:, :, None] * eye_h
    lag5 = lag[None, :, :, None, None]
    full = jnp.where(lag5 > 0, kf, jnp.where(lag5 < 0, kb, kd[:, None, None]))
    toep = full.transpose(0, 1, 4, 2, 3).reshape(g, tc * h, tc * h)

    jv = jnp.arange(tc)
    pf = pw[0][:, :, tc - 1 - jv]
    pb = pw[1][:, :, jv]
    sf = pf[:, :, :, None] * b_bar[0][:, :, None, :]
    sb = pb[:, :, :, None] * b_bar[1][:, :, None, :]
    parts = [sf.real, sf.imag, sb.real, sb.imag]
    ws = jnp.stack([x.transpose(0, 2, 3, 1).reshape(g, tc * h, p) for x in parts], axis=2)
    odd = (jnp.arange(g) % 2)[:, None, None, None]
    zeros = jnp.zeros_like(ws)
    ws = jnp.where(odd == 0, jnp.concatenate([ws, zeros], -1), jnp.concatenate([zeros, ws], -1))

    ef = cmat[0].transpose(0, 2, 1)[:, :, None, :] * pw[0][:, :, 1 + jv][:, :, :, None]
    eb = cmat[1].transpose(0, 2, 1)[:, :, None, :] * pw[1][:, :, tc - jv][:, :, :, None]
    cparts = [ef.real, -ef.imag, eb.real, -eb.imag]
    wc = jnp.stack([x.reshape(g, p, tc * h) for x in cparts], axis=1)
    zc = jnp.zeros_like(wc)
    odd4 = (jnp.arange(g) % 2)[:, None, None, None]
    wc = jnp.where(odd4 == 0, jnp.concatenate([wc, zc], 2), jnp.concatenate([zc, wc], 2))
    wc = wc.reshape(g, 4 * 2 * p, tc * h)

    af = pw[0][:, :, tc].reshape(1, g * p)
    ab = pw[1][:, :, tc].reshape(1, g * p)
    decay = jnp.concatenate([af.real, af.imag, ab.real, ab.imag], axis=0)
    return toep.astype(BF16), ws.reshape(g, tc * h, 4 * 2 * p).astype(BF16), wc.astype(BF16), decay


def _s5_intra_kernel(u_ref, toep_ref, ws_ref, y_ref, s0_ref, s1_ref, s2_ref, s3_ref):
    u0 = u_ref[0]
    u1 = u_ref[1]
    y_ref[0] = jnp.dot(u0, toep_ref[0], preferred_element_type=F32)
    y_ref[1] = jnp.dot(u1, toep_ref[1], preferred_element_type=F32)
    st = (jnp.dot(u0, ws_ref[0], preferred_element_type=F32)
          + jnp.dot(u1, ws_ref[1], preferred_element_type=F32))
    for k, ref in enumerate((s0_ref, s1_ref, s2_ref, s3_ref)):
        ref[...] = st[:, k * LANES:(k + 1) * LANES]


def _s5_intra(ug, toep, ws, *, tr):
    g, nc, kw = ug.shape
    npair = g // 2
    st_shape = jax.ShapeDtypeStruct((nc, npair * LANES), F32)
    st_spec = pl.BlockSpec((tr, LANES), lambda pr, r: (r, pr))
    return pl.pallas_call(
        _s5_intra_kernel,
        grid=(npair, nc // tr),
        in_specs=[pl.BlockSpec((2, tr, kw), lambda pr, r: (pr, r, 0)),
                  pl.BlockSpec((2, kw, kw), lambda pr, r: (pr, 0, 0)),
                  pl.BlockSpec((2, kw, 4 * LANES), lambda pr, r: (pr, 0, 0))],
        out_specs=[pl.BlockSpec((2, tr, kw), lambda pr, r: (pr, r, 0)), st_spec, st_spec, st_spec, st_spec],
        out_shape=[jax.ShapeDtypeStruct((g, nc, kw), F32), st_shape, st_shape, st_shape, st_shape],
        compiler_params=_cparams(("parallel", "arbitrary")),
        name="s5_intra",
    )(ug, toep, ws)


def _s5_scan_kernel(decay_ref, cfr_ref, cfi_ref, cbr_ref, cbi_ref, sfr_ref, sfi_ref, sbr_ref, sbi_ref):
    n = cfr_ref.shape[0]
    w = cfr_ref.shape[1]
    afr = decay_ref[0:1, :]
    afi = decay_ref[1:2, :]
    abr = decay_ref[2:3, :]
    abi = decay_ref[3:4, :]
    zero = jnp.zeros((1, w), F32)

    def fwd(c, st):
        re, im = st
        sfr_ref[pl.ds(c, 1), :] = re
        sfi_ref[pl.ds(c, 1), :] = im
        cr = cfr_ref[pl.ds(c, 1), :]
        ci = cfi_ref[pl.ds(c, 1), :]
        return afr * re - afi * im + cr, afr * im + afi * re + ci

    lax.fori_loop(0, n, fwd, (zero, zero))

    def bwd(kk, st):
        c = n - 1 - kk
        re, im = st
        sbr_ref[pl.ds(c, 1), :] = re
        sbi_ref[pl.ds(c, 1), :] = im
        cr = cbr_ref[pl.ds(c, 1), :]
        ci = cbi_ref[pl.ds(c, 1), :]
        return abr * re - abi * im + cr, abr * im + abi * re + ci

    lax.fori_loop(0, n, bwd, (zero, zero))


def _s5_scan(decay, contribs, *, chunk0, n_seq, cps):
    w = contribs[0].shape[1]
    b0 = chunk0 // cps
    spec_in = pl.BlockSpec((cps, w), lambda b: (b0 + b, 0))
    spec_out = pl.BlockSpec((cps, w), lambda b: (b, 0))
    shape = jax.ShapeDtypeStruct((n_seq * cps, w), F32)
    return pl.pallas_call(
        _s5_scan_kernel,
        grid=(n_seq,),
        in_specs=[pl.BlockSpec((4, w), lambda b: (0, 0))] + [spec_in] * 4,
        out_specs=[spec_out] * 4,
        out_shape=[shape] * 4,
        compiler_params=_cparams(("parallel",)),
        name="s5_scan",
    )(decay, *contribs)


def _s5_out_kernel(y_ref, s0_ref, s1_ref, s2_ref, s3_ref, wc_ref, o_ref):
    st = jnp.concatenate([s0_ref[...], s1_ref[...], s2_ref[...], s3_ref[...]], axis=-1).astype(BF16)
    for k in range(2):
        y = y_ref[k] + jnp.dot(st, wc_ref[k], preferred_element_type=F32)
        o_ref[k] = 0.5 * y * (1.0 + lax.erf(y * (2.0 ** -0.5)))


def _s5_out(y_intra, states, wc, *, tr):
    g, nc, kw = y_intra.shape
    npair = g // 2
    st_spec = pl.BlockSpec((tr, LANES), lambda pr, r: (r, pr))
    return pl.pallas_call(
        _s5_out_kernel,
        grid=(npair, nc // tr),
        in_specs=[pl.BlockSpec((2, tr, kw), lambda pr, r: (pr, r, 0)), st_spec, st_spec, st_spec, st_spec,
                  pl.BlockSpec((2, 4 * LANES, kw), lambda pr, r: (pr, 0, 0))],
        out_specs=pl.BlockSpec((2, tr, kw), lambda pr, r: (pr, r, 0)),
        out_shape=jax.ShapeDtypeStruct((g, nc, kw), F32),
        compiler_params=_cparams(("parallel", "arbitrary")),
        name="s5_out",
    )(y_intra, *states, wc)


def _conv_kernel(prev_ref, cur_ref, next_ref, wdw_ref, bdw_ref, lng_ref, lnb_ref, wpw_ref, bn_ref, o_ref, scr,
                 *, n_prompt, n_seq, nb, rt):
    si = pl.program_id(0)
    i = pl.program_id(1)
    tm = cur_ref.shape[0]
    prev_ok = jnp.logical_or(i > 0, si > n_prompt)
    next_ok = jnp.logical_or(i < nb - 1, jnp.logical_and(si >= n_prompt, si < n_seq - 1))
    scr[0:HALO, :] = jnp.where(prev_ok, prev_ref[...], 0.0)
    scr[HALO:HALO + tm, :] = cur_ref[...]
    scr[HALO + tm:2 * HALO + tm, :] = jnp.where(next_ok, next_ref[...], 0.0)
    off = HALO - CONV_PAD
    for r0 in range(0, tm, rt):
        acc = wdw_ref[0:1, :] * scr[r0 + off:r0 + off + rt, :]
        for kk in range(1, CONV_KERNEL):
            acc = acc + wdw_ref[kk:kk + 1, :] * scr[r0 + off + kk:r0 + off + kk + rt, :]
        acc = acc + bdw_ref[...]
        mu = jnp.mean(acc, axis=-1, keepdims=True)
        cen = acc - mu
        var = jnp.mean(cen * cen, axis=-1, keepdims=True)
        y = cen * lax.rsqrt(var + EPS) * lng_ref[...] + lnb_ref[...]
        y = y * jax.nn.sigmoid(y)
        z = jnp.dot(y.astype(BF16), wpw_ref[...], preferred_element_type=F32)
        o_ref[r0:r0 + rt, :] = (z * _rsqrt_mean_sq(z) * bn_ref[...]).astype(o_ref.dtype)


def _conv_module(hc, w_dw, b_dw, ln_g, ln_b, w_pw_bf, bn_conv, *, n_prompt, lc, tm):
    t, cw = hc.shape
    nb = lc // tm
    s = t // lc
    hb = tm // HALO
    n_hblk = t // HALO
    kpad = 32

    def row(si, i):
        return si * nb + i

    wdw = jnp.zeros((kpad, cw), F32).at[:CONV_KERNEL].set(w_dw.astype(F32))
    vec = lambda a: a.astype(F32).reshape(1, cw)
    const = lambda shape: pl.BlockSpec(shape, lambda si, i: (0, 0))
    return pl.pallas_call(
        functools.partial(_conv_kernel, n_prompt=n_prompt, n_seq=s, nb=nb, rt=min(128, tm)),
        grid=(s, nb),
        in_specs=[pl.BlockSpec((HALO, cw), lambda si, i: (jnp.maximum(row(si, i) * hb - 1, 0), 0)),
                  pl.BlockSpec((tm, cw), lambda si, i: (row(si, i), 0)),
                  pl.BlockSpec((HALO, cw), lambda si, i: (jnp.minimum((row(si, i) + 1) * hb, n_hblk - 1), 0)),
                  const((kpad, cw)), const((1, cw)), const((1, cw)), const((1, cw)), const((cw, cw)),
                  const((1, cw))],
        out_specs=pl.BlockSpec((tm, cw), lambda si, i: (row(si, i), 0)),
        out_shape=jax.ShapeDtypeStruct((t, cw), BF16),
        scratch_shapes=[pltpu.VMEM((tm + 2 * HALO, cw), F32)],
        compiler_params=_cparams(("parallel", "arbitrary")),
        name="conv_module",
    )(hc, hc, hc, wdw, vec(b_dw), vec(ln_g), vec(ln_b), w_pw_bf, vec(bn_conv))


def _outproj_kernel(x_ref, attn_ref, ys_ref, cn_ref, mod_ref, bn_ref, wglu_ref, bglu_ref, wout_ref, npost_ref,
                    npre2_ref, rw_ref, rb_ref, x1_ref, h2_ref, ti_ref, tg_ref):
    aw = attn_ref.shape[1]
    sw = ys_ref.shape[1]
    a = attn_ref[...].astype(F32)
    an = a * _rsqrt_mean_sq(a) * bn_ref[:, 0:aw]
    y = ys_ref[...]
    z = jnp.dot(y.astype(BF16), wglu_ref[...], preferred_element_type=F32) + bglu_ref[...]
    sm = y * jax.nn.sigmoid(z)
    sn = sm * _rsqrt_mean_sq(sm) * bn_ref[:, aw:aw + sw]
    mix = (jnp.dot(an.astype(BF16), wout_ref[0:aw, :], preferred_element_type=F32)
           + jnp.dot(sn.astype(BF16), wout_ref[aw:aw + sw, :], preferred_element_type=F32)
           + jnp.dot(cn_ref[...], wout_ref[aw + sw:, :], preferred_element_type=F32))
    g1 = mod_ref[0, 2:3, :]
    sh2 = mod_ref[0, 3:4, :]
    sc2 = mod_ref[0, 4:5, :]
    x1 = x_ref[...] + g1 * (mix * _rsqrt_mean_sq(mix) * npost_ref[...])
    x1_ref[...] = x1
    h2 = x1 * _rsqrt_mean_sq(x1) * npre2_ref[...] * (1.0 + sc2) + sh2
    h2_ref[...] = h2
    logits = jnp.dot(h2, rw_ref[...], preferred_element_type=F32,
                     precision=lax.Precision.HIGHEST) + rb_ref[...]
    tm = logits.shape[0]
    lane = lax.broadcasted_iota(I32, (tm, LANES), 1)
    vals = []
    ti = jnp.zeros((tm, LANES), I32)
    for kk in range(TOP_K):
        m = jnp.max(logits, axis=1, keepdims=True)
        ik = jnp.min(jnp.where(logits == m, lane, LANES), axis=1, keepdims=True)
        vals.append(m)
        ti = jnp.where(lane == kk, ik, ti)
        logits = jnp.where(lane == ik, NEG_BIG * 2.0, logits)
    es = [jnp.exp(v - vals[0]) for v in vals]
    inv = 1.0 / (es[0] + es[1] + es[2] + es[3])
    tg = jnp.zeros((tm, LANES), F32)
    for kk in range(TOP_K):
        tg = jnp.where(lane == kk, es[kk] * inv, tg)
    ti_ref[...] = ti
    tg_ref[...] = tg


def _outproj(x, attn, ys, cn, mod, bn, wglu_bf, bglu, wout_bf, npost, npre2, rw_pad, rb_pad, *, lc, tm):
    t, d = x.shape
    nb = lc // tm
    s = t // lc
    aw, sw, cw = attn.shape[1], ys.shape[1], cn.shape[1]

    def row(si, i):
        return si * nb + i

    rows = lambda w: pl.BlockSpec((tm, w), lambda si, i: (row(si, i), 0))
    const = lambda shape: pl.BlockSpec(shape, lambda si, i: (0, 0))
    return pl.pallas_call(
        _outproj_kernel,
        grid=(s, nb),
        in_specs=[rows(d), rows(aw), rows(sw), rows(cw),
                  pl.BlockSpec((1, 6, d), lambda si, i: (si, 0, 0)),
                  const((1, d)), const((sw, sw)), const((1, sw)), const((d, d)), const((1, d)), const((1, d)),
                  const((d, LANES)), const((1, LANES))],
        out_specs=[rows(d), rows(d), rows(LANES), rows(LANES)],
        out_shape=[jax.ShapeDtypeStruct((t, d), F32), jax.ShapeDtypeStruct((t, d), F32),
                   jax.ShapeDtypeStruct((t, LANES), I32), jax.ShapeDtypeStruct((t, LANES), F32)],
        compiler_params=_cparams(("parallel", "arbitrary")),
        name="outproj_router",
    )(x, attn, ys, cn, mod, bn, wglu_bf, bglu, wout_bf, npost, npre2, rw_pad, rb_pad)


def _route(topi, blk):
    t = topi.shape[0]
    n_assign = t * TOP_K
    flat_e = topi.reshape(-1)
    order = jnp.argsort(flat_e, stable=True).astype(I32)
    onehot = (flat_e[:, None] == jnp.arange(N_EXPERTS, dtype=I32)[None, :]).astype(I32)
    counts = jnp.sum(onehot, axis=0)
    starts = jnp.cumsum(counts) - counts
    padded = ((counts + blk - 1) // blk) * blk
    ends_p = jnp.cumsum(padded)
    starts_p = ends_p - padded
    n_blocks = n_assign // blk + N_EXPERTS
    cap = n_blocks * blk
    n_used = (ends_p[-1] // blk).astype(I32)
    bidx = jnp.arange(n_blocks, dtype=I32)
    be_raw = jnp.searchsorted(ends_p, bidx * blk, side='right').astype(I32)
    last_e = jnp.max(jnp.where(counts > 0, jnp.arange(N_EXPERTS, dtype=I32), 0))
    block_e = jnp.where(bidx < n_used, jnp.minimum(be_raw, N_EXPERTS - 1), last_e)
    d = jnp.arange(cap, dtype=I32)
    e_d = block_e[d // blk]
    off = d - starts_p[e_d]
    valid = jnp.logical_and(off < counts[e_d], (d // blk) < n_used)
    src = jnp.clip(starts[e_d] + off, 0, n_assign - 1)
    tok_of = jnp.where(valid, order[src] // TOP_K, 0).astype(I32)
    rank = jnp.sum((jnp.cumsum(onehot, axis=0) - onehot) * onehot, axis=1)
    dest = (starts_p[flat_e] + rank).astype(I32)
    return tok_of, dest, block_e, n_used.reshape(1)


def _gather_kernel(idx_ref, h_hbm, o_ref, buf, sem):
    blk = buf.shape[0]

    def issue(r, carry):
        pltpu.make_async_copy(h_hbm.at[pl.ds(idx_ref[0, 0, r], 1)], buf.at[pl.ds(r, 1)], sem).start()
        return carry

    lax.fori_loop(0, blk, issue, 0)
    pltpu.make_async_copy(h_hbm.at[pl.ds(0, blk)], buf, sem).wait()
    o_ref[...] = buf[...].astype(o_ref.dtype)


def _gather_rows(h2, tok_of, *, blk):
    t, d = h2.shape
    cap = tok_of.shape[0]
    nblk = cap // blk
    return pl.pallas_call(
        _gather_kernel,
        grid=(nblk,),
        in_specs=[pl.BlockSpec((1, 1, blk), lambda b: (b, 0, 0), memory_space=pltpu.SMEM),
                  pl.BlockSpec(memory_space=pl.ANY)],
        out_specs=pl.BlockSpec((blk, d), lambda b: (b, 0)),
        out_shape=jax.ShapeDtypeStruct((cap, d), BF16),
        scratch_shapes=[pltpu.VMEM((blk, d), F32), pltpu.SemaphoreType.DMA(())],
        compiler_params=_cparams(("arbitrary",)),
        name="moe_gather",
    )(tok_of.reshape(nblk, 1, blk), h2)


def _expert_kernel(be_ref, nu_ref, xs_ref, w1_ref, b1_ref, w2_ref, b2_ref, ys_ref):
    b = pl.program_id(0)
    de = w2_ref.shape[1]

    @pl.when(b < nu_ref[0])
    def _():
        a = jnp.dot(xs_ref[...], w1_ref[0], preferred_element_type=F32) + b1_ref[0]
        gt = jnp.minimum(a[:, :de], SWIGLU_LIMIT)
        up = jnp.clip(a[:, de:], -SWIGLU_LIMIT, SWIGLU_LIMIT)
        act = (up + 1.0) * (gt * jax.nn.sigmoid(SWIGLU_ALPHA * gt))
        ys_ref[...] = jnp.dot(act.astype(BF16), w2_ref[0], preferred_element_type=F32) + b2_ref[0]

    @pl.when(b >= nu_ref[0])
    def _():
        ys_ref[...] = jnp.zeros(ys_ref.shape, ys_ref.dtype)


def _experts(xs, block_e, n_used, w1_bf, b1, w2_bf, b2, *, blk):
    cap, d = xs.shape
    ne, _, de2 = w1_bf.shape
    de = w2_bf.shape[1]
    nblk = cap // blk

    def xrow(b, be, nu):
        return (jnp.minimum(b, nu[0] - 1), 0)

    grid_spec = pltpu.PrefetchScalarGridSpec(
        num_scalar_prefetch=2,
        grid=(nblk,),
        in_specs=[pl.BlockSpec((blk, d), xrow),
                  pl.BlockSpec((1, d, de2), lambda b, be, nu: (be[b], 0, 0)),
                  pl.BlockSpec((1, 1, de2), lambda b, be, nu: (be[b], 0, 0)),
                  pl.BlockSpec((1, de, d), lambda b, be, nu: (be[b], 0, 0)),
                  pl.BlockSpec((1, 1, d), lambda b, be, nu: (be[b], 0, 0))],
        out_specs=pl.BlockSpec((blk, d), lambda b, be, nu: (b, 0)),
    )
    return pl.pallas_call(
        _expert_kernel,
        grid_spec=grid_spec,
        out_shape=jax.ShapeDtypeStruct((cap, d), F32),
        compiler_params=_cparams(("arbitrary",)),
        name="moe_experts",
    )(block_e, n_used, xs, w1_bf, b1.reshape(ne, 1, de2), w2_bf, b2.reshape(ne, 1, d))


def _combine_kernel(dest_ref, gate_ref, x1_ref, mod_ref, npost_ref, ys_hbm, o_ref, buf, sem):
    tm = x1_ref.shape[0]

    def issue(r, carry):
        for kk in range(TOP_K):
            pltpu.make_async_copy(ys_hbm.at[pl.ds(dest_ref[0, 0, r * TOP_K + kk], 1)],
                                  buf.at[kk, pl.ds(r, 1)], sem).start()
        return carry

    lax.fori_loop(0, tm, issue, 0)
    for kk in range(TOP_K):
        pltpu.make_async_copy(ys_hbm.at[pl.ds(0, tm)], buf.at[kk], sem).wait()
    gate = gate_ref[...]
    ff = gate[:, 0:1] * buf[0]
    for kk in range(1, TOP_K):
        ff = ff + gate[:, kk:kk + 1] * buf[kk]
    g2 = mod_ref[0, 5:6, :]
    o_ref[...] = x1_ref[...] + g2 * (ff * _rsqrt_mean_sq(ff) * npost_ref[...])


def _combine(ys, dest, gate, x1, mod, npost, *, lc, tm):
    t, d = x1.shape
    nb = lc // tm
    s = t // lc

    def row(si, i):
        return si * nb + i

    return pl.pallas_call(
        _combine_kernel,
        grid=(s, nb),
        in_specs=[pl.BlockSpec((1, 1, tm * TOP_K), lambda si, i: (row(si, i), 0, 0), memory_space=pltpu.SMEM),
                  pl.BlockSpec((tm, LANES), lambda si, i: (row(si, i), 0)),
                  pl.BlockSpec((tm, d), lambda si, i: (row(si, i), 0)),
                  pl.BlockSpec((1, 6, d), lambda si, i: (si, 0, 0)),
                  pl.BlockSpec((1, d), lambda si, i: (0, 0)),
                  pl.BlockSpec(memory_space=pl.ANY)],
        out_specs=pl.BlockSpec((tm, d), lambda si, i: (row(si, i), 0)),
        out_shape=jax.ShapeDtypeStruct((t, d), F32),
        scratch_shapes=[pltpu.VMEM((TOP_K, tm, d), F32), pltpu.SemaphoreType.DMA(())],
        compiler_params=_cparams(("arbitrary", "arbitrary")),
        name="moe_combine",
    )(dest.reshape(t // tm, 1, tm * TOP_K), gate, x1, mod, npost, ys)


def _rope_tables(n_pos):
    rows = n_pos // GRID_W
    row_idx = jnp.repeat(jnp.arange(rows, dtype=F32), GRID_W)
    col_idx = jnp.tile(jnp.arange(GRID_W, dtype=F32), rows)
    inv = 1.0 / (ROPE_THETA ** (jnp.arange(0, ROPE_AXIS_DIM, 2, dtype=F32) / ROPE_AXIS_DIM))
    ar = row_idx[:, None] * inv[None, :]
    ac = col_idx[:, None] * inv[None, :]
    cos = jnp.concatenate([jnp.cos(ar), jnp.cos(ar), jnp.cos(ac), jnp.cos(ac)], axis=-1)
    sin = jnp.concatenate([-jnp.sin(ar), jnp.sin(ar), -jnp.sin(ac), jnp.sin(ac)], axis=-1)
    reps = LANES // HEAD_DIM
    return jnp.tile(cos, (1, reps)), jnp.tile(sin, (1, reps))


def _tiles(lc):
    tm = min(512, lc)
    tq = min(256, lc)
    tk = min(512, lc)
    return tm, tq, tk


def kernel(x_prompt, x_sample, c_prompt, c_sample, w_ada, b_ada, norm_pre_mix, norm_post_mix, norm_pre_ffn, norm_post_ffn, w_in, q_norm, k_norm, ssm_lam_re, ssm_lam_im, ssm_log_dt, ssm_b_re, ssm_b_im, ssm_c_re, ssm_c_im, ssm_d, ssm_w_glu, ssm_b_glu, conv_w_dw, conv_b_dw, conv_ln_g, conv_ln_b, conv_w_pw, branch_norm, w_out, router_w, router_b, expert_w1, expert_b1, expert_w2, expert_b2):
    bsz, lc, d = x_prompt.shape
    dbsz, dl, _ = x_sample.shape
    assert dbsz == 1 and dl % lc == 0 and lc % S5_CHUNK == 0 and lc % GRID_W == 0
    n_sample = dl // lc
    n_seq = bsz + n_sample
    t_prompt = bsz * lc
    t = t_prompt + dl
    depth = w_ada.shape[0]
    ssm_w = ssm_d.shape[1]
    conv_w = conv_b_dw.shape[1]
    n_groups = ssm_w // SSM_GROUP
    tm, tq, tk = _tiles(lc)
    tc = S5_CHUNK
    nc = t // tc
    cps = lc // tc
    kw = tc * SSM_GROUP
    assert t_prompt % dl == 0
    blk = min(MOE_BLOCK, t * TOP_K // N_EXPERTS)

    x = jnp.concatenate([x_prompt.reshape(t_prompt, d), x_sample.reshape(dl, d)], axis=0)
    c_all = jnp.concatenate([c_prompt, jnp.repeat(c_sample, n_sample, axis=0)], axis=0)
    cos_t, sin_t = _rope_tables(max(lc, dl))
    bd = (jnp.arange(LANES)[:, None] // HEAD_DIM == jnp.arange(LANES)[None, :] // HEAD_DIM).astype(BF16)
    reps = LANES // HEAD_DIM
    vec = lambda a: a.astype(F32).reshape(1, -1)

    for li in range(depth):
        mod = _modulation(c_all, w_ada[li], b_ada[li]).reshape(n_seq, 6, d)
        qkn = jnp.concatenate([jnp.tile(vec(q_norm[li]) * (HEAD_DIM ** -0.5), (ATTN_W // LANES, reps)),
                               jnp.tile(vec(k_norm[li]), (1, reps)),
                               jnp.zeros((8 - ATTN_W // LANES - 1, LANES), F32)], axis=0)
        q, k, v, u, hc = _inproj(x, mod, vec(norm_pre_mix[li]), w_in[li].astype(BF16), qkn, cos_t, sin_t, bd,
                                 n_prompt=bsz, lc=lc, tm=tm, ssm_w=ssm_w, conv_w=conv_w)

        attn = jnp.concatenate([
            _attention(q, k, v, row0=0, n_seq=bsz, lkv=lc, tq=tq, tk=tk),
            _attention(q, k, v, row0=t_prompt, n_seq=1, lkv=dl, tq=tq, tk=tk)], axis=0)

        toep, ws, wc, decay = _s5_weights(ssm_lam_re[li], ssm_lam_im[li], ssm_log_dt[li], ssm_b_re[li],
                                          ssm_b_im[li], ssm_c_re[li], ssm_c_im[li], ssm_d[li], tc)
        ug = u.reshape(nc, tc, n_groups, SSM_GROUP).transpose(2, 0, 1, 3).reshape(n_groups, nc, kw)
        tr = math.gcd(nc, 384)
        y_intra, *contribs = _s5_intra(ug, toep, ws, tr=tr)
        st_p = _s5_scan(decay, contribs, chunk0=0, n_seq=bsz, cps=cps)
        st_s = _s5_scan(decay, contribs, chunk0=t_prompt // tc, n_seq=1, cps=dl // tc)
        states = [jnp.concatenate([a, b], axis=0) for a, b in zip(st_p, st_s)]
        yg = _s5_out(y_intra, states, wc, tr=tr)
        ys = yg.reshape(n_groups, nc, tc, SSM_GROUP).transpose(1, 2, 0, 3).reshape(t, ssm_w)

        bn = vec(branch_norm[li])
        cn = _conv_module(hc, conv_w_dw[li], conv_b_dw[li], conv_ln_g[li], conv_ln_b[li],
                          conv_w_pw[li].astype(BF16), bn[0, ATTN_W + ssm_w:], n_prompt=bsz, lc=lc, tm=tm)

        rw_pad = jnp.zeros((d, LANES), F32).at[:, :N_EXPERTS].set(router_w[li].astype(F32))
        rb_pad = jnp.full((1, LANES), NEG_BIG, F32).at[0, :N_EXPERTS].set(router_b[li].astype(F32))
        x1, h2, topi, topg = _outproj(x, attn, ys, cn, mod, bn, ssm_w_glu[li].astype(BF16), vec(ssm_b_glu[li]),
                                      w_out[li].astype(BF16), vec(norm_post_mix[li]), vec(norm_pre_ffn[li]),
                                      rw_pad, rb_pad, lc=lc, tm=tm)

        tok_of, dest, block_e, n_used = _route(topi[:, :TOP_K], blk)
        xs = _gather_rows(h2, tok_of, blk=blk)
        ye = _experts(xs, block_e, n_used, expert_w1[li].astype(BF16), expert_b1[li].astype(F32),
                      expert_w2[li].astype(BF16), expert_b2[li].astype(F32), blk=blk)
        x = _combine(ye, dest, topg, x1, mod, vec(norm_post_ffn[li]), lc=lc, tm=tm)

    return (x[:t_prompt].reshape(bsz, lc, d), x[t_prompt:].reshape(dbsz, dl, d))
```

```python
import functools
import math

import jax
import jax.numpy as jnp
from jax import lax
from jax.experimental import pallas as pl
from jax.experimental.pallas import tpu as pltpu

F32 = jnp.float32
BF16 = jnp.bfloat16
I32 = jnp.int32

HEAD_DIM = 64
N_HEADS = 8
N_KV_HEADS = 2
GQA_GROUP = N_HEADS // N_KV_HEADS
ATTN_W = N_HEADS * HEAD_DIM
KV_W = N_KV_HEADS * HEAD_DIM
SSM_GROUP = 16
SSM_STATE = 64
CONV_KERNEL = 31
CONV_PAD = CONV_KERNEL // 2
GRID_W = 64
ROPE_AXIS_DIM = HEAD_DIM // 2
ROPE_THETA = 10000.0
N_EXPERTS = 32
TOP_K = 4
SWIGLU_LIMIT = 7.0
SWIGLU_ALPHA = 1.702
EPS = 1e-6

LANES = 128
HALO = 16
S5_CHUNK = 64
MOE_BLOCK = 512
NEG_BIG = -1e30
VMEM_LIMIT = 48 * 1024 * 1024
EXPERT_VMEM_LIMIT = 56 * 1024 * 1024


def _cparams(sem, vmem_limit_bytes=VMEM_LIMIT, **kw):
    return pltpu.CompilerParams(dimension_semantics=sem, vmem_limit_bytes=vmem_limit_bytes, **kw)


def _rsqrt_mean_sq(x):
    return lax.rsqrt(jnp.mean(x * x, axis=-1, keepdims=True) + EPS)


def _mod_kernel(c_ref, w_ref, b_ref, o_ref):
    c = c_ref[...]
    s = c * jax.nn.sigmoid(c)
    o_ref[...] = jnp.dot(s, w_ref[...], preferred_element_type=F32,
                         precision=lax.Precision.HIGHEST) + b_ref[...]


def _modulation(c_all, w_ada, b_ada):
    s, d = c_all.shape
    n = w_ada.shape[1]
    tn = n // 4
    return pl.pallas_call(
        _mod_kernel,
        grid=(n // tn,),
        in_specs=[pl.BlockSpec((s, d), lambda j: (0, 0)),
                  pl.BlockSpec((d, tn), lambda j: (0, j)),
                  pl.BlockSpec((1, tn), lambda j: (0, j))],
        out_specs=pl.BlockSpec((s, tn), lambda j: (0, j)),
        out_shape=jax.ShapeDtypeStruct((s, n), F32),
        compiler_params=_cparams(("arbitrary",)),
        name="adaln_mod",
    )(c_all, w_ada, b_ada.reshape(1, n))


def _inproj_kernel(x_ref, mod_ref, npre_ref, w_ref, qkn_ref, cos_ref, sin_ref, bd_ref,
                   q_ref, k_ref, v_ref, u_ref, hc_ref):
    x = x_ref[...]
    sh = mod_ref[0, 0:1, :]
    sc = mod_ref[0, 1:2, :]
    h = x * _rsqrt_mean_sq(x) * npre_ref[...] * (1.0 + sc) + sh
    proj = jnp.dot(h.astype(BF16), w_ref[...], preferred_element_type=F32)
    tm = x.shape[0]
    cos = cos_ref[...]
    sin = sin_ref[...]
    lane = lax.broadcasted_iota(I32, (tm, LANES), 1)
    first_half = (lane % ROPE_AXIS_DIM) < (ROPE_AXIS_DIM // 2)
    n_qk = (ATTN_W + KV_W) // LANES
    for c in range(n_qk):
        xc = proj[:, c * LANES:(c + 1) * LANES]
        ms = jnp.dot((xc * xc).astype(BF16), bd_ref[...], preferred_element_type=F32) * (1.0 / HEAD_DIM)
        xn = xc * lax.rsqrt(ms + EPS) * qkn_ref[c:c + 1, :]
        half = ROPE_AXIS_DIM // 2
        swapped = jnp.where(first_half, pltpu.roll(xn, LANES - half, 1), pltpu.roll(xn, half, 1))
        rot = (xn * cos + swapped * sin).astype(BF16)
        if c < ATTN_W // LANES:
            q_ref[2 * c] = rot[:, :HEAD_DIM]
            q_ref[2 * c + 1] = rot[:, HEAD_DIM:]
        else:
            k_ref[0] = rot[:, :HEAD_DIM]
            k_ref[1] = rot[:, HEAD_DIM:]
    o2 = ATTN_W + KV_W
    v = proj[:, o2:o2 + KV_W].astype(BF16)
    v_ref[0] = v[:, :HEAD_DIM]
    v_ref[1] = v[:, HEAD_DIM:]
    o3 = o2 + KV_W
    ssm_w = u_ref.shape[1]
    u_ref[...] = proj[:, o3:o3 + ssm_w].astype(BF16)
    o4 = o3 + ssm_w
    conv_w = hc_ref.shape[1]
    a = proj[:, o4:o4 + conv_w]
    gate = proj[:, o4 + conv_w:o4 + 2 * conv_w]
    hc_ref[...] = a * jax.nn.sigmoid(gate)


def _inproj(x, mod, norm_pre, w_in_bf, qkn, cos_t, sin_t, bd, *, n_prompt, lc, tm, ssm_w, conv_w):
    t, d = x.shape
    nb = lc // tm
    s = t // lc
    in_w = w_in_bf.shape[1]

    def row(si, i):
        return si * nb + i

    def pos(si, i):
        return jnp.where(si < n_prompt, i, (si - n_prompt) * nb + i)

    return pl.pallas_call(
        _inproj_kernel,
        grid=(s, nb),
        in_specs=[pl.BlockSpec((tm, d), lambda si, i: (row(si, i), 0)),
                  pl.BlockSpec((1, 6, d), lambda si, i: (si, 0, 0)),
                  pl.BlockSpec((1, d), lambda si, i: (0, 0)),
                  pl.BlockSpec((d, in_w), lambda si, i: (0, 0)),
                  pl.BlockSpec((8, LANES), lambda si, i: (0, 0)),
                  pl.BlockSpec((tm, LANES), lambda si, i: (pos(si, i), 0)),
                  pl.BlockSpec((tm, LANES), lambda si, i: (pos(si, i), 0)),
                  pl.BlockSpec((LANES, LANES), lambda si, i: (0, 0))],
        out_specs=[pl.BlockSpec((N_HEADS, tm, HEAD_DIM), lambda si, i: (0, row(si, i), 0)),
                   pl.BlockSpec((N_KV_HEADS, tm, HEAD_DIM), lambda si, i: (0, row(si, i), 0)),
                   pl.BlockSpec((N_KV_HEADS, tm, HEAD_DIM), lambda si, i: (0, row(si, i), 0)),
                   pl.BlockSpec((tm, ssm_w), lambda si, i: (row(si, i), 0)),
                   pl.BlockSpec((tm, conv_w), lambda si, i: (row(si, i), 0))],
        out_shape=[jax.ShapeDtypeStruct((N_HEADS, t, HEAD_DIM), BF16),
                   jax.ShapeDtypeStruct((N_KV_HEADS, t, HEAD_DIM), BF16),
                   jax.ShapeDtypeStruct((N_KV_HEADS, t, HEAD_DIM), BF16),
                   jax.ShapeDtypeStruct((t, ssm_w), BF16),
                   jax.ShapeDtypeStruct((t, conv_w), F32)],
        compiler_params=_cparams(("parallel", "arbitrary")),
        name="inproj",
    )(x, mod, norm_pre, w_in_bf, qkn, cos_t, sin_t, bd)


def _attn_kernel(q_ref, k_ref, v_ref, o_ref, s_ref, m_ref, l_ref, acc_ref, *, tk):
    g, tq, hd = q_ref.shape
    rows = g * tq
    q = q_ref[...].reshape(rows, hd)
    n = k_ref.shape[1] // tk
    m_ref[...] = jnp.full(m_ref.shape, NEG_BIG, F32)
    l_ref[...] = jnp.zeros(l_ref.shape, F32)
    acc_ref[...] = jnp.zeros(acc_ref.shape, F32)

    def scores(j):
        kc = k_ref[0, pl.ds(pl.multiple_of(j * tk, tk), tk), :]
        return lax.dot_general(q, kc, (((1,), (1,)), ((), ())), preferred_element_type=F32)

    def step(j, slot, prefetch):
        if prefetch:
            s_ref[1 - slot] = scores(j + 1)
        s = s_ref[slot]
        vc = v_ref[0, pl.ds(pl.multiple_of(j * tk, tk), tk), :]
        m_prev = m_ref[...]
        m_next = jnp.maximum(m_prev, jnp.max(s, axis=1, keepdims=True))
        alpha = jnp.exp(m_prev - m_next)
        p = jnp.exp(s - jnp.tile(m_next, (1, tk // LANES)))
        l_ref[...] = alpha * l_ref[...] + jnp.sum(p, axis=1, keepdims=True)
        acc_ref[...] = acc_ref[...] * alpha[:, :hd] + jnp.dot(p.astype(BF16), vc, preferred_element_type=F32)
        m_ref[...] = m_next

    s_ref[0] = scores(0)
    n_pairs = (n - 1) // 2

    def body(jj, carry):
        step(2 * jj, 0, True)
        step(2 * jj + 1, 1, True)
        return carry

    lax.fori_loop(0, n_pairs, body, 0)
    if n - 2 * n_pairs == 2:
        step(2 * n_pairs, 0, True)
        step(2 * n_pairs + 1, 1, False)
    else:
        step(2 * n_pairs, 0, False)
    inv = 1.0 / l_ref[...]
    o = (acc_ref[...] * inv[:, :hd]).reshape(g, tq, hd)
    o_ref[...] = jnp.concatenate([o[i] for i in range(g)], axis=-1).astype(o_ref.dtype)


def _attention(q, k, v, *, row0, n_seq, lkv, tq, tk):
    nq = lkv // tq
    qb0 = row0 // tq
    kb0 = row0 // lkv
    rows = GQA_GROUP * tq
    return pl.pallas_call(
        functools.partial(_attn_kernel, tk=tk),
        grid=(n_seq, N_KV_HEADS, nq),
        in_specs=[pl.BlockSpec((GQA_GROUP, tq, HEAD_DIM), lambda b, g, i: (g, qb0 + b * nq + i, 0)),
                  pl.BlockSpec((1, lkv, HEAD_DIM), lambda b, g, i: (g, kb0 + b, 0)),
                  pl.BlockSpec((1, lkv, HEAD_DIM), lambda b, g, i: (g, kb0 + b, 0))],
        out_specs=pl.BlockSpec((tq, GQA_GROUP * HEAD_DIM), lambda b, g, i: (b * nq + i, g)),
        out_shape=jax.ShapeDtypeStruct((n_seq * lkv, ATTN_W), BF16),
        scratch_shapes=[pltpu.VMEM((2, rows, tk), F32), pltpu.VMEM((rows, LANES), F32),
                        pltpu.VMEM((rows, LANES), F32), pltpu.VMEM((rows, HEAD_DIM), F32)],
        compiler_params=_cparams(("parallel", "parallel", "arbitrary")),
        name="attention",
    )(q, k, v)


def _s5_weights(lam_re, lam_im, log_dt, b_re, b_im, c_re, c_im, d_skip, tc):
    hp = lax.Precision.HIGHEST
    lam = lax.complex(lam_re.astype(F32), lam_im.astype(F32))
    dt = jnp.exp(log_dt.astype(F32))[..., None]
    lam_dt = lam * dt
    lam_bar = jnp.exp(lam_dt)
    b_bar = ((lam_bar - 1.0) / lam)[..., None] * lax.complex(b_re.astype(F32), b_im.astype(F32))
    cmat = lax.complex(c_re.astype(F32), c_im.astype(F32))
    g, p = lam.shape[1], lam.shape[2]
    h = b_bar.shape[-1]
    taus = jnp.arange(tc + 1, dtype=F32)
    pw = jnp.exp(lam_dt[..., None] * taus)
    kern = jnp.einsum('dgop,dgpt,dgpi->dgtoi', cmat, pw, b_bar, precision=hp).real
    jj = jnp.arange(tc)[:, None]
    ii = jnp.arange(tc)[None, :]
    lag = ii - jj
    kf = kern[0][:, jnp.clip(lag, 0, tc)]
    kb = kern[1][:, jnp.clip(-lag, 0, tc)]
    eye_h = jnp.eye(h, dtype=F32)
    kd = kern[0][:, 0] + kern[1][:, 0] + d_skip.astype(F32).reshape(g, h)[:, :, None] * eye_h
    lag5 = lag[None, :, :, None, None]
    full = jnp.where(lag5 > 0, kf, jnp.where(lag5 < 0, kb, kd[:, None, None]))
    toep = full.transpose(0, 1, 4, 2, 3).reshape(g, tc * h, tc * h)

    jv = jnp.arange(tc)
    pf = pw[0][:, :, tc - 1 - jv]
    pb = pw[1][:, :, jv]
    sf = pf[:, :, :, None] * b_bar[0][:, :, None, :]
    sb = pb[:, :, :, None] * b_bar[1][:, :, None, :]
    parts = [sf.real, sf.imag, sb.real, sb.imag]
    ws = jnp.stack([x.transpose(0, 2, 3, 1).reshape(g, tc * h, p) for x in parts], axis=2)
    odd = (jnp.arange(g) % 2)[:, None, None, None]
    zeros = jnp.zeros_like(ws)
    ws = jnp.where(odd == 0, jnp.concatenate([ws, zeros], -1), jnp.concatenate([zeros, ws], -1))

    ef = cmat[0].transpose(0, 2, 1)[:, :, None, :] * pw[0][:, :, 1 + jv][:, :, :, None]
    eb = cmat[1].transpose(0, 2, 1)[:, :, None, :] * pw[1][:, :, tc - jv][:, :, :, None]
    cparts = [ef.real, -ef.imag, eb.real, -eb.imag]
    wc = jnp.stack([x.reshape(g, p, tc * h) for x in cparts], axis=1)
    zc = jnp.zeros_like(wc)
    odd4 = (jnp.arange(g) % 2)[:, None, None, None]
    wc = jnp.where(odd4 == 0, jnp.concatenate([wc, zc], 2), jnp.concatenate([zc, wc], 2))
    wc = wc.reshape(g, 4 * 2 * p, tc * h)

    af = pw[0][:, :, tc].reshape(1, g * p)
    ab = pw[1][:, :, tc].reshape(1, g * p)
    decay = jnp.concatenate([af.real, af.imag, ab.real, ab.imag], axis=0)
    return toep.astype(BF16), ws.reshape(g, tc * h, 4 * 2 * p).astype(BF16), wc.astype(BF16), decay


def _s5_intra_kernel(u_ref, toep_ref, ws_ref, y_ref, s0_ref, s1_ref, s2_ref, s3_ref):
    u0 = u_ref[0]
    u1 = u_ref[1]
    y_ref[0] = jnp.dot(u0, toep_ref[0], preferred_element_type=F32)
    y_ref[1] = jnp.dot(u1, toep_ref[1], preferred_element_type=F32)
    st = (jnp.dot(u0, ws_ref[0], preferred_element_type=F32)
          + jnp.dot(u1, ws_ref[1], preferred_element_type=F32))
    for k, ref in enumerate((s0_ref, s1_ref, s2_ref, s3_ref)):
        ref[...] = st[:, k * LANES:(k + 1) * LANES]


def _s5_intra(ug, toep, ws, *, tr):
    g, nc, kw = ug.shape
    npair = g // 2
    st_shape = jax.ShapeDtypeStruct((nc, npair * LANES), F32)
    st_spec = pl.BlockSpec((tr, LANES), lambda pr, r: (r, pr))
    return pl.pallas_call(
        _s5_intra_kernel,
        grid=(npair, nc // tr),
        in_specs=[pl.BlockSpec((2, tr, kw), lambda pr, r: (pr, r, 0)),
                  pl.BlockSpec((2, kw, kw), lambda pr, r: (pr, 0, 0)),
                  pl.BlockSpec((2, kw, 4 * LANES), lambda pr, r: (pr, 0, 0))],
        out_specs=[pl.BlockSpec((2, tr, kw), lambda pr, r: (pr, r, 0)), st_spec, st_spec, st_spec, st_spec],
        out_shape=[jax.ShapeDtypeStruct((g, nc, kw), F32), st_shape, st_shape, st_shape, st_shape],
        compiler_params=_cparams(("parallel", "arbitrary")),
        name="s5_intra",
    )(ug, toep, ws)


def _s5_scan_kernel(decay_ref, cfr_ref, cfi_ref, cbr_ref, cbi_ref, sfr_ref, sfi_ref, sbr_ref, sbi_ref):
    n = cfr_ref.shape[0]
    w = cfr_ref.shape[1]
    afr = decay_ref[0:1, :]
    afi = decay_ref[1:2, :]
    abr = decay_ref[2:3, :]
    abi = decay_ref[3:4, :]
    zero = jnp.zeros((1, w), F32)

    def fwd(c, st):
        re, im = st
        sfr_ref[pl.ds(c, 1), :] = re
        sfi_ref[pl.ds(c, 1), :] = im
        cr = cfr_ref[pl.ds(c, 1), :]
        ci = cfi_ref[pl.ds(c, 1), :]
        return afr * re - afi * im + cr, afr * im + afi * re + ci

    lax.fori_loop(0, n, fwd, (zero, zero))

    def bwd(kk, st):
        c = n - 1 - kk
        re, im = st
        sbr_ref[pl.ds(c, 1), :] = re
        sbi_ref[pl.ds(c, 1), :] = im
        cr = cbr_ref[pl.ds(c, 1), :]
        ci = cbi_ref[pl.ds(c, 1), :]
        return abr * re - abi * im + cr, abr * im + abi * re + ci

    lax.fori_loop(0, n, bwd, (zero, zero))


def _s5_scan(decay, contribs, *, chunk0, n_seq, cps):
    w = contribs[0].shape[1]
    b0 = chunk0 // cps
    spec_in = pl.BlockSpec((cps, w), lambda b: (b0 + b, 0))
    spec_out = pl.BlockSpec((cps, w), lambda b: (b, 0))
    shape = jax.ShapeDtypeStruct((n_seq * cps, w), F32)
    return pl.pallas_call(
        _s5_scan_kernel,
        grid=(n_seq,),
        in_specs=[pl.BlockSpec((4, w), lambda b: (0, 0))] + [spec_in] * 4,
        out_specs=[spec_out] * 4,
        out_shape=[shape] * 4,
        compiler_params=_cparams(("parallel",)),
        name="s5_scan",
    )(decay, *contribs)


def _s5_out_kernel(y_ref, s0_ref, s1_ref, s2_ref, s3_ref, wc_ref, o_ref):
    st = jnp.concatenate([s0_ref[...], s1_ref[...], s2_ref[...], s3_ref[...]], axis=-1).astype(BF16)
    for k in range(2):
        y = y_ref[k] + jnp.dot(st, wc_ref[k], preferred_element_type=F32)
        o_ref[k] = 0.5 * y * (1.0 + lax.erf(y * (2.0 ** -0.5)))


def _s5_out(y_intra, states, wc, *, tr):
    g, nc, kw = y_intra.shape
    npair = g // 2
    st_spec = pl.BlockSpec((tr, LANES), lambda pr, r: (r, pr))
    return pl.pallas_call(
        _s5_out_kernel,
        grid=(npair, nc // tr),
        in_specs=[pl.BlockSpec((2, tr, kw), lambda pr, r: (pr, r, 0)), st_spec, st_spec, st_spec, st_spec,
                  pl.BlockSpec((2, 4 * LANES, kw), lambda pr, r: (pr, 0, 0))],
        out_specs=pl.BlockSpec((2, tr, kw), lambda pr, r: (pr, r, 0)),
        out_shape=jax.ShapeDtypeStruct((g, nc, kw), F32),
        compiler_params=_cparams(("parallel", "arbitrary")),
        name="s5_out",
    )(y_intra, *states, wc)


def _conv_kernel(prev_ref, cur_ref, next_ref, wdw_ref, bdw_ref, lng_ref, lnb_ref, wpw_ref, bn_ref, o_ref, scr,
                 *, n_prompt, n_seq, nb, rt):
    si = pl.program_id(0)
    i = pl.program_id(1)
    tm = cur_ref.shape[0]
    prev_ok = jnp.logical_or(i > 0, si > n_prompt)
    next_ok = jnp.logical_or(i < nb - 1, jnp.logical_and(si >= n_prompt, si < n_seq - 1))
    scr[0:HALO, :] = jnp.where(prev_ok, prev_ref[...], 0.0)
    scr[HALO:HALO + tm, :] = cur_ref[...]
    scr[HALO + tm:2 * HALO + tm, :] = jnp.where(next_ok, next_ref[...], 0.0)
    off = HALO - CONV_PAD
    for r0 in range(0, tm, rt):
        acc = wdw_ref[0:1, :] * scr[r0 + off:r0 + off + rt, :]
        for kk in range(1, CONV_KERNEL):
            acc = acc + wdw_ref[kk:kk + 1, :] * scr[r0 + off + kk:r0 + off + kk + rt, :]
        acc = acc + bdw_ref[...]
        mu = jnp.mean(acc, axis=-1, keepdims=True)
        cen = acc - mu
        var = jnp.mean(cen * cen, axis=-1, keepdims=True)
        y = cen * lax.rsqrt(var + EPS) * lng_ref[...] + lnb_ref[...]
        y = y * jax.nn.sigmoid(y)
        z = jnp.dot(y.astype(BF16), wpw_ref[...], preferred_element_type=F32)
        o_ref[r0:r0 + rt, :] = (z * _rsqrt_mean_sq(z) * bn_ref[...]).astype(o_ref.dtype)


def _conv_module(hc, w_dw, b_dw, ln_g, ln_b, w_pw_bf, bn_conv, *, n_prompt, lc, tm):
    t, cw = hc.shape
    nb = lc // tm
    s = t // lc
    hb = tm // HALO
    n_hblk = t // HALO
    kpad = 32

    def row(si, i):
        return si * nb + i

    wdw = jnp.zeros((kpad, cw), F32).at[:CONV_KERNEL].set(w_dw.astype(F32))
    vec = lambda a: a.astype(F32).reshape(1, cw)
    const = lambda shape: pl.BlockSpec(shape, lambda si, i: (0, 0))
    return pl.pallas_call(
        functools.partial(_conv_kernel, n_prompt=n_prompt, n_seq=s, nb=nb, rt=min(128, tm)),
        grid=(s, nb),
        in_specs=[pl.BlockSpec((HALO, cw), lambda si, i: (jnp.maximum(row(si, i) * hb - 1, 0), 0)),
                  pl.BlockSpec((tm, cw), lambda si, i: (row(si, i), 0)),
                  pl.BlockSpec((HALO, cw), lambda si, i: (jnp.minimum((row(si, i) + 1) * hb, n_hblk - 1), 0)),
                  const((kpad, cw)), const((1, cw)), const((1, cw)), const((1, cw)), const((cw, cw)),
                  const((1, cw))],
        out_specs=pl.BlockSpec((tm, cw), lambda si, i: (row(si, i), 0)),
        out_shape=jax.ShapeDtypeStruct((t, cw), BF16),
        scratch_shapes=[pltpu.VMEM((tm + 2 * HALO, cw), F32)],
        compiler_params=_cparams(("parallel", "arbitrary")),
        name="conv_module",
    )(hc, hc, hc, wdw, vec(b_dw), vec(ln_g), vec(ln_b), w_pw_bf, vec(bn_conv))


def _outproj_kernel(x_ref, attn_ref, ys_ref, cn_ref, mod_ref, bn_ref, wglu_ref, bglu_ref, wout_ref, npost_ref,
                    npre2_ref, rw_ref, rb_ref, x1_ref, h2_ref, ti_ref, tg_ref):
    aw = attn_ref.shape[1]
    sw = ys_ref.shape[1]
    a = attn_ref[...].astype(F32)
    an = a * _rsqrt_mean_sq(a) * bn_ref[:, 0:aw]
    y = ys_ref[...]
    z = jnp.dot(y.astype(BF16), wglu_ref[...], preferred_element_type=F32) + bglu_ref[...]
    sm = y * jax.nn.sigmoid(z)
    sn = sm * _rsqrt_mean_sq(sm) * bn_ref[:, aw:aw + sw]
    mix = (jnp.dot(an.astype(BF16), wout_ref[0:aw, :], preferred_element_type=F32)
           + jnp.dot(sn.astype(BF16), wout_ref[aw:aw + sw, :], preferred_element_type=F32)
           + jnp.dot(cn_ref[...], wout_ref[aw + sw:, :], preferred_element_type=F32))
    g1 = mod_ref[0, 2:3, :]
    sh2 = mod_ref[0, 3:4, :]
    sc2 = mod_ref[0, 4:5, :]
    x1 = x_ref[...] + g1 * (mix * _rsqrt_mean_sq(mix) * npost_ref[...])
    x1_ref[...] = x1
    h2 = x1 * _rsqrt_mean_sq(x1) * npre2_ref[...] * (1.0 + sc2) + sh2
    h2_ref[...] = h2
    logits = jnp.dot(h2, rw_ref[...], preferred_element_type=F32,
                     precision=lax.Precision.HIGHEST) + rb_ref[...]
    tm = logits.shape[0]
    lane = lax.broadcasted_iota(I32, (tm, LANES), 1)
    vals = []
    ti = jnp.zeros((tm, LANES), I32)
    for kk in range(TOP_K):
        m = jnp.max(logits, axis=1, keepdims=True)
        ik = jnp.min(jnp.where(logits == m, lane, LANES), axis=1, keepdims=True)
        vals.append(m)
        ti = jnp.where(lane == kk, ik, ti)
        logits = jnp.where(lane == ik, NEG_BIG * 2.0, logits)
    es = [jnp.exp(v - vals[0]) for v in vals]
    inv = 1.0 / (es[0] + es[1] + es[2] + es[3])
    tg = jnp.zeros((tm, LANES), F32)
    for kk in range(TOP_K):
        tg = jnp.where(lane == kk, es[kk] * inv, tg)
    ti_ref[...] = ti
    tg_ref[...] = tg


def _outproj(x, attn, ys, cn, mod, bn, wglu_bf, bglu, wout_bf, npost, npre2, rw_pad, rb_pad, *, lc, tm):
    t, d = x.shape
    nb = lc // tm
    s = t // lc
    aw, sw, cw = attn.shape[1], ys.shape[1], cn.shape[1]

    def row(si, i):
        return si * nb + i

    rows = lambda w: pl.BlockSpec((tm, w), lambda si, i: (row(si, i), 0))
    const = lambda shape: pl.BlockSpec(shape, lambda si, i: (0, 0))
    return pl.pallas_call(
        _outproj_kernel,
        grid=(s, nb),
        in_specs=[rows(d), rows(aw), rows(sw), rows(cw),
                  pl.BlockSpec((1, 6, d), lambda si, i: (si, 0, 0)),
                  const((1, d)), const((sw, sw)), const((1, sw)), const((d, d)), const((1, d)), const((1, d)),
                  const((d, LANES)), const((1, LANES))],
        out_specs=[rows(d), rows(d), rows(LANES), rows(LANES)],
        out_shape=[jax.ShapeDtypeStruct((t, d), F32), jax.ShapeDtypeStruct((t, d), F32),
                   jax.ShapeDtypeStruct((t, LANES), I32), jax.ShapeDtypeStruct((t, LANES), F32)],
        compiler_params=_cparams(("parallel", "arbitrary")),
        name="outproj_router",
    )(x, attn, ys, cn, mod, bn, wglu_bf, bglu, wout_bf, npost, npre2, rw_pad, rb_pad)


def _route(topi, blk):
    t = topi.shape[0]
    n_assign = t * TOP_K
    flat_e = topi.reshape(-1)
    eids = jnp.arange(N_EXPERTS, dtype=I32)
    onehot = (flat_e[:, None] == eids[None, :]).astype(I32)
    csum = jnp.cumsum(onehot, axis=0)
    counts = csum[-1]
    rank = jnp.sum((csum - onehot) * onehot, axis=1)
    padded = ((counts + blk - 1) // blk) * blk
    ends_p = jnp.cumsum(padded)
    starts_p = ends_p - padded
    dest = (jnp.sum(onehot * starts_p[None, :], axis=1) + rank).astype(I32)
    n_blocks = n_assign // blk + N_EXPERTS
    cap = n_blocks * blk
    n_used = (ends_p[-1] // blk).astype(I32)
    bidx = jnp.arange(n_blocks, dtype=I32)
    be_raw = jnp.sum((ends_p[None, :] <= (bidx * blk)[:, None]).astype(I32), axis=1)
    last_e = jnp.max(jnp.where(counts > 0, eids, 0))
    block_e = jnp.where(bidx < n_used, jnp.minimum(be_raw, N_EXPERTS - 1), last_e).astype(I32)
    fill_base = jnp.concatenate([starts_p + counts, ends_p[-1:]]).astype(I32)
    fill_n = jnp.concatenate([padded - counts, cap - ends_p[-1:]]).astype(I32)
    return dest, block_e, n_used.reshape(1), fill_base, fill_n


def _dispatch_kernel(fb_ref, fn_ref, dest_ref, h_ref, xs_hbm, zrow, sem, zsem):
    step = pl.program_id(0)
    tm = h_ref.shape[0]

    @pl.when(step == 0)
    def _():
        zrow[...] = jnp.zeros(zrow.shape, zrow.dtype)
        for e in range(N_EXPERTS + 1):
            base = fb_ref[e]
            n = fn_ref[e]

            def zissue(j, carry):
                pltpu.make_async_copy(zrow.at[pl.ds(0, 1)], xs_hbm.at[pl.ds(base + j, 1)], zsem).start()
                return carry

            def zwait(j, carry):
                pltpu.make_async_copy(zrow.at[pl.ds(0, 1)], xs_hbm.at[pl.ds(0, 1)], zsem).wait()
                return carry

            lax.fori_loop(0, n, zissue, 0)
            lax.fori_loop(0, n, zwait, 0)

    def issue(r, carry):
        for kk in range(TOP_K):
            pltpu.make_async_copy(h_ref.at[pl.ds(r, 1)],
                                  xs_hbm.at[pl.ds(dest_ref[0, 0, r * TOP_K + kk], 1)], sem).start(priority=kk % 2)
        return carry

    lax.fori_loop(0, tm, issue, 0, unroll=4)
    for kk in range(TOP_K):
        pltpu.make_async_copy(h_ref, xs_hbm.at[pl.ds(0, tm)], sem).wait()


def _dispatch(h2, dest, fill_base, fill_n, *, cap, tm):
    t, d = h2.shape
    grid_spec = pltpu.PrefetchScalarGridSpec(
        num_scalar_prefetch=2,
        grid=(t // tm,),
        in_specs=[pl.BlockSpec((1, 1, tm * TOP_K), lambda i, fb, fn: (i, 0, 0), memory_space=pltpu.SMEM),
                  pl.BlockSpec((tm, d), lambda i, fb, fn: (i, 0))],
        out_specs=pl.BlockSpec(memory_space=pl.ANY),
        scratch_shapes=[pltpu.VMEM((8, d), F32), pltpu.SemaphoreType.DMA(()), pltpu.SemaphoreType.DMA(())],
    )
    return pl.pallas_call(
        _dispatch_kernel,
        grid_spec=grid_spec,
        out_shape=jax.ShapeDtypeStruct((cap, d), F32),
        compiler_params=_cparams(("arbitrary",), disable_bounds_checks=True),
        name="moe_dispatch",
    )(fill_base, fill_n, dest.reshape(t // tm, 1, tm * TOP_K), h2)


def _expert_kernel(be_ref, nu_ref, xs_ref, w1_ref, b1_ref, w2_ref, b2_ref, ys_ref, w1b, w2b):
    b = pl.program_id(0)
    de = w2_ref.shape[1]
    used = b < nu_ref[0]
    fresh = jnp.logical_or(b == 0, be_ref[b] != be_ref[jnp.maximum(b - 1, 0)])

    @pl.when(jnp.logical_and(used, fresh))
    def _():
        w1b[...] = w1_ref[0].astype(BF16)
        w2b[...] = w2_ref[0].astype(BF16)

    @pl.when(used)
    def _():
        a = jnp.dot(xs_ref[...].astype(BF16), w1b[...], preferred_element_type=F32) + b1_ref[0]
        gt = jnp.minimum(a[:, :de], SWIGLU_LIMIT)
        up = jnp.clip(a[:, de:], -SWIGLU_LIMIT, SWIGLU_LIMIT)
        act = (up + 1.0) * (gt * jax.nn.sigmoid(SWIGLU_ALPHA * gt))
        ys_ref[...] = jnp.dot(act.astype(BF16), w2b[...], preferred_element_type=F32) + b2_ref[0]

    @pl.when(jnp.logical_not(used))
    def _():
        ys_ref[...] = jnp.zeros(ys_ref.shape, ys_ref.dtype)


def _experts(xs, block_e, n_used, w1, b1, w2, b2, *, blk):
    cap, d = xs.shape
    ne, _, de2 = w1.shape
    de = w2.shape[1]
    nblk = cap // blk

    def xrow(b, be, nu):
        return (jnp.minimum(b, nu[0] - 1), 0)

    grid_spec = pltpu.PrefetchScalarGridSpec(
        num_scalar_prefetch=2,
        grid=(nblk,),
        in_specs=[pl.BlockSpec((blk, d), xrow),
                  pl.BlockSpec((1, d, de2), lambda b, be, nu: (be[b], 0, 0)),
                  pl.BlockSpec((1, 1, de2), lambda b, be, nu: (be[b], 0, 0)),
                  pl.BlockSpec((1, de, d), lambda b, be, nu: (be[b], 0, 0)),
                  pl.BlockSpec((1, 1, d), lambda b, be, nu: (be[b], 0, 0))],
        out_specs=pl.BlockSpec((blk, d), lambda b, be, nu: (b, 0)),
        scratch_shapes=[pltpu.VMEM((d, de2), BF16), pltpu.VMEM((de, d), BF16)],
    )
    return pl.pallas_call(
        _expert_kernel,
        grid_spec=grid_spec,
        out_shape=jax.ShapeDtypeStruct((cap, d), F32),
        compiler_params=_cparams(("arbitrary",), vmem_limit_bytes=EXPERT_VMEM_LIMIT),
        name="moe_experts",
    )(block_e, n_used, xs, w1, b1.reshape(ne, 1, de2), w2, b2.reshape(ne, 1, d))


def _combine_kernel(dest_ref, gate_ref, x1_ref, mod_ref, npost_ref, ys_hbm, o_ref, buf, sem):
    tm = x1_ref.shape[0]

    def issue(r, carry):
        for kk in range(TOP_K):
            pltpu.make_async_copy(ys_hbm.at[pl.ds(dest_ref[0, 0, r * TOP_K + kk], 1)],
                                  buf.at[kk, pl.ds(r, 1)], sem).start()
        return carry

    lax.fori_loop(0, tm, issue, 0, unroll=4)
    for kk in range(TOP_K):
        pltpu.make_async_copy(ys_hbm.at[pl.ds(0, tm)], buf.at[kk], sem).wait()
    gate = gate_ref[...]
    ff = gate[:, 0:1] * buf[0]
    for kk in range(1, TOP_K):
        ff = ff + gate[:, kk:kk + 1] * buf[kk]
    g2 = mod_ref[0, 5:6, :]
    o_ref[...] = x1_ref[...] + g2 * (ff * _rsqrt_mean_sq(ff) * npost_ref[...])


def _combine(ys, dest, gate, x1, mod, npost, *, lc, tm):
    t, d = x1.shape
    nb = lc // tm
    s = t // lc

    def row(si, i):
        return si * nb + i

    return pl.pallas_call(
        _combine_kernel,
        grid=(s, nb),
        in_specs=[pl.BlockSpec((1, 1, tm * TOP_K), lambda si, i: (row(si, i), 0, 0), memory_space=pltpu.SMEM),
                  pl.BlockSpec((tm, LANES), lambda si, i: (row(si, i), 0)),
                  pl.BlockSpec((tm, d), lambda si, i: (row(si, i), 0)),
                  pl.BlockSpec((1, 6, d), lambda si, i: (si, 0, 0)),
                  pl.BlockSpec((1, d), lambda si, i: (0, 0)),
                  pl.BlockSpec(memory_space=pl.ANY)],
        out_specs=pl.BlockSpec((tm, d), lambda si, i: (row(si, i), 0)),
        out_shape=jax.ShapeDtypeStruct((t, d), F32),
        scratch_shapes=[pltpu.VMEM((TOP_K, tm, d), F32), pltpu.SemaphoreType.DMA(())],
        compiler_params=_cparams(("arbitrary", "arbitrary"), disable_bounds_checks=True),
        name="moe_combine",
    )(dest.reshape(t // tm, 1, tm * TOP_K), gate, x1, mod, npost, ys)


def _rope_tables(n_pos):
    rows = n_pos // GRID_W
    row_idx = jnp.repeat(jnp.arange(rows, dtype=F32), GRID_W)
    col_idx = jnp.tile(jnp.arange(GRID_W, dtype=F32), rows)
    inv = 1.0 / (ROPE_THETA ** (jnp.arange(0, ROPE_AXIS_DIM, 2, dtype=F32) / ROPE_AXIS_DIM))
    ar = row_idx[:, None] * inv[None, :]
    ac = col_idx[:, None] * inv[None, :]
    cos = jnp.concatenate([jnp.cos(ar), jnp.cos(ar), jnp.cos(ac), jnp.cos(ac)], axis=-1)
    sin = jnp.concatenate([-jnp.sin(ar), jnp.sin(ar), -jnp.sin(ac), jnp.sin(ac)], axis=-1)
    reps = LANES // HEAD_DIM
    return jnp.tile(cos, (1, reps)), jnp.tile(sin, (1, reps))


def _tiles(lc):
    tm = min(512, lc)
    tq = min(256, lc)
    tk = min(1024, lc)
    return tm, tq, tk


def kernel(x_prompt, x_sample, c_prompt, c_sample, w_ada, b_ada, norm_pre_mix, norm_post_mix, norm_pre_ffn, norm_post_ffn, w_in, q_norm, k_norm, ssm_lam_re, ssm_lam_im, ssm_log_dt, ssm_b_re, ssm_b_im, ssm_c_re, ssm_c_im, ssm_d, ssm_w_glu, ssm_b_glu, conv_w_dw, conv_b_dw, conv_ln_g, conv_ln_b, conv_w_pw, branch_norm, w_out, router_w, router_b, expert_w1, expert_b1, expert_w2, expert_b2):
    bsz, lc, d = x_prompt.shape
    dbsz, dl, _ = x_sample.shape
    assert dbsz == 1 and dl % lc == 0 and lc % S5_CHUNK == 0 and lc % GRID_W == 0
    n_sample = dl // lc
    n_seq = bsz + n_sample
    t_prompt = bsz * lc
    t = t_prompt + dl
    depth = w_ada.shape[0]
    ssm_w = ssm_d.shape[1]
    conv_w = conv_b_dw.shape[1]
    n_groups = ssm_w // SSM_GROUP
    tm, tq, tk = _tiles(lc)
    tc = S5_CHUNK
    nc = t // tc
    cps = lc // tc
    kw = tc * SSM_GROUP
    assert t_prompt % dl == 0
    blk = min(MOE_BLOCK, t * TOP_K // N_EXPERTS)

    x = jnp.concatenate([x_prompt.reshape(t_prompt, d), x_sample.reshape(dl, d)], axis=0)
    c_all = jnp.concatenate([c_prompt, jnp.repeat(c_sample, n_sample, axis=0)], axis=0)
    cos_t, sin_t = _rope_tables(max(lc, dl))
    bd = (jnp.arange(LANES)[:, None] // HEAD_DIM == jnp.arange(LANES)[None, :] // HEAD_DIM).astype(BF16)
    reps = LANES // HEAD_DIM
    vec = lambda a: a.astype(F32).reshape(1, -1)

    for li in range(depth):
        mod = _modulation(c_all, w_ada[li], b_ada[li]).reshape(n_seq, 6, d)
        qkn = jnp.concatenate([jnp.tile(vec(q_norm[li]) * (HEAD_DIM ** -0.5), (ATTN_W // LANES, reps)),
                               jnp.tile(vec(k_norm[li]), (1, reps)),
                               jnp.zeros((8 - ATTN_W // LANES - 1, LANES), F32)], axis=0)
        q, k, v, u, hc = _inproj(x, mod, vec(norm_pre_mix[li]), w_in[li].astype(BF16), qkn, cos_t, sin_t, bd,
                                 n_prompt=bsz, lc=lc, tm=tm, ssm_w=ssm_w, conv_w=conv_w)

        attn = jnp.concatenate([
            _attention(q, k, v, row0=0, n_seq=bsz, lkv=lc, tq=tq, tk=tk),
            _attention(q, k, v, row0=t_prompt, n_seq=1, lkv=dl, tq=tq, tk=tk)], axis=0)

        toep, ws, wc, decay = _s5_weights(ssm_lam_re[li], ssm_lam_im[li], ssm_log_dt[li], ssm_b_re[li],
                                          ssm_b_im[li], ssm_c_re[li], ssm_c_im[li], ssm_d[li], tc)
        ug = u.reshape(nc, tc, n_groups, SSM_GROUP).transpose(2, 0, 1, 3).reshape(n_groups, nc, kw)
        tr = math.gcd(nc, 384)
        y_intra, *contribs = _s5_intra(ug, toep, ws, tr=tr)
        st_p = _s5_scan(decay, contribs, chunk0=0, n_seq=bsz, cps=cps)
        st_s = _s5_scan(decay, contribs, chunk0=t_prompt // tc, n_seq=1, cps=dl // tc)
        states = [jnp.concatenate([a, b], axis=0) for a, b in zip(st_p, st_s)]
        yg = _s5_out(y_intra, states, wc, tr=tr)
        ys = yg.reshape(n_groups, nc, tc, SSM_GROUP).transpose(1, 2, 0, 3).reshape(t, ssm_w)

        bn = vec(branch_norm[li])
        cn = _conv_module(hc, conv_w_dw[li], conv_b_dw[li], conv_ln_g[li], conv_ln_b[li],
                          conv_w_pw[li].astype(BF16), bn[0, ATTN_W + ssm_w:], n_prompt=bsz, lc=lc, tm=tm)

        rw_pad = jnp.zeros((d, LANES), F32).at[:, :N_EXPERTS].set(router_w[li].astype(F32))
        rb_pad = jnp.full((1, LANES), NEG_BIG, F32).at[0, :N_EXPERTS].set(router_b[li].astype(F32))
        x1, h2, topi, topg = _outproj(x, attn, ys, cn, mod, bn, ssm_w_glu[li].astype(BF16), vec(ssm_b_glu[li]),
                                      w_out[li].astype(BF16), vec(norm_post_mix[li]), vec(norm_pre_ffn[li]),
                                      rw_pad, rb_pad, lc=lc, tm=tm)

        dest, block_e, n_used, fill_base, fill_n = _route(topi[:, :TOP_K], blk)
        xs = _dispatch(h2, dest, fill_base, fill_n, cap=block_e.shape[0] * blk, tm=tm)
        ye = _experts(xs, block_e, n_used, expert_w1[li].astype(F32), expert_b1[li].astype(F32),
                      expert_w2[li].astype(F32), expert_b2[li].astype(F32), blk=blk)
        x = _combine(ye, dest, topg, x1, mod, vec(norm_post_ffn[li]), lc=lc, tm=tm)

    return (x[:t_prompt].reshape(bsz, lc, d), x[t_prompt:].reshape(dbsz, dl, d))
```

```python
import functools
import math

import jax
import jax.numpy as jnp
from jax import lax
from jax.experimental import pallas as pl
from jax.experimental.pallas import tpu as pltpu

F32 = jnp.float32
BF16 = jnp.bfloat16
I32 = jnp.int32

HEAD_DIM = 64
N_HEADS = 8
N_KV_HEADS = 2
GQA_GROUP = N_HEADS // N_KV_HEADS
ATTN_W = N_HEADS * HEAD_DIM
KV_W = N_KV_HEADS * HEAD_DIM
SSM_GROUP = 16
SSM_STATE = 64
CONV_KERNEL = 31
CONV_PAD = CONV_KERNEL // 2
GRID_W = 64
ROPE_AXIS_DIM = HEAD_DIM // 2
ROPE_THETA = 10000.0
N_EXPERTS = 32
TOP_K = 4
SWIGLU_LIMIT = 7.0
SWIGLU_ALPHA = 1.702
EPS = 1e-6

LANES = 128
HALO = 16
V_ROWS = HEAD_DIM + 16
S5_CHUNK = 32
MOE_BLOCK = 512
NEG_BIG = -1e30
VMEM_LIMIT = 48 * 1024 * 1024
EXPERT_VMEM_LIMIT = 56 * 1024 * 1024


def _cparams(sem, vmem_limit_bytes=VMEM_LIMIT, **kw):
    return pltpu.CompilerParams(dimension_semantics=sem, vmem_limit_bytes=vmem_limit_bytes, **kw)


def _rsqrt_mean_sq(x):
    return lax.rsqrt(jnp.mean(x * x, axis=-1, keepdims=True) + EPS)


def _mod_kernel(c_ref, w_ref, b_ref, o_ref):
    c = c_ref[...]
    s = c * jax.nn.sigmoid(c)
    o_ref[...] = jnp.dot(s, w_ref[...], preferred_element_type=F32,
                         precision=lax.Precision.HIGHEST) + b_ref[...]


def _modulation(c_all, w_ada, b_ada):
    s, d = c_all.shape
    n = w_ada.shape[1]
    tn = n // 4
    return pl.pallas_call(
        _mod_kernel,
        grid=(n // tn,),
        in_specs=[pl.BlockSpec((s, d), lambda j: (0, 0)),
                  pl.BlockSpec((d, tn), lambda j: (0, j)),
                  pl.BlockSpec((1, tn), lambda j: (0, j))],
        out_specs=pl.BlockSpec((s, tn), lambda j: (0, j)),
        out_shape=jax.ShapeDtypeStruct((s, n), F32),
        compiler_params=_cparams(("arbitrary",)),
        name="adaln_mod",
    )(c_all, w_ada, b_ada.reshape(1, n))


def _inproj_kernel(x_ref, mod_ref, npre_ref, w_ref, qkn_ref, cos_ref, sin_ref, bd_ref,
                   q_ref, k_ref, v_ref, u_ref, hc_ref):
    x = x_ref[...]
    sh = mod_ref[0, 0:1, :]
    sc = mod_ref[0, 1:2, :]
    h = x * _rsqrt_mean_sq(x) * npre_ref[...] * (1.0 + sc) + sh
    proj = jnp.dot(h.astype(BF16), w_ref[...], preferred_element_type=F32)
    tm = x.shape[0]
    cos = cos_ref[...]
    sin = sin_ref[...]
    lane = lax.broadcasted_iota(I32, (tm, LANES), 1)
    first_half = (lane % ROPE_AXIS_DIM) < (ROPE_AXIS_DIM // 2)
    n_qk = (ATTN_W + KV_W) // LANES
    for c in range(n_qk):
        xc = proj[:, c * LANES:(c + 1) * LANES]
        ms = jnp.dot((xc * xc).astype(BF16), bd_ref[...], preferred_element_type=F32) * (1.0 / HEAD_DIM)
        xn = xc * lax.rsqrt(ms + EPS) * qkn_ref[c:c + 1, :]
        half = ROPE_AXIS_DIM // 2
        swapped = jnp.where(first_half, pltpu.roll(xn, LANES - half, 1), pltpu.roll(xn, half, 1))
        rot = xn * cos + swapped * sin
        if c < ATTN_W // LANES:
            rot_t = rot.T.astype(BF16)
            q_ref[2 * c] = rot_t[:HEAD_DIM, :]
            q_ref[2 * c + 1] = rot_t[HEAD_DIM:, :]
        else:
            rot = rot.astype(BF16)
            k_ref[0] = rot[:, :HEAD_DIM]
            k_ref[1] = rot[:, HEAD_DIM:]
    o2 = ATTN_W + KV_W
    v_t = proj[:, o2:o2 + KV_W].T.astype(BF16)
    ones = jnp.ones((V_ROWS - HEAD_DIM, tm), BF16)
    for hh in range(N_KV_HEADS):
        v_ref[hh, 0, 0:HEAD_DIM, :] = v_t[hh * HEAD_DIM:(hh + 1) * HEAD_DIM, :]
        v_ref[hh, 0, HEAD_DIM:V_ROWS, :] = ones
    o3 = o2 + KV_W
    ssm_w = u_ref.shape[1]
    u_ref[...] = proj[:, o3:o3 + ssm_w].astype(BF16)
    o4 = o3 + ssm_w
    conv_w = hc_ref.shape[1]
    a = proj[:, o4:o4 + conv_w]
    gate = proj[:, o4 + conv_w:o4 + 2 * conv_w]
    hc_ref[...] = a * jax.nn.sigmoid(gate)


def _inproj(x, mod, norm_pre, w_in_bf, qkn, cos_t, sin_t, bd, *, n_prompt, lc, tm, ssm_w, conv_w):
    t, d = x.shape
    nb = lc // tm
    s = t // lc
    in_w = w_in_bf.shape[1]

    def row(si, i):
        return si * nb + i

    def pos(si, i):
        return jnp.where(si < n_prompt, i, (si - n_prompt) * nb + i)

    return pl.pallas_call(
        _inproj_kernel,
        grid=(s, nb),
        in_specs=[pl.BlockSpec((tm, d), lambda si, i: (row(si, i), 0)),
                  pl.BlockSpec((1, 6, d), lambda si, i: (si, 0, 0)),
                  pl.BlockSpec((1, d), lambda si, i: (0, 0)),
                  pl.BlockSpec((d, in_w), lambda si, i: (0, 0)),
                  pl.BlockSpec((8, LANES), lambda si, i: (0, 0)),
                  pl.BlockSpec((tm, LANES), lambda si, i: (pos(si, i), 0)),
                  pl.BlockSpec((tm, LANES), lambda si, i: (pos(si, i), 0)),
                  pl.BlockSpec((LANES, LANES), lambda si, i: (0, 0))],
        out_specs=[pl.BlockSpec((N_HEADS, HEAD_DIM, tm), lambda si, i: (0, 0, row(si, i))),
                   pl.BlockSpec((N_KV_HEADS, tm, HEAD_DIM), lambda si, i: (0, row(si, i), 0)),
                   pl.BlockSpec((N_KV_HEADS, 1, V_ROWS, tm), lambda si, i: (0, row(si, i), 0, 0)),
                   pl.BlockSpec((tm, ssm_w), lambda si, i: (row(si, i), 0)),
                   pl.BlockSpec((tm, conv_w), lambda si, i: (row(si, i), 0))],
        out_shape=[jax.ShapeDtypeStruct((N_HEADS, HEAD_DIM, t), BF16),
                   jax.ShapeDtypeStruct((N_KV_HEADS, t, HEAD_DIM), BF16),
                   jax.ShapeDtypeStruct((N_KV_HEADS, t // tm, V_ROWS, tm), BF16),
                   jax.ShapeDtypeStruct((t, ssm_w), BF16),
                   jax.ShapeDtypeStruct((t, conv_w), F32)],
        compiler_params=_cparams(("parallel", "arbitrary")),
        name="inproj",
    )(x, mod, norm_pre, w_in_bf, qkn, cos_t, sin_t, bd)


def _attn_kernel(q_ref, k_ref, v_ref, o_ref, s_ref, m_ref, acc_ref, *, tk):
    g, hd, tq = q_ref.shape
    q_t = jnp.concatenate([q_ref[i] for i in range(g)], axis=-1)
    n = k_ref.shape[1] // tk
    m_ref[...] = jnp.full(m_ref.shape, NEG_BIG, F32)
    acc_ref[...] = jnp.zeros(acc_ref.shape, F32)

    def scores(j):
        kc = k_ref[0, pl.ds(pl.multiple_of(j * tk, tk), tk), :]
        return jnp.dot(kc, q_t, preferred_element_type=F32)

    def step(j, slot, prefetch):
        if prefetch:
            s_ref[1 - slot] = scores(j + 1)
        s = s_ref[slot]
        m_prev = m_ref[...]
        m_next = jnp.maximum(m_prev, jnp.max(s, axis=0, keepdims=True))
        alpha = jnp.exp2(m_prev - m_next)
        p = jnp.exp2(s - m_next)
        acc_ref[...] = acc_ref[...] * alpha + jnp.dot(v_ref[0, j], p.astype(BF16), preferred_element_type=F32)
        m_ref[...] = m_next

    s_ref[0] = scores(0)
    n_pairs = (n - 1) // 2

    def body(jj, carry):
        step(2 * jj, 0, True)
        step(2 * jj + 1, 1, True)
        return carry

    lax.fori_loop(0, n_pairs, body, 0)
    if n - 2 * n_pairs == 2:
        step(2 * n_pairs, 0, True)
        step(2 * n_pairs + 1, 1, False)
    else:
        step(2 * n_pairs, 0, False)
    o_t = acc_ref[0:hd, :] * (1.0 / acc_ref[hd:hd + 1, :])
    o_ref[...] = jnp.concatenate([o_t[:, i * tq:(i + 1) * tq] for i in range(g)], axis=0).T.astype(o_ref.dtype)


def _attention(q_t, k, v_t, *, row0, n_seq, lkv, tq, tk):
    nq = lkv // tq
    qb0 = row0 // tq
    kb0 = row0 // lkv
    cols = GQA_GROUP * tq
    return pl.pallas_call(
        functools.partial(_attn_kernel, tk=tk),
        grid=(n_seq, N_KV_HEADS, nq),
        in_specs=[pl.BlockSpec((GQA_GROUP, HEAD_DIM, tq), lambda b, g, i: (g, 0, qb0 + b * nq + i)),
                  pl.BlockSpec((1, lkv, HEAD_DIM), lambda b, g, i: (g, kb0 + b, 0)),
                  pl.BlockSpec((1, lkv // tk, V_ROWS, tk), lambda b, g, i: (g, kb0 + b, 0, 0))],
        out_specs=pl.BlockSpec((tq, GQA_GROUP * HEAD_DIM), lambda b, g, i: (b * nq + i, g)),
        out_shape=jax.ShapeDtypeStruct((n_seq * lkv, ATTN_W), BF16),
        scratch_shapes=[pltpu.VMEM((2, tk, cols), F32), pltpu.VMEM((1, cols), F32),
                        pltpu.VMEM((V_ROWS, cols), F32)],
        compiler_params=_cparams(("parallel", "parallel", "arbitrary")),
        name="attention",
    )(q_t, k, v_t)


def _s5_weights(lam_re, lam_im, log_dt, b_re, b_im, c_re, c_im, d_skip, tc):
    hp = lax.Precision.HIGHEST
    lam = lax.complex(lam_re.astype(F32), lam_im.astype(F32))
    dt = jnp.exp(log_dt.astype(F32))[..., None]
    lam_dt = lam * dt
    lam_bar = jnp.exp(lam_dt)
    b_bar = ((lam_bar - 1.0) / lam)[..., None] * lax.complex(b_re.astype(F32), b_im.astype(F32))
    cmat = lax.complex(c_re.astype(F32), c_im.astype(F32))
    g, p = lam.shape[1], lam.shape[2]
    h = b_bar.shape[-1]
    taus = jnp.arange(tc + 1, dtype=F32)
    pw = jnp.exp(lam_dt[..., None] * taus)
    kern = jnp.einsum('dgop,dgpt,dgpi->dgtoi', cmat, pw, b_bar, precision=hp).real
    eye_h = jnp.eye(h, dtype=F32)
    kd = kern[0][:, 0] + kern[1][:, 0] + d_skip.astype(F32).reshape(g, h)[:, :, None] * eye_h
    by_lag = jnp.concatenate([jnp.flip(kern[1][:, 1:tc], axis=1), kd[:, None], kern[0][:, 1:tc]], axis=1)
    by_lag = by_lag.transpose(0, 2, 3, 1)
    padded = jnp.concatenate([by_lag, jnp.zeros((g, h, h, 1), F32)], axis=-1)
    skew = jnp.tile(padded, (1, 1, 1, tc))[..., :tc * (2 * tc - 1)].reshape(g, h, h, tc, 2 * tc - 1)
    full = skew[..., tc - 1:]
    toep = full.transpose(0, 3, 2, 4, 1).reshape(g, tc * h, tc * h)

    pf = jnp.flip(pw[0][:, :, :tc], axis=2)
    pb = pw[1][:, :, :tc]
    sf = pf[:, :, :, None] * b_bar[0][:, :, None, :]
    sb = pb[:, :, :, None] * b_bar[1][:, :, None, :]
    parts = [sf.real, sf.imag, sb.real, sb.imag]
    ws = jnp.stack([x.transpose(0, 2, 3, 1).reshape(g, tc * h, p) for x in parts], axis=2)
    odd = (jnp.arange(g) % 2)[:, None, None, None]
    zeros = jnp.zeros_like(ws)
    ws = jnp.where(odd == 0, jnp.concatenate([ws, zeros], -1), jnp.concatenate([zeros, ws], -1))

    ef = cmat[0].transpose(0, 2, 1)[:, :, None, :] * pw[0][:, :, 1:tc + 1][:, :, :, None]
    eb = cmat[1].transpose(0, 2, 1)[:, :, None, :] * jnp.flip(pw[1][:, :, 1:tc + 1], axis=2)[:, :, :, None]
    cparts = [ef.real, -ef.imag, eb.real, -eb.imag]
    wc = jnp.stack([x.reshape(g, p, tc * h) for x in cparts], axis=1)
    zc = jnp.zeros_like(wc)
    odd4 = (jnp.arange(g) % 2)[:, None, None, None]
    wc = jnp.where(odd4 == 0, jnp.concatenate([wc, zc], 2), jnp.concatenate([zc, wc], 2))
    wc = wc.reshape(g, 4 * 2 * p, tc * h)

    af = pw[0][:, :, tc].reshape(1, g * p)
    ab = pw[1][:, :, tc].reshape(1, g * p)
    decay = jnp.concatenate([af.real, af.imag, ab.real, ab.imag], axis=0)
    return toep.astype(BF16), ws.reshape(g, tc * h, 4 * 2 * p).astype(BF16), wc.astype(BF16), decay


def _s5_intra_kernel(u_ref, toep_ref, ws_ref, y_ref, s0_ref, s1_ref, s2_ref, s3_ref):
    u0 = u_ref[0]
    u1 = u_ref[1]
    y_ref[0] = jnp.dot(u0, toep_ref[0], preferred_element_type=F32)
    y_ref[1] = jnp.dot(u1, toep_ref[1], preferred_element_type=F32)
    st = (jnp.dot(u0, ws_ref[0], preferred_element_type=F32)
          + jnp.dot(u1, ws_ref[1], preferred_element_type=F32))
    for k, ref in enumerate((s0_ref, s1_ref, s2_ref, s3_ref)):
        ref[...] = st[:, k * LANES:(k + 1) * LANES]


def _s5_intra(ug, toep, ws, *, tr):
    g, nc, kw = ug.shape
    npair = g // 2
    st_shape = jax.ShapeDtypeStruct((nc, npair * LANES), F32)
    st_spec = pl.BlockSpec((tr, LANES), lambda pr, r: (r, pr))
    return pl.pallas_call(
        _s5_intra_kernel,
        grid=(npair, nc // tr),
        in_specs=[pl.BlockSpec((2, tr, kw), lambda pr, r: (pr, r, 0)),
                  pl.BlockSpec((2, kw, kw), lambda pr, r: (pr, 0, 0)),
                  pl.BlockSpec((2, kw, 4 * LANES), lambda pr, r: (pr, 0, 0))],
        out_specs=[pl.BlockSpec((2, tr, kw), lambda pr, r: (pr, r, 0)), st_spec, st_spec, st_spec, st_spec],
        out_shape=[jax.ShapeDtypeStruct((g, nc, kw), F32), st_shape, st_shape, st_shape, st_shape],
        compiler_params=_cparams(("parallel", "arbitrary")),
        name="s5_intra",
    )(ug, toep, ws)


def _s5_scan_kernel(decay_ref, cfr_ref, cfi_ref, cbr_ref, cbi_ref, sfr_ref, sfi_ref, sbr_ref, sbi_ref):
    n = cfr_ref.shape[0]
    w = cfr_ref.shape[1]
    afr = decay_ref[0:1, :]
    afi = decay_ref[1:2, :]
    abr = decay_ref[2:3, :]
    abi = decay_ref[3:4, :]
    zero = jnp.zeros((1, w), F32)

    def fwd(c, st):
        re, im = st
        sfr_ref[pl.ds(c, 1), :] = re
        sfi_ref[pl.ds(c, 1), :] = im
        cr = cfr_ref[pl.ds(c, 1), :]
        ci = cfi_ref[pl.ds(c, 1), :]
        return afr * re - afi * im + cr, afr * im + afi * re + ci

    lax.fori_loop(0, n, fwd, (zero, zero))

    def bwd(kk, st):
        c = n - 1 - kk
        re, im = st
        sbr_ref[pl.ds(c, 1), :] = re
        sbi_ref[pl.ds(c, 1), :] = im
        cr = cbr_ref[pl.ds(c, 1), :]
        ci = cbi_ref[pl.ds(c, 1), :]
        return abr * re - abi * im + cr, abr * im + abi * re + ci

    lax.fori_loop(0, n, bwd, (zero, zero))


def _s5_scan(decay, contribs, *, chunk0, n_seq, cps):
    w = contribs[0].shape[1]
    b0 = chunk0 // cps
    spec_in = pl.BlockSpec((cps, w), lambda b: (b0 + b, 0))
    spec_out = pl.BlockSpec((cps, w), lambda b: (b, 0))
    shape = jax.ShapeDtypeStruct((n_seq * cps, w), F32)
    return pl.pallas_call(
        _s5_scan_kernel,
        grid=(n_seq,),
        in_specs=[pl.BlockSpec((4, w), lambda b: (0, 0))] + [spec_in] * 4,
        out_specs=[spec_out] * 4,
        out_shape=[shape] * 4,
        compiler_params=_cparams(("parallel",)),
        name="s5_scan",
    )(decay, *contribs)


def _s5_out_kernel(y_ref, s0_ref, s1_ref, s2_ref, s3_ref, wc_ref, o_ref):
    st = jnp.concatenate([s0_ref[...], s1_ref[...], s2_ref[...], s3_ref[...]], axis=-1).astype(BF16)
    for k in range(2):
        y = y_ref[k] + jnp.dot(st, wc_ref[k], preferred_element_type=F32)
        o_ref[k] = 0.5 * y * (1.0 + lax.erf(y * (2.0 ** -0.5)))


def _s5_out(y_intra, states, wc, *, tr):
    g, nc, kw = y_intra.shape
    npair = g // 2
    st_spec = pl.BlockSpec((tr, LANES), lambda pr, r: (r, pr))
    return pl.pallas_call(
        _s5_out_kernel,
        grid=(npair, nc // tr),
        in_specs=[pl.BlockSpec((2, tr, kw), lambda pr, r: (pr, r, 0)), st_spec, st_spec, st_spec, st_spec,
                  pl.BlockSpec((2, 4 * LANES, kw), lambda pr, r: (pr, 0, 0))],
        out_specs=pl.BlockSpec((2, tr, kw), lambda pr, r: (pr, r, 0)),
        out_shape=jax.ShapeDtypeStruct((g, nc, kw), F32),
        compiler_params=_cparams(("parallel", "arbitrary")),
        name="s5_out",
    )(y_intra, *states, wc)


def _conv_kernel(prev_ref, cur_ref, next_ref, wdw_ref, bdw_ref, lng_ref, lnb_ref, wpw_ref, bn_ref, o_ref, scr,
                 *, n_prompt, n_seq, nb, rt):
    si = pl.program_id(0)
    i = pl.program_id(1)
    tm = cur_ref.shape[0]
    prev_ok = jnp.logical_or(i > 0, si > n_prompt)
    next_ok = jnp.logical_or(i < nb - 1, jnp.logical_and(si >= n_prompt, si < n_seq - 1))
    scr[0:HALO, :] = jnp.where(prev_ok, prev_ref[...], 0.0)
    scr[HALO:HALO + tm, :] = cur_ref[...]
    scr[HALO + tm:2 * HALO + tm, :] = jnp.where(next_ok, next_ref[...], 0.0)
    off = HALO - CONV_PAD
    for r0 in range(0, tm, rt):
        acc = wdw_ref[0:1, :] * scr[r0 + off:r0 + off + rt, :]
        for kk in range(1, CONV_KERNEL):
            acc = acc + wdw_ref[kk:kk + 1, :] * scr[r0 + off + kk:r0 + off + kk + rt, :]
        acc = acc + bdw_ref[...]
        mu = jnp.mean(acc, axis=-1, keepdims=True)
        cen = acc - mu
        var = jnp.mean(cen * cen, axis=-1, keepdims=True)
        y = cen * lax.rsqrt(var + EPS) * lng_ref[...] + lnb_ref[...]
        y = y * jax.nn.sigmoid(y)
        z = jnp.dot(y.astype(BF16), wpw_ref[...], preferred_element_type=F32)
        o_ref[r0:r0 + rt, :] = (z * _rsqrt_mean_sq(z) * bn_ref[...]).astype(o_ref.dtype)


def _conv_module(hc, w_dw, b_dw, ln_g, ln_b, w_pw_bf, bn_conv, *, n_prompt, lc, tm):
    t, cw = hc.shape
    nb = lc // tm
    s = t // lc
    hb = tm // HALO
    n_hblk = t // HALO
    kpad = 32

    def row(si, i):
        return si * nb + i

    wdw = jnp.zeros((kpad, cw), F32).at[:CONV_KERNEL].set(w_dw.astype(F32))
    vec = lambda a: a.astype(F32).reshape(1, cw)
    const = lambda shape: pl.BlockSpec(shape, lambda si, i: (0, 0))
    return pl.pallas_call(
        functools.partial(_conv_kernel, n_prompt=n_prompt, n_seq=s, nb=nb, rt=min(128, tm)),
        grid=(s, nb),
        in_specs=[pl.BlockSpec((HALO, cw), lambda si, i: (jnp.maximum(row(si, i) * hb - 1, 0), 0)),
                  pl.BlockSpec((tm, cw), lambda si, i: (row(si, i), 0)),
                  pl.BlockSpec((HALO, cw), lambda si, i: (jnp.minimum((row(si, i) + 1) * hb, n_hblk - 1), 0)),
                  const((kpad, cw)), const((1, cw)), const((1, cw)), const((1, cw)), const((cw, cw)),
                  const((1, cw))],
        out_specs=pl.BlockSpec((tm, cw), lambda si, i: (row(si, i), 0)),
        out_shape=jax.ShapeDtypeStruct((t, cw), BF16),
        scratch_shapes=[pltpu.VMEM((tm + 2 * HALO, cw), F32)],
        compiler_params=_cparams(("parallel", "arbitrary")),
        name="conv_module",
    )(hc, hc, hc, wdw, vec(b_dw), vec(ln_g), vec(ln_b), w_pw_bf, vec(bn_conv))


def _outproj_kernel(x_ref, attn_ref, ys_ref, cn_ref, mod_ref, bn_ref, wglu_ref, bglu_ref, wout_ref, npost_ref,
                    npre2_ref, rw_ref, rb_ref, x1_ref, h2_ref, ti_ref, tg_ref):
    aw = attn_ref.shape[1]
    sw = ys_ref.shape[1]
    a = attn_ref[...].astype(F32)
    an = a * _rsqrt_mean_sq(a) * bn_ref[:, 0:aw]
    y = ys_ref[...]
    z = jnp.dot(y.astype(BF16), wglu_ref[...], preferred_element_type=F32) + bglu_ref[...]
    sm = y * jax.nn.sigmoid(z)
    sn = sm * _rsqrt_mean_sq(sm) * bn_ref[:, aw:aw + sw]
    mix = (jnp.dot(an.astype(BF16), wout_ref[0:aw, :], preferred_element_type=F32)
           + jnp.dot(sn.astype(BF16), wout_ref[aw:aw + sw, :], preferred_element_type=F32)
           + jnp.dot(cn_ref[...], wout_ref[aw + sw:, :], preferred_element_type=F32))
    g1 = mod_ref[0, 2:3, :]
    sh2 = mod_ref[0, 3:4, :]
    sc2 = mod_ref[0, 4:5, :]
    x1 = x_ref[...] + g1 * (mix * _rsqrt_mean_sq(mix) * npost_ref[...])
    x1_ref[...] = x1
    h2 = x1 * _rsqrt_mean_sq(x1) * npre2_ref[...] * (1.0 + sc2) + sh2
    h2_ref[...] = h2
    h_hi = h2.astype(BF16)
    h_lo = (h2 - h_hi.astype(F32)).astype(BF16)
    both = jnp.dot(h_hi, rw_ref[...], preferred_element_type=F32)
    logits = (both[:, :LANES] + both[:, LANES:]
              + jnp.dot(h_lo, rw_ref[:, :LANES], preferred_element_type=F32) + rb_ref[...])
    tm = logits.shape[0]
    lane = lax.broadcasted_iota(I32, (tm, LANES), 1)
    vals = []
    ti = jnp.zeros((tm, LANES), I32)
    for kk in range(TOP_K):
        m = jnp.max(logits, axis=1, keepdims=True)
        ik = jnp.min(jnp.where(logits == m, lane, LANES), axis=1, keepdims=True)
        vals.append(m)
        ti = jnp.where(lane == kk, ik, ti)
        logits = jnp.where(lane == ik, NEG_BIG * 2.0, logits)
    es = [jnp.exp(v - vals[0]) for v in vals]
    inv = 1.0 / (es[0] + es[1] + es[2] + es[3])
    tg = jnp.zeros((tm, LANES), F32)
    for kk in range(TOP_K):
        tg = jnp.where(lane == kk, es[kk] * inv, tg)
    ti_ref[...] = ti
    tg_ref[...] = tg


def _outproj(x, attn, ys, cn, mod, bn, wglu_bf, bglu, wout_bf, npost, npre2, rw_pad, rb_pad, *, lc, tm):
    t, d = x.shape
    nb = lc // tm
    s = t // lc
    aw, sw, cw = attn.shape[1], ys.shape[1], cn.shape[1]

    def row(si, i):
        return si * nb + i

    rows = lambda w: pl.BlockSpec((tm, w), lambda si, i: (row(si, i), 0))
    const = lambda shape: pl.BlockSpec(shape, lambda si, i: (0, 0))
    return pl.pallas_call(
        _outproj_kernel,
        grid=(s, nb),
        in_specs=[rows(d), rows(aw), rows(sw), rows(cw),
                  pl.BlockSpec((1, 6, d), lambda si, i: (si, 0, 0)),
                  const((1, d)), const((sw, sw)), const((1, sw)), const((d, d)), const((1, d)), const((1, d)),
                  const((d, 2 * LANES)), const((1, LANES))],
        out_specs=[rows(d), rows(d), rows(LANES), rows(LANES)],
        out_shape=[jax.ShapeDtypeStruct((t, d), F32), jax.ShapeDtypeStruct((t, d), F32),
                   jax.ShapeDtypeStruct((t, LANES), I32), jax.ShapeDtypeStruct((t, LANES), F32)],
        compiler_params=_cparams(("parallel", "arbitrary")),
        name="outproj_router",
    )(x, attn, ys, cn, mod, bn, wglu_bf, bglu, wout_bf, npost, npre2, rw_pad, rb_pad)


def _route(topi, blk):
    t = topi.shape[0]
    n_assign = t * TOP_K
    flat_e = topi.reshape(-1)
    eids = jnp.arange(N_EXPERTS, dtype=I32)
    onehot = (flat_e[:, None] == eids[None, :]).astype(I32)
    csum = jnp.cumsum(onehot, axis=0)
    counts = csum[-1]
    rank = jnp.sum((csum - onehot) * onehot, axis=1)
    padded = ((counts + blk - 1) // blk) * blk
    ends_p = jnp.cumsum(padded)
    starts_p = ends_p - padded
    dest = (jnp.sum(onehot * starts_p[None, :], axis=1) + rank).astype(I32)
    n_blocks = n_assign // blk + N_EXPERTS
    cap = n_blocks * blk
    n_used = (ends_p[-1] // blk).astype(I32)
    bidx = jnp.arange(n_blocks, dtype=I32)
    be_raw = jnp.sum((ends_p[None, :] <= (bidx * blk)[:, None]).astype(I32), axis=1)
    last_e = jnp.max(jnp.where(counts > 0, eids, 0))
    block_e = jnp.where(bidx < n_used, jnp.minimum(be_raw, N_EXPERTS - 1), last_e).astype(I32)
    fill_base = jnp.concatenate([starts_p + counts, ends_p[-1:]]).astype(I32)
    fill_n = jnp.concatenate([padded - counts, cap - ends_p[-1:]]).astype(I32)
    return dest, block_e, n_used.reshape(1), fill_base, fill_n


def _dispatch_kernel(fb_ref, fn_ref, dest_ref, h_ref, xs_hbm, zrow, sem, zsem):
    step = pl.program_id(0)
    tm = h_ref.shape[0]

    @pl.when(step == 0)
    def _():
        zrow[...] = jnp.zeros(zrow.shape, zrow.dtype)
        for e in range(N_EXPERTS + 1):
            base = fb_ref[e]
            n = fn_ref[e]

            def zissue(j, carry):
                pltpu.make_async_copy(zrow.at[pl.ds(0, 1)], xs_hbm.at[pl.ds(base + j, 1)], zsem).start()
                return carry

            def zwait(j, carry):
                pltpu.make_async_copy(zrow.at[pl.ds(0, 1)], xs_hbm.at[pl.ds(0, 1)], zsem).wait()
                return carry

            lax.fori_loop(0, n, zissue, 0)
            lax.fori_loop(0, n, zwait, 0)

    def issue(r, carry):
        for kk in range(TOP_K):
            pltpu.make_async_copy(h_ref.at[pl.ds(r, 1)],
                                  xs_hbm.at[pl.ds(dest_ref[0, 0, r * TOP_K + kk], 1)], sem).start(priority=kk % 2)
        return carry

    lax.fori_loop(0, tm, issue, 0, unroll=4)
    for kk in range(TOP_K):
        pltpu.make_async_copy(h_ref, xs_hbm.at[pl.ds(0, tm)], sem).wait()


def _dispatch(h2, dest, fill_base, fill_n, *, cap, tm):
    t, d = h2.shape
    grid_spec = pltpu.PrefetchScalarGridSpec(
        num_scalar_prefetch=2,
        grid=(t // tm,),
        in_specs=[pl.BlockSpec((1, 1, tm * TOP_K), lambda i, fb, fn: (i, 0, 0), memory_space=pltpu.SMEM),
                  pl.BlockSpec((tm, d), lambda i, fb, fn: (i, 0))],
        out_specs=pl.BlockSpec(memory_space=pl.ANY),
        scratch_shapes=[pltpu.VMEM((8, d), F32), pltpu.SemaphoreType.DMA(()), pltpu.SemaphoreType.DMA(())],
    )
    return pl.pallas_call(
        _dispatch_kernel,
        grid_spec=grid_spec,
        out_shape=jax.ShapeDtypeStruct((cap, d), F32),
        compiler_params=_cparams(("arbitrary",), disable_bounds_checks=True),
        name="moe_dispatch",
    )(fill_base, fill_n, dest.reshape(t // tm, 1, tm * TOP_K), h2)


def _expert_kernel(be_ref, nu_ref, xs_ref, w1_ref, b1_ref, w2_ref, b2_ref, ys_ref, w1b, w2b):
    b = pl.program_id(0)
    de = w2_ref.shape[1]
    used = b < nu_ref[0]
    fresh = jnp.logical_or(b == 0, be_ref[b] != be_ref[jnp.maximum(b - 1, 0)])

    @pl.when(jnp.logical_and(used, fresh))
    def _():
        w1b[...] = w1_ref[0].astype(BF16)
        w2b[...] = w2_ref[0].astype(BF16)

    @pl.when(used)
    def _():
        a = jnp.dot(xs_ref[...].astype(BF16), w1b[...], preferred_element_type=F32) + b1_ref[0]
        gt = jnp.minimum(a[:, :de], SWIGLU_LIMIT)
        up = jnp.clip(a[:, de:], -SWIGLU_LIMIT, SWIGLU_LIMIT)
        act = (up + 1.0) * (gt * jax.nn.sigmoid(SWIGLU_ALPHA * gt))
        ys_ref[...] = jnp.dot(act.astype(BF16), w2b[...], preferred_element_type=F32) + b2_ref[0]

    @pl.when(jnp.logical_not(used))
    def _():
        ys_ref[...] = jnp.zeros(ys_ref.shape, ys_ref.dtype)


def _experts(xs, block_e, n_used, w1, b1, w2, b2, *, blk):
    cap, d = xs.shape
    ne, _, de2 = w1.shape
    de = w2.shape[1]
    nblk = cap // blk

    def xrow(b, be, nu):
        return (jnp.minimum(b, nu[0] - 1), 0)

    grid_spec = pltpu.PrefetchScalarGridSpec(
        num_scalar_prefetch=2,
        grid=(nblk,),
        in_specs=[pl.BlockSpec((blk, d), xrow),
                  pl.BlockSpec((1, d, de2), lambda b, be, nu: (be[b], 0, 0)),
                  pl.BlockSpec((1, 1, de2), lambda b, be, nu: (be[b], 0, 0)),
                  pl.BlockSpec((1, de, d), lambda b, be, nu: (be[b], 0, 0)),
                  pl.BlockSpec((1, 1, d), lambda b, be, nu: (be[b], 0, 0))],
        out_specs=pl.BlockSpec((blk, d), lambda b, be, nu: (b, 0)),
        scratch_shapes=[pltpu.VMEM((d, de2), BF16), pltpu.VMEM((de, d), BF16)],
    )
    return pl.pallas_call(
        _expert_kernel,
        grid_spec=grid_spec,
        out_shape=jax.ShapeDtypeStruct((cap, d), F32),
        compiler_params=_cparams(("arbitrary",), vmem_limit_bytes=EXPERT_VMEM_LIMIT),
        name="moe_experts",
    )(block_e, n_used, xs, w1, b1.reshape(ne, 1, de2), w2, b2.reshape(ne, 1, d))


def _combine_kernel(dest_ref, gate_ref, x1_ref, mod_ref, npost_ref, ys_hbm, o_ref, buf, sem):
    tm = x1_ref.shape[0]

    def issue(r, carry):
        for kk in range(TOP_K):
            pltpu.make_async_copy(ys_hbm.at[pl.ds(dest_ref[0, 0, r * TOP_K + kk], 1)],
                                  buf.at[kk, pl.ds(r, 1)], sem).start()
        return carry

    lax.fori_loop(0, tm, issue, 0, unroll=4)
    for kk in range(TOP_K):
        pltpu.make_async_copy(ys_hbm.at[pl.ds(0, tm)], buf.at[kk], sem).wait()
    gate = gate_ref[...]
    ff = gate[:, 0:1] * buf[0]
    for kk in range(1, TOP_K):
        ff = ff + gate[:, kk:kk + 1] * buf[kk]
    g2 = mod_ref[0, 5:6, :]
    o_ref[...] = x1_ref[...] + g2 * (ff * _rsqrt_mean_sq(ff) * npost_ref[...])


def _combine(ys, dest, gate, x1, mod, npost, *, lc, tm):
    t, d = x1.shape
    nb = lc // tm
    s = t // lc

    def row(si, i):
        return si * nb + i

    return pl.pallas_call(
        _combine_kernel,
        grid=(s, nb),
        in_specs=[pl.BlockSpec((1, 1, tm * TOP_K), lambda si, i: (row(si, i), 0, 0), memory_space=pltpu.SMEM),
                  pl.BlockSpec((tm, LANES), lambda si, i: (row(si, i), 0)),
                  pl.BlockSpec((tm, d), lambda si, i: (row(si, i), 0)),
                  pl.BlockSpec((1, 6, d), lambda si, i: (si, 0, 0)),
                  pl.BlockSpec((1, d), lambda si, i: (0, 0)),
                  pl.BlockSpec(memory_space=pl.ANY)],
        out_specs=pl.BlockSpec((tm, d), lambda si, i: (row(si, i), 0)),
        out_shape=jax.ShapeDtypeStruct((t, d), F32),
        scratch_shapes=[pltpu.VMEM((TOP_K, tm, d), F32), pltpu.SemaphoreType.DMA(())],
        compiler_params=_cparams(("arbitrary", "arbitrary"), disable_bounds_checks=True),
        name="moe_combine",
    )(dest.reshape(t // tm, 1, tm * TOP_K), gate, x1, mod, npost, ys)


def _rope_tables(n_pos):
    rows = n_pos // GRID_W
    row_idx = jnp.repeat(jnp.arange(rows, dtype=F32), GRID_W)
    col_idx = jnp.tile(jnp.arange(GRID_W, dtype=F32), rows)
    inv = 1.0 / (ROPE_THETA ** (jnp.arange(0, ROPE_AXIS_DIM, 2, dtype=F32) / ROPE_AXIS_DIM))
    ar = row_idx[:, None] * inv[None, :]
    ac = col_idx[:, None] * inv[None, :]
    cos = jnp.concatenate([jnp.cos(ar), jnp.cos(ar), jnp.cos(ac), jnp.cos(ac)], axis=-1)
    sin = jnp.concatenate([-jnp.sin(ar), jnp.sin(ar), -jnp.sin(ac), jnp.sin(ac)], axis=-1)
    reps = LANES // HEAD_DIM
    return jnp.tile(cos, (1, reps)), jnp.tile(sin, (1, reps))


def _tiles(lc):
    tm = min(512, lc)
    tq = min(256, lc)
    return tm, tq


def kernel(x_prompt, x_sample, c_prompt, c_sample, w_ada, b_ada, norm_pre_mix, norm_post_mix, norm_pre_ffn, norm_post_ffn, w_in, q_norm, k_norm, ssm_lam_re, ssm_lam_im, ssm_log_dt, ssm_b_re, ssm_b_im, ssm_c_re, ssm_c_im, ssm_d, ssm_w_glu, ssm_b_glu, conv_w_dw, conv_b_dw, conv_ln_g, conv_ln_b, conv_w_pw, branch_norm, w_out, router_w, router_b, expert_w1, expert_b1, expert_w2, expert_b2):
    bsz, lc, d = x_prompt.shape
    dbsz, dl, _ = x_sample.shape
    assert dbsz == 1 and dl % lc == 0 and lc % S5_CHUNK == 0 and lc % GRID_W == 0
    n_sample = dl // lc
    n_seq = bsz + n_sample
    t_prompt = bsz * lc
    t = t_prompt + dl
    depth = w_ada.shape[0]
    ssm_w = ssm_d.shape[1]
    conv_w = conv_b_dw.shape[1]
    n_groups = ssm_w // SSM_GROUP
    tm, tq = _tiles(lc)
    tc = S5_CHUNK
    nc = t // tc
    cps = lc // tc
    kw = tc * SSM_GROUP
    assert t_prompt % dl == 0
    blk = min(MOE_BLOCK, t * TOP_K // N_EXPERTS)

    x = jnp.concatenate([x_prompt.reshape(t_prompt, d), x_sample.reshape(dl, d)], axis=0)
    c_all = jnp.concatenate([c_prompt, jnp.repeat(c_sample, n_sample, axis=0)], axis=0)
    cos_t, sin_t = _rope_tables(max(lc, dl))
    bd = (jnp.arange(LANES)[:, None] // HEAD_DIM == jnp.arange(LANES)[None, :] // HEAD_DIM).astype(BF16)
    reps = LANES // HEAD_DIM
    vec = lambda a: a.astype(F32).reshape(1, -1)

    for li in range(depth):
        mod = _modulation(c_all, w_ada[li], b_ada[li]).reshape(n_seq, 6, d)
        qkn = jnp.concatenate([jnp.tile(vec(q_norm[li]) * (HEAD_DIM ** -0.5 * math.log2(math.e)),
                                        (ATTN_W // LANES, reps)),
                               jnp.tile(vec(k_norm[li]), (1, reps)),
                               jnp.zeros((8 - ATTN_W // LANES - 1, LANES), F32)], axis=0)
        q_t, k, v_t, u, hc = _inproj(x, mod, vec(norm_pre_mix[li]), w_in[li].astype(BF16), qkn, cos_t, sin_t, bd,
                                     n_prompt=bsz, lc=lc, tm=tm, ssm_w=ssm_w, conv_w=conv_w)

        attn = jnp.concatenate([
            _attention(q_t, k, v_t, row0=0, n_seq=bsz, lkv=lc, tq=tq, tk=tm),
            _attention(q_t, k, v_t, row0=t_prompt, n_seq=1, lkv=dl, tq=tq, tk=tm)], axis=0)

        toep, ws, wc, decay = _s5_weights(ssm_lam_re[li], ssm_lam_im[li], ssm_log_dt[li], ssm_b_re[li],
                                          ssm_b_im[li], ssm_c_re[li], ssm_c_im[li], ssm_d[li], tc)
        ug = u.reshape(nc, tc, n_groups, SSM_GROUP).transpose(2, 0, 1, 3).reshape(n_groups, nc, kw)
        tr = math.gcd(nc, 384)
        y_intra, *contribs = _s5_intra(ug, toep, ws, tr=tr)
        st_p = _s5_scan(decay, contribs, chunk0=0, n_seq=bsz, cps=cps)
        st_s = _s5_scan(decay, contribs, chunk0=t_prompt // tc, n_seq=1, cps=dl // tc)
        states = [jnp.concatenate([a, b], axis=0) for a, b in zip(st_p, st_s)]
        yg = _s5_out(y_intra, states, wc, tr=tr)
        ys = yg.reshape(n_groups, nc, tc, SSM_GROUP).transpose(1, 2, 0, 3).reshape(t, ssm_w)

        bn = vec(branch_norm[li])
        cn = _conv_module(hc, conv_w_dw[li], conv_b_dw[li], conv_ln_g[li], conv_ln_b[li],
                          conv_w_pw[li].astype(BF16), bn[0, ATTN_W + ssm_w:], n_prompt=bsz, lc=lc, tm=tm)

        rw_pad = jnp.zeros((d, LANES), F32).at[:, :N_EXPERTS].set(router_w[li].astype(F32))
        rb_pad = jnp.full((1, LANES), NEG_BIG, F32).at[0, :N_EXPERTS].set(router_b[li].astype(F32))
        rw_hi = rw_pad.astype(BF16)
        rw_pad = jnp.concatenate([rw_hi, (rw_pad - rw_hi.astype(F32)).astype(BF16)], axis=1)
        x1, h2, topi, topg = _outproj(x, attn, ys, cn, mod, bn, ssm_w_glu[li].astype(BF16), vec(ssm_b_glu[li]),
                                      w_out[li].astype(BF16), vec(norm_post_mix[li]), vec(norm_pre_ffn[li]),
                                      rw_pad, rb_pad, lc=lc, tm=tm)

        dest, block_e, n_used, fill_base, fill_n = _route(topi[:, :TOP_K], blk)
        xs = _dispatch(h2, dest, fill_base, fill_n, cap=block_e.shape[0] * blk, tm=tm)
        ye = _experts(xs, block_e, n_used, expert_w1[li].astype(F32), expert_b1[li].astype(F32),
                      expert_w2[li].astype(F32), expert_b2[li].astype(F32), blk=blk)
        x = _combine(ye, dest, topg, x1, mod, vec(norm_post_ffn[li]), lc=lc, tm=tm)

    return (x[:t_prompt].reshape(bsz, lc, d), x[t_prompt:].reshape(dbsz, dl, d))
```

```python
import functools
import math

import jax
import jax.numpy as jnp
from jax import lax
from jax.experimental import pallas as pl
from jax.experimental.pallas import tpu as pltpu

F32 = jnp.float32
BF16 = jnp.bfloat16
I32 = jnp.int32

HEAD_DIM = 64
N_HEADS = 8
N_KV_HEADS = 2
GQA_GROUP = N_HEADS // N_KV_HEADS
ATTN_W = N_HEADS * HEAD_DIM
KV_W = N_KV_HEADS * HEAD_DIM
SSM_GROUP = 16
SSM_STATE = 64
CONV_KERNEL = 31
CONV_PAD = CONV_KERNEL // 2
GRID_W = 64
ROPE_AXIS_DIM = HEAD_DIM // 2
ROPE_THETA = 10000.0
N_EXPERTS = 32
TOP_K = 4
SWIGLU_LIMIT = 7.0
SWIGLU_ALPHA = 1.702
EPS = 1e-6

LANES = 128
HALO = 16
V_ROWS = HEAD_DIM + 16
S5_CHUNK = 32
MOE_BLOCK = 512
DMA_BATCH = 8
NEG_BIG = -1e30
VMEM_LIMIT = 48 * 1024 * 1024
EXPERT_VMEM_LIMIT = 56 * 1024 * 1024


def _cparams(sem, vmem_limit_bytes=VMEM_LIMIT, **kw):
    return pltpu.CompilerParams(dimension_semantics=sem, vmem_limit_bytes=vmem_limit_bytes, **kw)


def _rsqrt_mean_sq(x):
    return lax.rsqrt(jnp.mean(x * x, axis=-1, keepdims=True) + EPS)


def _mod_kernel(c_ref, w_ref, b_ref, o_ref):
    c = c_ref[...]
    s = c * jax.nn.sigmoid(c)
    o_ref[...] = jnp.dot(s, w_ref[...], preferred_element_type=F32,
                         precision=lax.Precision.HIGHEST) + b_ref[...]


def _modulation(c_all, w_ada, b_ada):
    s, d = c_all.shape
    n = w_ada.shape[1]
    tn = n // 4
    return pl.pallas_call(
        _mod_kernel,
        grid=(n // tn,),
        in_specs=[pl.BlockSpec((s, d), lambda j: (0, 0)),
                  pl.BlockSpec((d, tn), lambda j: (0, j)),
                  pl.BlockSpec((1, tn), lambda j: (0, j))],
        out_specs=pl.BlockSpec((s, tn), lambda j: (0, j)),
        out_shape=jax.ShapeDtypeStruct((s, n), F32),
        compiler_params=_cparams(("arbitrary",)),
        name="adaln_mod",
    )(c_all, w_ada, b_ada.reshape(1, n))


def _inproj_kernel(x_ref, mod_ref, npre_ref, w_ref, qkn_ref, cos_ref, sin_ref, bd_ref,
                   q_ref, k_ref, v_ref, u_ref, hc_ref):
    x = x_ref[...]
    sh = mod_ref[0, 0:1, :]
    sc = mod_ref[0, 1:2, :]
    h = x * _rsqrt_mean_sq(x) * npre_ref[...] * (1.0 + sc) + sh
    proj = jnp.dot(h.astype(BF16), w_ref[...], preferred_element_type=F32)
    tm = x.shape[0]
    cos = cos_ref[...]
    sin = sin_ref[...]
    lane = lax.broadcasted_iota(I32, (tm, LANES), 1)
    first_half = (lane % ROPE_AXIS_DIM) < (ROPE_AXIS_DIM // 2)
    n_qk = (ATTN_W + KV_W) // LANES
    for c in range(n_qk):
        xc = proj[:, c * LANES:(c + 1) * LANES]
        ms = jnp.dot((xc * xc).astype(BF16), bd_ref[...], preferred_element_type=F32) * (1.0 / HEAD_DIM)
        xn = xc * lax.rsqrt(ms + EPS) * qkn_ref[c:c + 1, :]
        half = ROPE_AXIS_DIM // 2
        swapped = jnp.where(first_half, pltpu.roll(xn, LANES - half, 1), pltpu.roll(xn, half, 1))
        rot = xn * cos + swapped * sin
        if c < ATTN_W // LANES:
            rot_t = rot.T.astype(BF16)
            q_ref[2 * c] = rot_t[:HEAD_DIM, :]
            q_ref[2 * c + 1] = rot_t[HEAD_DIM:, :]
        else:
            rot = rot.astype(BF16)
            k_ref[0] = rot[:, :HEAD_DIM]
            k_ref[1] = rot[:, HEAD_DIM:]
    o2 = ATTN_W + KV_W
    v_t = proj[:, o2:o2 + KV_W].T.astype(BF16)
    ones = jnp.ones((V_ROWS - HEAD_DIM, tm), BF16)
    for hh in range(N_KV_HEADS):
        v_ref[hh, 0, 0:HEAD_DIM, :] = v_t[hh * HEAD_DIM:(hh + 1) * HEAD_DIM, :]
        v_ref[hh, 0, HEAD_DIM:V_ROWS, :] = ones
    o3 = o2 + KV_W
    ssm_w = u_ref.shape[1]
    u_ref[...] = proj[:, o3:o3 + ssm_w].astype(BF16)
    o4 = o3 + ssm_w
    conv_w = hc_ref.shape[1]
    a = proj[:, o4:o4 + conv_w]
    gate = proj[:, o4 + conv_w:o4 + 2 * conv_w]
    hc_ref[...] = a * jax.nn.sigmoid(gate)


def _inproj(x, mod, norm_pre, w_in_bf, qkn, cos_t, sin_t, bd, *, n_prompt, lc, tm, ssm_w, conv_w):
    t, d = x.shape
    nb = lc // tm
    s = t // lc
    in_w = w_in_bf.shape[1]

    def row(si, i):
        return si * nb + i

    def pos(si, i):
        return jnp.where(si < n_prompt, i, (si - n_prompt) * nb + i)

    return pl.pallas_call(
        _inproj_kernel,
        grid=(s, nb),
        in_specs=[pl.BlockSpec((tm, d), lambda si, i: (row(si, i), 0)),
                  pl.BlockSpec((1, 6, d), lambda si, i: (si, 0, 0)),
                  pl.BlockSpec((1, d), lambda si, i: (0, 0)),
                  pl.BlockSpec((d, in_w), lambda si, i: (0, 0)),
                  pl.BlockSpec((8, LANES), lambda si, i: (0, 0)),
                  pl.BlockSpec((tm, LANES), lambda si, i: (pos(si, i), 0)),
                  pl.BlockSpec((tm, LANES), lambda si, i: (pos(si, i), 0)),
                  pl.BlockSpec((LANES, LANES), lambda si, i: (0, 0))],
        out_specs=[pl.BlockSpec((N_HEADS, HEAD_DIM, tm), lambda si, i: (0, 0, row(si, i))),
                   pl.BlockSpec((N_KV_HEADS, tm, HEAD_DIM), lambda si, i: (0, row(si, i), 0)),
                   pl.BlockSpec((N_KV_HEADS, 1, V_ROWS, tm), lambda si, i: (0, row(si, i), 0, 0)),
                   pl.BlockSpec((tm, ssm_w), lambda si, i: (row(si, i), 0)),
                   pl.BlockSpec((tm, conv_w), lambda si, i: (row(si, i), 0))],
        out_shape=[jax.ShapeDtypeStruct((N_HEADS, HEAD_DIM, t), BF16),
                   jax.ShapeDtypeStruct((N_KV_HEADS, t, HEAD_DIM), BF16),
                   jax.ShapeDtypeStruct((N_KV_HEADS, t // tm, V_ROWS, tm), BF16),
                   jax.ShapeDtypeStruct((t, ssm_w), BF16),
                   jax.ShapeDtypeStruct((t, conv_w), F32)],
        compiler_params=_cparams(("parallel", "arbitrary")),
        name="inproj",
    )(x, mod, norm_pre, w_in_bf, qkn, cos_t, sin_t, bd)


def _attn_kernel(q_ref, k_ref, v_ref, o_ref, s_ref, m_ref, acc_ref, *, tk):
    g, hd, tq = q_ref.shape
    q_t = jnp.concatenate([q_ref[i] for i in range(g)], axis=-1)
    n = k_ref.shape[1] // tk
    m_ref[...] = jnp.full(m_ref.shape, NEG_BIG, F32)
    acc_ref[...] = jnp.zeros(acc_ref.shape, F32)

    def scores(j):
        kc = k_ref[0, pl.ds(pl.multiple_of(j * tk, tk), tk), :]
        return jnp.dot(kc, q_t, preferred_element_type=F32)

    def step(j, slot, prefetch):
        if prefetch:
            s_ref[1 - slot] = scores(j + 1)
        s = s_ref[slot]
        m_prev = m_ref[...]
        m_next = jnp.maximum(m_prev, jnp.max(s, axis=0, keepdims=True))
        alpha = jnp.exp2(m_prev - m_next)
        p = jnp.exp2(s - m_next)
        acc_ref[...] = acc_ref[...] * alpha + jnp.dot(v_ref[0, j], p.astype(BF16), preferred_element_type=F32)
        m_ref[...] = m_next

    s_ref[0] = scores(0)
    n_pairs = (n - 1) // 2

    def body(jj, carry):
        step(2 * jj, 0, True)
        step(2 * jj + 1, 1, True)
        return carry

    lax.fori_loop(0, n_pairs, body, 0)
    if n - 2 * n_pairs == 2:
        step(2 * n_pairs, 0, True)
        step(2 * n_pairs + 1, 1, False)
    else:
        step(2 * n_pairs, 0, False)
    o_t = acc_ref[0:hd, :] * (1.0 / acc_ref[hd:hd + 1, :])
    o_ref[...] = jnp.concatenate([o_t[:, i * tq:(i + 1) * tq] for i in range(g)], axis=0).T.astype(o_ref.dtype)


def _attention(q_t, k, v_t, *, row0, n_seq, lkv, tq, tk):
    nq = lkv // tq
    qb0 = row0 // tq
    kb0 = row0 // lkv
    cols = GQA_GROUP * tq
    return pl.pallas_call(
        functools.partial(_attn_kernel, tk=tk),
        grid=(n_seq, N_KV_HEADS, nq),
        in_specs=[pl.BlockSpec((GQA_GROUP, HEAD_DIM, tq), lambda b, g, i: (g, 0, qb0 + b * nq + i)),
                  pl.BlockSpec((1, lkv, HEAD_DIM), lambda b, g, i: (g, kb0 + b, 0)),
                  pl.BlockSpec((1, lkv // tk, V_ROWS, tk), lambda b, g, i: (g, kb0 + b, 0, 0))],
        out_specs=pl.BlockSpec((tq, GQA_GROUP * HEAD_DIM), lambda b, g, i: (b * nq + i, g)),
        out_shape=jax.ShapeDtypeStruct((n_seq * lkv, ATTN_W), BF16),
        scratch_shapes=[pltpu.VMEM((2, tk, cols), F32), pltpu.VMEM((1, cols), F32),
                        pltpu.VMEM((V_ROWS, cols), F32)],
        compiler_params=_cparams(("parallel", "parallel", "arbitrary")),
        name="attention",
    )(q_t, k, v_t)


def _s5_weights(lam_re, lam_im, log_dt, b_re, b_im, c_re, c_im, d_skip, tc):
    hp = lax.Precision.HIGHEST
    lam = lax.complex(lam_re.astype(F32), lam_im.astype(F32))
    dt = jnp.exp(log_dt.astype(F32))[..., None]
    lam_dt = lam * dt
    lam_bar = jnp.exp(lam_dt)
    b_bar = ((lam_bar - 1.0) / lam)[..., None] * lax.complex(b_re.astype(F32), b_im.astype(F32))
    cmat = lax.complex(c_re.astype(F32), c_im.astype(F32))
    g, p = lam.shape[1], lam.shape[2]
    h = b_bar.shape[-1]
    taus = jnp.arange(tc + 1, dtype=F32)
    pw = jnp.exp(lam_dt[..., None] * taus)
    kern = jnp.einsum('dgop,dgpt,dgpi->dgtoi', cmat, pw, b_bar, precision=hp).real
    eye_h = jnp.eye(h, dtype=F32)
    kd = kern[0][:, 0] + kern[1][:, 0] + d_skip.astype(F32).reshape(g, h)[:, :, None] * eye_h
    by_lag = jnp.concatenate([jnp.flip(kern[1][:, 1:tc], axis=1), kd[:, None], kern[0][:, 1:tc]], axis=1)
    by_lag = by_lag.transpose(0, 2, 3, 1)
    padded = jnp.concatenate([by_lag, jnp.zeros((g, h, h, 1), F32)], axis=-1)
    skew = jnp.tile(padded, (1, 1, 1, tc))[..., :tc * (2 * tc - 1)].reshape(g, h, h, tc, 2 * tc - 1)
    full = skew[..., tc - 1:]
    toep = full.transpose(0, 3, 2, 4, 1).reshape(g, tc * h, tc * h)

    pf = jnp.flip(pw[0][:, :, :tc], axis=2)
    pb = pw[1][:, :, :tc]
    sf = pf[:, :, :, None] * b_bar[0][:, :, None, :]
    sb = pb[:, :, :, None] * b_bar[1][:, :, None, :]
    parts = [sf.real, sf.imag, sb.real, sb.imag]
    ws = jnp.stack([x.transpose(0, 2, 3, 1).reshape(g, tc * h, p) for x in parts], axis=2)
    odd = (jnp.arange(g) % 2)[:, None, None, None]
    zeros = jnp.zeros_like(ws)
    ws = jnp.where(odd == 0, jnp.concatenate([ws, zeros], -1), jnp.concatenate([zeros, ws], -1))

    ef = cmat[0].transpose(0, 2, 1)[:, :, None, :] * pw[0][:, :, 1:tc + 1][:, :, :, None]
    eb = cmat[1].transpose(0, 2, 1)[:, :, None, :] * jnp.flip(pw[1][:, :, 1:tc + 1], axis=2)[:, :, :, None]
    cparts = [ef.real, -ef.imag, eb.real, -eb.imag]
    wc = jnp.stack([x.reshape(g, p, tc * h) for x in cparts], axis=1)
    zc = jnp.zeros_like(wc)
    odd4 = (jnp.arange(g) % 2)[:, None, None, None]
    wc = jnp.where(odd4 == 0, jnp.concatenate([wc, zc], 2), jnp.concatenate([zc, wc], 2))
    wc = wc.reshape(g, 4 * 2 * p, tc * h)

    af = pw[0][:, :, tc].reshape(1, g * p)
    ab = pw[1][:, :, tc].reshape(1, g * p)
    decay = jnp.concatenate([af.real, af.imag, ab.real, ab.imag], axis=0)
    return toep.astype(BF16), ws.reshape(g, tc * h, 4 * 2 * p).astype(BF16), wc.astype(BF16), decay


def _s5_intra_kernel(u_ref, toep_ref, ws_ref, y_ref, s0_ref, s1_ref, s2_ref, s3_ref):
    u0 = u_ref[0]
    u1 = u_ref[1]
    y_ref[0] = jnp.dot(u0, toep_ref[0], preferred_element_type=F32)
    y_ref[1] = jnp.dot(u1, toep_ref[1], preferred_element_type=F32)
    st = (jnp.dot(u0, ws_ref[0], preferred_element_type=F32)
          + jnp.dot(u1, ws_ref[1], preferred_element_type=F32))
    for k, ref in enumerate((s0_ref, s1_ref, s2_ref, s3_ref)):
        ref[...] = st[:, k * LANES:(k + 1) * LANES]


def _s5_intra(ug, toep, ws, *, tr):
    g, nc, kw = ug.shape
    npair = g // 2
    st_shape = jax.ShapeDtypeStruct((nc, npair * LANES), F32)
    st_spec = pl.BlockSpec((tr, LANES), lambda pr, r: (r, pr))
    return pl.pallas_call(
        _s5_intra_kernel,
        grid=(npair, nc // tr),
        in_specs=[pl.BlockSpec((2, tr, kw), lambda pr, r: (pr, r, 0)),
                  pl.BlockSpec((2, kw, kw), lambda pr, r: (pr, 0, 0)),
                  pl.BlockSpec((2, kw, 4 * LANES), lambda pr, r: (pr, 0, 0))],
        out_specs=[pl.BlockSpec((2, tr, kw), lambda pr, r: (pr, r, 0)), st_spec, st_spec, st_spec, st_spec],
        out_shape=[jax.ShapeDtypeStruct((g, nc, kw), F32), st_shape, st_shape, st_shape, st_shape],
        compiler_params=_cparams(("parallel", "arbitrary")),
        name="s5_intra",
    )(ug, toep, ws)


def _s5_scan_kernel(decay_ref, cfr_ref, cfi_ref, cbr_ref, cbi_ref, sfr_ref, sfi_ref, sbr_ref, sbi_ref):
    n = cfr_ref.shape[0]
    w = cfr_ref.shape[1]
    afr = decay_ref[0:1, :]
    afi = decay_ref[1:2, :]
    abr = decay_ref[2:3, :]
    abi = decay_ref[3:4, :]
    zero = jnp.zeros((1, w), F32)

    def fwd(c, st):
        re, im = st
        sfr_ref[pl.ds(c, 1), :] = re
        sfi_ref[pl.ds(c, 1), :] = im
        cr = cfr_ref[pl.ds(c, 1), :]
        ci = cfi_ref[pl.ds(c, 1), :]
        return afr * re - afi * im + cr, afr * im + afi * re + ci

    lax.fori_loop(0, n, fwd, (zero, zero))

    def bwd(kk, st):
        c = n - 1 - kk
        re, im = st
        sbr_ref[pl.ds(c, 1), :] = re
        sbi_ref[pl.ds(c, 1), :] = im
        cr = cbr_ref[pl.ds(c, 1), :]
        ci = cbi_ref[pl.ds(c, 1), :]
        return abr * re - abi * im + cr, abr * im + abi * re + ci

    lax.fori_loop(0, n, bwd, (zero, zero))


def _s5_scan(decay, contribs, *, chunk0, n_seq, cps):
    w = contribs[0].shape[1]
    b0 = chunk0 // cps
    spec_in = pl.BlockSpec((cps, w), lambda b: (b0 + b, 0))
    spec_out = pl.BlockSpec((cps, w), lambda b: (b, 0))
    shape = jax.ShapeDtypeStruct((n_seq * cps, w), F32)
    return pl.pallas_call(
        _s5_scan_kernel,
        grid=(n_seq,),
        in_specs=[pl.BlockSpec((4, w), lambda b: (0, 0))] + [spec_in] * 4,
        out_specs=[spec_out] * 4,
        out_shape=[shape] * 4,
        compiler_params=_cparams(("parallel",)),
        name="s5_scan",
    )(decay, *contribs)


def _s5_out_kernel(y_ref, s0_ref, s1_ref, s2_ref, s3_ref, wc_ref, o_ref):
    st = jnp.concatenate([s0_ref[...], s1_ref[...], s2_ref[...], s3_ref[...]], axis=-1).astype(BF16)
    for k in range(2):
        y = y_ref[k] + jnp.dot(st, wc_ref[k], preferred_element_type=F32)
        o_ref[k] = 0.5 * y * (1.0 + lax.erf(y * (2.0 ** -0.5)))


def _s5_out(y_intra, states, wc, *, tr):
    g, nc, kw = y_intra.shape
    npair = g // 2
    st_spec = pl.BlockSpec((tr, LANES), lambda pr, r: (r, pr))
    return pl.pallas_call(
        _s5_out_kernel,
        grid=(npair, nc // tr),
        in_specs=[pl.BlockSpec((2, tr, kw), lambda pr, r: (pr, r, 0)), st_spec, st_spec, st_spec, st_spec,
                  pl.BlockSpec((2, 4 * LANES, kw), lambda pr, r: (pr, 0, 0))],
        out_specs=pl.BlockSpec((2, tr, kw), lambda pr, r: (pr, r, 0)),
        out_shape=jax.ShapeDtypeStruct((g, nc, kw), F32),
        compiler_params=_cparams(("parallel", "arbitrary")),
        name="s5_out",
    )(y_intra, *states, wc)


def _conv_kernel(prev_ref, cur_ref, next_ref, wdw_ref, bdw_ref, lng_ref, lnb_ref, wpw_ref, bn_ref, o_ref, scr,
                 *, n_prompt, n_seq, nb, rt):
    si = pl.program_id(0)
    i = pl.program_id(1)
    tm = cur_ref.shape[0]
    prev_ok = jnp.logical_or(i > 0, si > n_prompt)
    next_ok = jnp.logical_or(i < nb - 1, jnp.logical_and(si >= n_prompt, si < n_seq - 1))
    scr[0:HALO, :] = jnp.where(prev_ok, prev_ref[...], 0.0)
    scr[HALO:HALO + tm, :] = cur_ref[...]
    scr[HALO + tm:2 * HALO + tm, :] = jnp.where(next_ok, next_ref[...], 0.0)
    off = HALO - CONV_PAD
    for r0 in range(0, tm, rt):
        acc = wdw_ref[0:1, :] * scr[r0 + off:r0 + off + rt, :]
        for kk in range(1, CONV_KERNEL):
            acc = acc + wdw_ref[kk:kk + 1, :] * scr[r0 + off + kk:r0 + off + kk + rt, :]
        acc = acc + bdw_ref[...]
        mu = jnp.mean(acc, axis=-1, keepdims=True)
        cen = acc - mu
        var = jnp.mean(cen * cen, axis=-1, keepdims=True)
        y = cen * lax.rsqrt(var + EPS) * lng_ref[...] + lnb_ref[...]
        y = y * jax.nn.sigmoid(y)
        z = jnp.dot(y.astype(BF16), wpw_ref[...], preferred_element_type=F32)
        o_ref[r0:r0 + rt, :] = (z * _rsqrt_mean_sq(z) * bn_ref[...]).astype(o_ref.dtype)


def _conv_module(hc, w_dw, b_dw, ln_g, ln_b, w_pw_bf, bn_conv, *, n_prompt, lc, tm):
    t, cw = hc.shape
    nb = lc // tm
    s = t // lc
    hb = tm // HALO
    n_hblk = t // HALO
    kpad = 32

    def row(si, i):
        return si * nb + i

    wdw = jnp.zeros((kpad, cw), F32).at[:CONV_KERNEL].set(w_dw.astype(F32))
    vec = lambda a: a.astype(F32).reshape(1, cw)
    const = lambda shape: pl.BlockSpec(shape, lambda si, i: (0, 0))
    return pl.pallas_call(
        functools.partial(_conv_kernel, n_prompt=n_prompt, n_seq=s, nb=nb, rt=min(128, tm)),
        grid=(s, nb),
        in_specs=[pl.BlockSpec((HALO, cw), lambda si, i: (jnp.maximum(row(si, i) * hb - 1, 0), 0)),
                  pl.BlockSpec((tm, cw), lambda si, i: (row(si, i), 0)),
                  pl.BlockSpec((HALO, cw), lambda si, i: (jnp.minimum((row(si, i) + 1) * hb, n_hblk - 1), 0)),
                  const((kpad, cw)), const((1, cw)), const((1, cw)), const((1, cw)), const((cw, cw)),
                  const((1, cw))],
        out_specs=pl.BlockSpec((tm, cw), lambda si, i: (row(si, i), 0)),
        out_shape=jax.ShapeDtypeStruct((t, cw), BF16),
        scratch_shapes=[pltpu.VMEM((tm + 2 * HALO, cw), F32)],
        compiler_params=_cparams(("parallel", "arbitrary")),
        name="conv_module",
    )(hc, hc, hc, wdw, vec(b_dw), vec(ln_g), vec(ln_b), w_pw_bf, vec(bn_conv))


def _outproj_kernel(x_ref, attn_ref, ys_ref, cn_ref, mod_ref, bn_ref, wglu_ref, bglu_ref, wout_ref, npost_ref,
                    npre2_ref, rw_ref, rb_ref, x1_ref, h2_ref, ti_ref, tg_ref):
    aw = attn_ref.shape[1]
    sw = ys_ref.shape[1]
    a = attn_ref[...].astype(F32)
    an = a * _rsqrt_mean_sq(a) * bn_ref[:, 0:aw]
    y = ys_ref[...]
    z = jnp.dot(y.astype(BF16), wglu_ref[...], preferred_element_type=F32) + bglu_ref[...]
    sm = y * jax.nn.sigmoid(z)
    sn = sm * _rsqrt_mean_sq(sm) * bn_ref[:, aw:aw + sw]
    mix = (jnp.dot(an.astype(BF16), wout_ref[0:aw, :], preferred_element_type=F32)
           + jnp.dot(sn.astype(BF16), wout_ref[aw:aw + sw, :], preferred_element_type=F32)
           + jnp.dot(cn_ref[...], wout_ref[aw + sw:, :], preferred_element_type=F32))
    g1 = mod_ref[0, 2:3, :]
    sh2 = mod_ref[0, 3:4, :]
    sc2 = mod_ref[0, 4:5, :]
    x1 = x_ref[...] + g1 * (mix * _rsqrt_mean_sq(mix) * npost_ref[...])
    x1_ref[...] = x1
    h2 = x1 * _rsqrt_mean_sq(x1) * npre2_ref[...] * (1.0 + sc2) + sh2
    h2_ref[...] = h2
    h_hi = h2.astype(BF16)
    h_lo = (h2 - h_hi.astype(F32)).astype(BF16)
    both = jnp.dot(h_hi, rw_ref[...], preferred_element_type=F32)
    logits = (both[:, :LANES] + both[:, LANES:]
              + jnp.dot(h_lo, rw_ref[:, :LANES], preferred_element_type=F32) + rb_ref[...])
    tm = logits.shape[0]
    lane = lax.broadcasted_iota(I32, (tm, LANES), 1)
    vals = []
    ti = jnp.zeros((tm, LANES), I32)
    for kk in range(TOP_K):
        m = jnp.max(logits, axis=1, keepdims=True)
        ik = jnp.min(jnp.where(logits == m, lane, LANES), axis=1, keepdims=True)
        vals.append(m)
        ti = jnp.where(lane == kk, ik, ti)
        logits = jnp.where(lane == ik, NEG_BIG * 2.0, logits)
    es = [jnp.exp(v - vals[0]) for v in vals]
    inv = 1.0 / (es[0] + es[1] + es[2] + es[3])
    tg = jnp.zeros((tm, LANES), F32)
    for kk in range(TOP_K):
        tg = jnp.where(lane == kk, es[kk] * inv, tg)
    ti_ref[...] = ti
    tg_ref[...] = tg


def _outproj(x, attn, ys, cn, mod, bn, wglu_bf, bglu, wout_bf, npost, npre2, rw_pad, rb_pad, *, lc, tm):
    t, d = x.shape
    nb = lc // tm
    s = t // lc
    aw, sw, cw = attn.shape[1], ys.shape[1], cn.shape[1]

    def row(si, i):
        return si * nb + i

    rows = lambda w: pl.BlockSpec((tm, w), lambda si, i: (row(si, i), 0))
    const = lambda shape: pl.BlockSpec(shape, lambda si, i: (0, 0))
    return pl.pallas_call(
        _outproj_kernel,
        grid=(s, nb),
        in_specs=[rows(d), rows(aw), rows(sw), rows(cw),
                  pl.BlockSpec((1, 6, d), lambda si, i: (si, 0, 0)),
                  const((1, d)), const((sw, sw)), const((1, sw)), const((d, d)), const((1, d)), const((1, d)),
                  const((d, 2 * LANES)), const((1, LANES))],
        out_specs=[rows(d), rows(d), rows(LANES), rows(LANES)],
        out_shape=[jax.ShapeDtypeStruct((t, d), F32), jax.ShapeDtypeStruct((t, d), F32),
                   jax.ShapeDtypeStruct((t, LANES), I32), jax.ShapeDtypeStruct((t, LANES), F32)],
        compiler_params=_cparams(("parallel", "arbitrary")),
        name="outproj_router",
    )(x, attn, ys, cn, mod, bn, wglu_bf, bglu, wout_bf, npost, npre2, rw_pad, rb_pad)


def _route(topi, blk):
    t = topi.shape[0]
    n_assign = t * TOP_K
    flat_e = topi.reshape(-1)
    eids = jnp.arange(N_EXPERTS, dtype=I32)
    order = jnp.argsort(flat_e, stable=True).astype(I32)
    counts = jnp.sum((flat_e[:, None] == eids[None, :]).astype(I32), axis=0)
    starts = jnp.cumsum(counts) - counts
    padded = ((counts + blk - 1) // blk) * blk
    ends_p = jnp.cumsum(padded)
    first_blk = (ends_p - padded) // blk
    n_blocks = n_assign // blk + N_EXPERTS
    n_used = ends_p[-1] // blk
    bidx = jnp.arange(n_blocks, dtype=I32)
    be_raw = jnp.sum((ends_p[None, :] <= (bidx * blk)[:, None]).astype(I32), axis=1)
    last_e = jnp.max(jnp.where(counts > 0, eids, 0))
    block_e = jnp.where(bidx < n_used, jnp.minimum(be_raw, N_EXPERTS - 1), last_e).astype(I32)
    off = (bidx - first_blk[block_e]) * blk
    seg0 = jnp.clip(starts[block_e] + off, 0, n_assign)
    order_pad = jnp.concatenate([order, jnp.zeros((blk,), I32)])
    a_blk = jax.vmap(lambda s0: lax.dynamic_slice(order_pad, (s0,), (blk,)))(seg0)
    r = jnp.arange(blk, dtype=I32)[None, :]
    valid = jnp.logical_and(off[:, None] + r < counts[block_e][:, None], (bidx < n_used)[:, None])
    tok = jnp.where(valid, a_blk // TOP_K, 0).astype(I32)
    spare = n_assign + (bidx % 2)[:, None] * blk + r
    dst = jnp.where(valid, (a_blk % TOP_K) * t + a_blk // TOP_K, spare).astype(I32)
    return tok, dst, block_e


def _moe_kernel(be_ref, tok0_ref, tokn_ref, dst_ref, h_hbm, w1_ref, b1_ref, w2_ref, b2_ref, yk_hbm,
                xbuf, ybuf, w1b, w2b, gsem, ssem):
    b = pl.program_id(0)
    nb = pl.num_programs(0)
    blk = xbuf.shape[1]
    de = w2_ref.shape[1]
    slot = b % 2
    other = 1 - slot

    def gather(idx_ref, s):
        for r0 in range(0, blk, DMA_BATCH):
            rows = [idx_ref[0, 0, r0 + i] for i in range(DMA_BATCH)]
            for i, row in enumerate(rows):
                pltpu.make_async_copy(h_hbm.at[pl.ds(row, 1)], xbuf.at[s, pl.ds(r0 + i, 1)], gsem.at[s]).start()

    def wait_gather(s):
        pltpu.make_async_copy(h_hbm.at[pl.ds(0, blk)], xbuf.at[s], gsem.at[s]).wait()

    def wait_scatter(s):
        pltpu.make_async_copy(ybuf.at[s], yk_hbm.at[pl.ds(0, blk)], ssem.at[s]).wait()

    @pl.when(b == 0)
    def _():
        gather(tok0_ref, 0)

    wait_gather(slot)

    @pl.when(b >= 2)
    def _():
        wait_scatter(slot)

    @pl.when(jnp.logical_or(b == 0, be_ref[b] != be_ref[jnp.maximum(b - 1, 0)]))
    def _():
        w1b[...] = w1_ref[0].astype(BF16)
        w2b[...] = w2_ref[0].astype(BF16)

    a = jnp.dot(xbuf[slot].astype(BF16), w1b[...], preferred_element_type=F32) + b1_ref[0]
    gather(tokn_ref, other)
    gt = jnp.minimum(a[:, :de], SWIGLU_LIMIT)
    up = jnp.clip(a[:, de:], -SWIGLU_LIMIT, SWIGLU_LIMIT)
    act = (up + 1.0) * (gt * jax.nn.sigmoid(SWIGLU_ALPHA * gt))
    ybuf[slot] = jnp.dot(act.astype(BF16), w2b[...], preferred_element_type=F32) + b2_ref[0]
    for r0 in range(0, blk, DMA_BATCH):
        rows = [dst_ref[0, 0, r0 + i] for i in range(DMA_BATCH)]
        for i, row in enumerate(rows):
            pltpu.make_async_copy(ybuf.at[slot, pl.ds(r0 + i, 1)], yk_hbm.at[pl.ds(row, 1)],
                                  ssem.at[slot]).start(priority=1)

    @pl.when(b == nb - 1)
    def _():
        wait_gather(other)
        wait_scatter(other)
        wait_scatter(slot)


def _moe_experts(h2, tok, dst, block_e, w1, b1, w2, b2, *, blk):
    t, d = h2.shape
    ne, _, de2 = w1.shape
    de = w2.shape[1]
    nblk = block_e.shape[0]
    assert nblk >= 2
    idx_spec = lambda f: pl.BlockSpec((1, 1, blk), f, memory_space=pltpu.SMEM)
    grid_spec = pltpu.PrefetchScalarGridSpec(
        num_scalar_prefetch=1,
        grid=(nblk,),
        in_specs=[idx_spec(lambda b, be: (0, 0, 0)),
                  idx_spec(lambda b, be: (jnp.minimum(b + 1, nblk - 1), 0, 0)),
                  idx_spec(lambda b, be: (b, 0, 0)),
                  pl.BlockSpec(memory_space=pl.ANY),
                  pl.BlockSpec((1, d, de2), lambda b, be: (be[b], 0, 0)),
                  pl.BlockSpec((1, 1, de2), lambda b, be: (be[b], 0, 0)),
                  pl.BlockSpec((1, de, d), lambda b, be: (be[b], 0, 0)),
                  pl.BlockSpec((1, 1, d), lambda b, be: (be[b], 0, 0))],
        out_specs=pl.BlockSpec(memory_space=pl.ANY),
        scratch_shapes=[pltpu.VMEM((2, blk, d), F32), pltpu.VMEM((2, blk, d), F32),
                        pltpu.VMEM((d, de2), BF16), pltpu.VMEM((de, d), BF16),
                        pltpu.SemaphoreType.DMA((2,)), pltpu.SemaphoreType.DMA((2,))],
    )
    tok3 = tok.reshape(nblk, 1, blk)
    return pl.pallas_call(
        _moe_kernel,
        grid_spec=grid_spec,
        out_shape=jax.ShapeDtypeStruct((TOP_K * t + 2 * blk, d), F32),
        compiler_params=_cparams(("arbitrary",), vmem_limit_bytes=EXPERT_VMEM_LIMIT, disable_bounds_checks=True),
        name="moe_experts",
    )(block_e, tok3, tok3, dst.reshape(nblk, 1, blk), h2, w1, b1.reshape(ne, 1, de2), w2, b2.reshape(ne, 1, d))


def _combine_kmajor_kernel(y0_ref, y1_ref, y2_ref, y3_ref, gate_ref, x1_ref, mod_ref, npost_ref, o_ref):
    gate = gate_ref[...]
    ff = gate[:, 0:1] * y0_ref[...]
    for kk, ref in enumerate((y1_ref, y2_ref, y3_ref), start=1):
        ff = ff + gate[:, kk:kk + 1] * ref[...]
    g2 = mod_ref[0, 5:6, :]
    o_ref[...] = x1_ref[...] + g2 * (ff * _rsqrt_mean_sq(ff) * npost_ref[...])


def _combine_kmajor(yk, gate, x1, mod, npost, *, lc, tm):
    t, d = x1.shape
    nb = lc // tm
    s = t // lc
    nbt = t // tm

    def row(si, i):
        return si * nb + i

    def yspec(kk):
        return pl.BlockSpec((tm, d), lambda si, i: (kk * nbt + row(si, i), 0))

    return pl.pallas_call(
        _combine_kmajor_kernel,
        grid=(s, nb),
        in_specs=[yspec(0), yspec(1), yspec(2), yspec(3),
                  pl.BlockSpec((tm, LANES), lambda si, i: (row(si, i), 0)),
                  pl.BlockSpec((tm, d), lambda si, i: (row(si, i), 0)),
                  pl.BlockSpec((1, 6, d), lambda si, i: (si, 0, 0)),
                  pl.BlockSpec((1, d), lambda si, i: (0, 0))],
        out_specs=pl.BlockSpec((tm, d), lambda si, i: (row(si, i), 0)),
        out_shape=jax.ShapeDtypeStruct((t, d), F32),
        compiler_params=_cparams(("parallel", "arbitrary")),
        name="moe_combine",
    )(yk, yk, yk, yk, gate, x1, mod, npost)


def _dispatch_kernel(fb_ref, fn_ref, dest_ref, h_ref, xs_hbm, zrow, sem, zsem):
    step = pl.program_id(0)
    tm = h_ref.shape[0]

    @pl.when(step == 0)
    def _():
        zrow[...] = jnp.zeros(zrow.shape, zrow.dtype)
        for e in range(N_EXPERTS + 1):
            base = fb_ref[e]
            n = fn_ref[e]

            def zissue(j, carry):
                pltpu.make_async_copy(zrow.at[pl.ds(0, 1)], xs_hbm.at[pl.ds(base + j, 1)], zsem).start()
                return carry

            def zwait(j, carry):
                pltpu.make_async_copy(zrow.at[pl.ds(0, 1)], xs_hbm.at[pl.ds(0, 1)], zsem).wait()
                return carry

            lax.fori_loop(0, n, zissue, 0)
            lax.fori_loop(0, n, zwait, 0)

    def issue(r, carry):
        for kk in range(TOP_K):
            pltpu.make_async_copy(h_ref.at[pl.ds(r, 1)],
                                  xs_hbm.at[pl.ds(dest_ref[0, 0, r * TOP_K + kk], 1)], sem).start(priority=kk % 2)
        return carry

    lax.fori_loop(0, tm, issue, 0, unroll=4)
    for kk in range(TOP_K):
        pltpu.make_async_copy(h_ref, xs_hbm.at[pl.ds(0, tm)], sem).wait()


def _dispatch(h2, dest, fill_base, fill_n, *, cap, tm):
    t, d = h2.shape
    grid_spec = pltpu.PrefetchScalarGridSpec(
        num_scalar_prefetch=2,
        grid=(t // tm,),
        in_specs=[pl.BlockSpec((1, 1, tm * TOP_K), lambda i, fb, fn: (i, 0, 0), memory_space=pltpu.SMEM),
                  pl.BlockSpec((tm, d), lambda i, fb, fn: (i, 0))],
        out_specs=pl.BlockSpec(memory_space=pl.ANY),
        scratch_shapes=[pltpu.VMEM((8, d), F32), pltpu.SemaphoreType.DMA(()), pltpu.SemaphoreType.DMA(())],
    )
    return pl.pallas_call(
        _dispatch_kernel,
        grid_spec=grid_spec,
        out_shape=jax.ShapeDtypeStruct((cap, d), F32),
        compiler_params=_cparams(("arbitrary",), disable_bounds_checks=True),
        name="moe_dispatch",
    )(fill_base, fill_n, dest.reshape(t // tm, 1, tm * TOP_K), h2)


def _expert_kernel(be_ref, nu_ref, xs_ref, w1_ref, b1_ref, w2_ref, b2_ref, ys_ref, w1b, w2b):
    b = pl.program_id(0)
    de = w2_ref.shape[1]
    used = b < nu_ref[0]
    fresh = jnp.logical_or(b == 0, be_ref[b] != be_ref[jnp.maximum(b - 1, 0)])

    @pl.when(jnp.logical_and(used, fresh))
    def _():
        w1b[...] = w1_ref[0].astype(BF16)
        w2b[...] = w2_ref[0].astype(BF16)

    @pl.when(used)
    def _():
        a = jnp.dot(xs_ref[...].astype(BF16), w1b[...], preferred_element_type=F32) + b1_ref[0]
        gt = jnp.minimum(a[:, :de], SWIGLU_LIMIT)
        up = jnp.clip(a[:, de:], -SWIGLU_LIMIT, SWIGLU_LIMIT)
        act = (up + 1.0) * (gt * jax.nn.sigmoid(SWIGLU_ALPHA * gt))
        ys_ref[...] = jnp.dot(act.astype(BF16), w2b[...], preferred_element_type=F32) + b2_ref[0]

    @pl.when(jnp.logical_not(used))
    def _():
        ys_ref[...] = jnp.zeros(ys_ref.shape, ys_ref.dtype)


def _experts(xs, block_e, n_used, w1, b1, w2, b2, *, blk):
    cap, d = xs.shape
    ne, _, de2 = w1.shape
    de = w2.shape[1]
    nblk = cap // blk

    def xrow(b, be, nu):
        return (jnp.minimum(b, nu[0] - 1), 0)

    grid_spec = pltpu.PrefetchScalarGridSpec(
        num_scalar_prefetch=2,
        grid=(nblk,),
        in_specs=[pl.BlockSpec((blk, d), xrow),
                  pl.BlockSpec((1, d, de2), lambda b, be, nu: (be[b], 0, 0)),
                  pl.BlockSpec((1, 1, de2), lambda b, be, nu: (be[b], 0, 0)),
                  pl.BlockSpec((1, de, d), lambda b, be, nu: (be[b], 0, 0)),
                  pl.BlockSpec((1, 1, d), lambda b, be, nu: (be[b], 0, 0))],
        out_specs=pl.BlockSpec((blk, d), lambda b, be, nu: (b, 0)),
        scratch_shapes=[pltpu.VMEM((d, de2), BF16), pltpu.VMEM((de, d), BF16)],
    )
    return pl.pallas_call(
        _expert_kernel,
        grid_spec=grid_spec,
        out_shape=jax.ShapeDtypeStruct((cap, d), F32),
        compiler_params=_cparams(("arbitrary",), vmem_limit_bytes=EXPERT_VMEM_LIMIT),
        name="moe_experts",
    )(block_e, n_used, xs, w1, b1.reshape(ne, 1, de2), w2, b2.reshape(ne, 1, d))


def _combine_kernel(dest_ref, gate_ref, x1_ref, mod_ref, npost_ref, ys_hbm, o_ref, buf, sem):
    tm = x1_ref.shape[0]

    def issue(r, carry):
        for kk in range(TOP_K):
            pltpu.make_async_copy(ys_hbm.at[pl.ds(dest_ref[0, 0, r * TOP_K + kk], 1)],
                                  buf.at[kk, pl.ds(r, 1)], sem).start()
        return carry

    lax.fori_loop(0, tm, issue, 0, unroll=4)
    for kk in range(TOP_K):
        pltpu.make_async_copy(ys_hbm.at[pl.ds(0, tm)], buf.at[kk], sem).wait()
    gate = gate_ref[...]
    ff = gate[:, 0:1] * buf[0]
    for kk in range(1, TOP_K):
        ff = ff + gate[:, kk:kk + 1] * buf[kk]
    g2 = mod_ref[0, 5:6, :]
    o_ref[...] = x1_ref[...] + g2 * (ff * _rsqrt_mean_sq(ff) * npost_ref[...])


def _combine(ys, dest, gate, x1, mod, npost, *, lc, tm):
    t, d = x1.shape
    nb = lc // tm
    s = t // lc

    def row(si, i):
        return si * nb + i

    return pl.pallas_call(
        _combine_kernel,
        grid=(s, nb),
        in_specs=[pl.BlockSpec((1, 1, tm * TOP_K), lambda si, i: (row(si, i), 0, 0), memory_space=pltpu.SMEM),
                  pl.BlockSpec((tm, LANES), lambda si, i: (row(si, i), 0)),
                  pl.BlockSpec((tm, d), lambda si, i: (row(si, i), 0)),
                  pl.BlockSpec((1, 6, d), lambda si, i: (si, 0, 0)),
                  pl.BlockSpec((1, d), lambda si, i: (0, 0)),
                  pl.BlockSpec(memory_space=pl.ANY)],
        out_specs=pl.BlockSpec((tm, d), lambda si, i: (row(si, i), 0)),
        out_shape=jax.ShapeDtypeStruct((t, d), F32),
        scratch_shapes=[pltpu.VMEM((TOP_K, tm, d), F32), pltpu.SemaphoreType.DMA(())],
        compiler_params=_cparams(("arbitrary", "arbitrary"), disable_bounds_checks=True),
        name="moe_combine",
    )(dest.reshape(t // tm, 1, tm * TOP_K), gate, x1, mod, npost, ys)


def _rope_tables(n_pos):
    rows = n_pos // GRID_W
    row_idx = jnp.repeat(jnp.arange(rows, dtype=F32), GRID_W)
    col_idx = jnp.tile(jnp.arange(GRID_W, dtype=F32), rows)
    inv = 1.0 / (ROPE_THETA ** (jnp.arange(0, ROPE_AXIS_DIM, 2, dtype=F32) / ROPE_AXIS_DIM))
    ar = row_idx[:, None] * inv[None, :]
    ac = col_idx[:, None] * inv[None, :]
    cos = jnp.concatenate([jnp.cos(ar), jnp.cos(ar), jnp.cos(ac), jnp.cos(ac)], axis=-1)
    sin = jnp.concatenate([-jnp.sin(ar), jnp.sin(ar), -jnp.sin(ac), jnp.sin(ac)], axis=-1)
    reps = LANES // HEAD_DIM
    return jnp.tile(cos, (1, reps)), jnp.tile(sin, (1, reps))


def _tiles(lc):
    tm = min(512, lc)
    tq = min(256, lc)
    return tm, tq


def kernel(x_prompt, x_sample, c_prompt, c_sample, w_ada, b_ada, norm_pre_mix, norm_post_mix, norm_pre_ffn, norm_post_ffn, w_in, q_norm, k_norm, ssm_lam_re, ssm_lam_im, ssm_log_dt, ssm_b_re, ssm_b_im, ssm_c_re, ssm_c_im, ssm_d, ssm_w_glu, ssm_b_glu, conv_w_dw, conv_b_dw, conv_ln_g, conv_ln_b, conv_w_pw, branch_norm, w_out, router_w, router_b, expert_w1, expert_b1, expert_w2, expert_b2):
    bsz, lc, d = x_prompt.shape
    dbsz, dl, _ = x_sample.shape
    assert dbsz == 1 and dl % lc == 0 and lc % S5_CHUNK == 0 and lc % GRID_W == 0
    n_sample = dl // lc
    n_seq = bsz + n_sample
    t_prompt = bsz * lc
    t = t_prompt + dl
    depth = w_ada.shape[0]
    ssm_w = ssm_d.shape[1]
    conv_w = conv_b_dw.shape[1]
    n_groups = ssm_w // SSM_GROUP
    tm, tq = _tiles(lc)
    tc = S5_CHUNK
    nc = t // tc
    cps = lc // tc
    kw = tc * SSM_GROUP
    assert t_prompt % dl == 0
    blk = min(MOE_BLOCK, t * TOP_K // N_EXPERTS)

    x = jnp.concatenate([x_prompt.reshape(t_prompt, d), x_sample.reshape(dl, d)], axis=0)
    c_all = jnp.concatenate([c_prompt, jnp.repeat(c_sample, n_sample, axis=0)], axis=0)
    cos_t, sin_t = _rope_tables(max(lc, dl))
    bd = (jnp.arange(LANES)[:, None] // HEAD_DIM == jnp.arange(LANES)[None, :] // HEAD_DIM).astype(BF16)
    reps = LANES // HEAD_DIM
    vec = lambda a: a.astype(F32).reshape(1, -1)
    s5_w = jax.vmap(lambda *p: _s5_weights(*p, tc))(ssm_lam_re, ssm_lam_im, ssm_log_dt, ssm_b_re, ssm_b_im,
                                                    ssm_c_re, ssm_c_im, ssm_d)

    for li in range(depth):
        mod = _modulation(c_all, w_ada[li], b_ada[li]).reshape(n_seq, 6, d)
        qkn = jnp.concatenate([jnp.tile(vec(q_norm[li]) * (HEAD_DIM ** -0.5 * math.log2(math.e)),
                                        (ATTN_W // LANES, reps)),
                               jnp.tile(vec(k_norm[li]), (1, reps)),
                               jnp.zeros((8 - ATTN_W // LANES - 1, LANES), F32)], axis=0)
        q_t, k, v_t, u, hc = _inproj(x, mod, vec(norm_pre_mix[li]), w_in[li].astype(BF16), qkn, cos_t, sin_t, bd,
                                     n_prompt=bsz, lc=lc, tm=tm, ssm_w=ssm_w, conv_w=conv_w)

        attn = jnp.concatenate([
            _attention(q_t, k, v_t, row0=0, n_seq=bsz, lkv=lc, tq=tq, tk=tm),
            _attention(q_t, k, v_t, row0=t_prompt, n_seq=1, lkv=dl, tq=tq, tk=tm)], axis=0)

        toep, ws, wc, decay = [w[li] for w in s5_w]
        ug = u.reshape(nc, tc, n_groups, SSM_GROUP).transpose(2, 0, 1, 3).reshape(n_groups, nc, kw)
        tr = math.gcd(nc, 384)
        y_intra, *contribs = _s5_intra(ug, toep, ws, tr=tr)
        st_p = _s5_scan(decay, contribs, chunk0=0, n_seq=bsz, cps=cps)
        st_s = _s5_scan(decay, contribs, chunk0=t_prompt // tc, n_seq=1, cps=dl // tc)
        states = [jnp.concatenate([a, b], axis=0) for a, b in zip(st_p, st_s)]
        yg = _s5_out(y_intra, states, wc, tr=tr)
        ys = yg.reshape(n_groups, nc, tc, SSM_GROUP).transpose(1, 2, 0, 3).reshape(t, ssm_w)

        bn = vec(branch_norm[li])
        cn = _conv_module(hc, conv_w_dw[li], conv_b_dw[li], conv_ln_g[li], conv_ln_b[li],
                          conv_w_pw[li].astype(BF16), bn[0, ATTN_W + ssm_w:], n_prompt=bsz, lc=lc, tm=tm)

        rw_pad = jnp.zeros((d, LANES), F32).at[:, :N_EXPERTS].set(router_w[li].astype(F32))
        rb_pad = jnp.full((1, LANES), NEG_BIG, F32).at[0, :N_EXPERTS].set(router_b[li].astype(F32))
        rw_hi = rw_pad.astype(BF16)
        rw_pad = jnp.concatenate([rw_hi, (rw_pad - rw_hi.astype(F32)).astype(BF16)], axis=1)
        x1, h2, topi, topg = _outproj(x, attn, ys, cn, mod, bn, ssm_w_glu[li].astype(BF16), vec(ssm_b_glu[li]),
                                      w_out[li].astype(BF16), vec(norm_post_mix[li]), vec(norm_pre_ffn[li]),
                                      rw_pad, rb_pad, lc=lc, tm=tm)

        tok, dst, block_e = _route(topi[:, :TOP_K], blk)
        yk = _moe_experts(h2, tok, dst, block_e, expert_w1[li].astype(F32), expert_b1[li].astype(F32),
                          expert_w2[li].astype(F32), expert_b2[li].astype(F32), blk=blk)
        x = _combine_kmajor(yk, topg, x1, mod, vec(norm_post_ffn[li]), lc=lc, tm=tm)

    return (x[:t_prompt].reshape(bsz, lc, d), x[t_prompt:].reshape(dbsz, dl, d))
```

```python
import functools
import math

import jax
import jax.numpy as jnp
from jax import lax
from jax.experimental import pallas as pl
from jax.experimental.pallas import tpu as pltpu

F32 = jnp.float32
BF16 = jnp.bfloat16
I32 = jnp.int32

HEAD_DIM = 64
N_HEADS = 8
N_KV_HEADS = 2
GQA_GROUP = N_HEADS // N_KV_HEADS
ATTN_W = N_HEADS * HEAD_DIM
KV_W = N_KV_HEADS * HEAD_DIM
SSM_GROUP = 16
SSM_STATE = 64
CONV_KERNEL = 31
CONV_PAD = CONV_KERNEL // 2
GRID_W = 64
ROPE_AXIS_DIM = HEAD_DIM // 2
ROPE_THETA = 10000.0
N_EXPERTS = 32
TOP_K = 4
SWIGLU_LIMIT = 7.0
SWIGLU_ALPHA = 1.702
EPS = 1e-6

LANES = 128
HALO = 16
V_ROWS = HEAD_DIM + 16
S5_CHUNK = 32
MOE_BLOCK = 512
NEG_BIG = -1e30
VMEM_LIMIT = 48 * 1024 * 1024
EXPERT_VMEM_LIMIT = 56 * 1024 * 1024


def _cparams(sem, vmem_limit_bytes=VMEM_LIMIT, **kw):
    return pltpu.CompilerParams(dimension_semantics=sem, vmem_limit_bytes=vmem_limit_bytes, **kw)


def _rsqrt_mean_sq(x):
    return lax.rsqrt(jnp.mean(x * x, axis=-1, keepdims=True) + EPS)


def _mod_kernel(c_ref, w_ref, b_ref, o_ref):
    c = c_ref[...]
    s = c * jax.nn.sigmoid(c)
    o_ref[...] = jnp.dot(s, w_ref[...], preferred_element_type=F32,
                         precision=lax.Precision.HIGHEST) + b_ref[...]


def _modulation(c_all, w_ada, b_ada):
    s, d = c_all.shape
    n = w_ada.shape[1]
    tn = n // 4
    return pl.pallas_call(
        _mod_kernel,
        grid=(n // tn,),
        in_specs=[pl.BlockSpec((s, d), lambda j: (0, 0)),
                  pl.BlockSpec((d, tn), lambda j: (0, j)),
                  pl.BlockSpec((1, tn), lambda j: (0, j))],
        out_specs=pl.BlockSpec((s, tn), lambda j: (0, j)),
        out_shape=jax.ShapeDtypeStruct((s, n), F32),
        compiler_params=_cparams(("arbitrary",)),
        name="adaln_mod",
    )(c_all, w_ada, b_ada.reshape(1, n))


def _inproj_kernel(x_ref, mod_ref, npre_ref, w_ref, qkn_ref, cos_ref, sin_ref, bd_ref,
                   q_ref, k_ref, v_ref, u_ref, hc_ref):
    x = x_ref[...]
    sh = mod_ref[0, 0:1, :]
    sc = mod_ref[0, 1:2, :]
    h = x * _rsqrt_mean_sq(x) * npre_ref[...] * (1.0 + sc) + sh
    proj = jnp.dot(h.astype(BF16), w_ref[...], preferred_element_type=F32)
    tm = x.shape[0]
    cos = cos_ref[...]
    sin = sin_ref[...]
    lane = lax.broadcasted_iota(I32, (tm, LANES), 1)
    first_half = (lane % ROPE_AXIS_DIM) < (ROPE_AXIS_DIM // 2)
    n_qk = (ATTN_W + KV_W) // LANES
    for c in range(n_qk):
        xc = proj[:, c * LANES:(c + 1) * LANES]
        ms = jnp.dot((xc * xc).astype(BF16), bd_ref[...], preferred_element_type=F32) * (1.0 / HEAD_DIM)
        xn = xc * lax.rsqrt(ms + EPS) * qkn_ref[c:c + 1, :]
        half = ROPE_AXIS_DIM // 2
        swapped = jnp.where(first_half, pltpu.roll(xn, LANES - half, 1), pltpu.roll(xn, half, 1))
        rot = xn * cos + swapped * sin
        if c < ATTN_W // LANES:
            rot_t = rot.T.astype(BF16)
            q_ref[2 * c] = rot_t[:HEAD_DIM, :]
            q_ref[2 * c + 1] = rot_t[HEAD_DIM:, :]
        else:
            rot = rot.astype(BF16)
            k_ref[0] = rot[:, :HEAD_DIM]
            k_ref[1] = rot[:, HEAD_DIM:]
    o2 = ATTN_W + KV_W
    v_t = proj[:, o2:o2 + KV_W].T.astype(BF16)
    ones = jnp.ones((V_ROWS - HEAD_DIM, tm), BF16)
    for hh in range(N_KV_HEADS):
        v_ref[hh, 0, 0:HEAD_DIM, :] = v_t[hh * HEAD_DIM:(hh + 1) * HEAD_DIM, :]
        v_ref[hh, 0, HEAD_DIM:V_ROWS, :] = ones
    o3 = o2 + KV_W
    ssm_w = u_ref.shape[1]
    u_ref[...] = proj[:, o3:o3 + ssm_w].astype(BF16)
    o4 = o3 + ssm_w
    conv_w = hc_ref.shape[1]
    a = proj[:, o4:o4 + conv_w]
    gate = proj[:, o4 + conv_w:o4 + 2 * conv_w]
    hc_ref[...] = a * jax.nn.sigmoid(gate)


def _inproj(x, mod, norm_pre, w_in_bf, qkn, cos_t, sin_t, bd, *, n_prompt, lc, tm, ssm_w, conv_w):
    t, d = x.shape
    nb = lc // tm
    s = t // lc
    in_w = w_in_bf.shape[1]

    def row(si, i):
        return si * nb + i

    def pos(si, i):
        return jnp.where(si < n_prompt, i, (si - n_prompt) * nb + i)

    return pl.pallas_call(
        _inproj_kernel,
        grid=(s, nb),
        in_specs=[pl.BlockSpec((tm, d), lambda si, i: (row(si, i), 0)),
                  pl.BlockSpec((1, 6, d), lambda si, i: (si, 0, 0)),
                  pl.BlockSpec((1, d), lambda si, i: (0, 0)),
                  pl.BlockSpec((d, in_w), lambda si, i: (0, 0)),
                  pl.BlockSpec((8, LANES), lambda si, i: (0, 0)),
                  pl.BlockSpec((tm, LANES), lambda si, i: (pos(si, i), 0)),
                  pl.BlockSpec((tm, LANES), lambda si, i: (pos(si, i), 0)),
                  pl.BlockSpec((LANES, LANES), lambda si, i: (0, 0))],
        out_specs=[pl.BlockSpec((N_HEADS, HEAD_DIM, tm), lambda si, i: (0, 0, row(si, i))),
                   pl.BlockSpec((N_KV_HEADS, tm, HEAD_DIM), lambda si, i: (0, row(si, i), 0)),
                   pl.BlockSpec((N_KV_HEADS, 1, V_ROWS, tm), lambda si, i: (0, row(si, i), 0, 0)),
                   pl.BlockSpec((tm, ssm_w), lambda si, i: (row(si, i), 0)),
                   pl.BlockSpec((tm, conv_w), lambda si, i: (row(si, i), 0))],
        out_shape=[jax.ShapeDtypeStruct((N_HEADS, HEAD_DIM, t), BF16),
                   jax.ShapeDtypeStruct((N_KV_HEADS, t, HEAD_DIM), BF16),
                   jax.ShapeDtypeStruct((N_KV_HEADS, t // tm, V_ROWS, tm), BF16),
                   jax.ShapeDtypeStruct((t, ssm_w), BF16),
                   jax.ShapeDtypeStruct((t, conv_w), F32)],
        compiler_params=_cparams(("parallel", "arbitrary")),
        name="inproj",
    )(x, mod, norm_pre, w_in_bf, qkn, cos_t, sin_t, bd)


def _attn_kernel(q_ref, k_ref, v_ref, o_ref, s_ref, m_ref, acc_ref, *, tk):
    g, hd, tq = q_ref.shape
    q_t = jnp.concatenate([q_ref[i] for i in range(g)], axis=-1)
    n = k_ref.shape[1] // tk
    m_ref[...] = jnp.full(m_ref.shape, NEG_BIG, F32)
    acc_ref[...] = jnp.zeros(acc_ref.shape, F32)

    def scores(j):
        kc = k_ref[0, pl.ds(pl.multiple_of(j * tk, tk), tk), :]
        return jnp.dot(kc, q_t, preferred_element_type=F32)

    def step(j, slot, prefetch):
        if prefetch:
            s_ref[1 - slot] = scores(j + 1)
        s = s_ref[slot]
        m_prev = m_ref[...]
        m_next = jnp.maximum(m_prev, jnp.max(s, axis=0, keepdims=True))
        alpha = jnp.exp2(m_prev - m_next)
        p = jnp.exp2(s - m_next)
        acc_ref[...] = acc_ref[...] * alpha + jnp.dot(v_ref[0, j], p.astype(BF16), preferred_element_type=F32)
        m_ref[...] = m_next

    s_ref[0] = scores(0)
    n_pairs = (n - 1) // 2

    def body(jj, carry):
        step(2 * jj, 0, True)
        step(2 * jj + 1, 1, True)
        return carry

    lax.fori_loop(0, n_pairs, body, 0)
    if n - 2 * n_pairs == 2:
        step(2 * n_pairs, 0, True)
        step(2 * n_pairs + 1, 1, False)
    else:
        step(2 * n_pairs, 0, False)
    o_t = acc_ref[0:hd, :] * (1.0 / acc_ref[hd:hd + 1, :])
    o_ref[...] = jnp.concatenate([o_t[:, i * tq:(i + 1) * tq] for i in range(g)], axis=0).T.astype(o_ref.dtype)


def _attention(q_t, k, v_t, *, row0, n_seq, lkv, tq, tk):
    nq = lkv // tq
    qb0 = row0 // tq
    kb0 = row0 // lkv
    cols = GQA_GROUP * tq
    return pl.pallas_call(
        functools.partial(_attn_kernel, tk=tk),
        grid=(n_seq, N_KV_HEADS, nq),
        in_specs=[pl.BlockSpec((GQA_GROUP, HEAD_DIM, tq), lambda b, g, i: (g, 0, qb0 + b * nq + i)),
                  pl.BlockSpec((1, lkv, HEAD_DIM), lambda b, g, i: (g, kb0 + b, 0)),
                  pl.BlockSpec((1, lkv // tk, V_ROWS, tk), lambda b, g, i: (g, kb0 + b, 0, 0))],
        out_specs=pl.BlockSpec((tq, GQA_GROUP * HEAD_DIM), lambda b, g, i: (b * nq + i, g)),
        out_shape=jax.ShapeDtypeStruct((n_seq * lkv, ATTN_W), BF16),
        scratch_shapes=[pltpu.VMEM((2, tk, cols), F32), pltpu.VMEM((1, cols), F32),
                        pltpu.VMEM((V_ROWS, cols), F32)],
        compiler_params=_cparams(("parallel", "parallel", "arbitrary")),
        name="attention",
    )(q_t, k, v_t)


def _s5_weights(lam_re, lam_im, log_dt, b_re, b_im, c_re, c_im, d_skip, tc):
    hp = lax.Precision.HIGHEST
    lam = lax.complex(lam_re.astype(F32), lam_im.astype(F32))
    dt = jnp.exp(log_dt.astype(F32))[..., None]
    lam_dt = lam * dt
    lam_bar = jnp.exp(lam_dt)
    b_bar = ((lam_bar - 1.0) / lam)[..., None] * lax.complex(b_re.astype(F32), b_im.astype(F32))
    cmat = lax.complex(c_re.astype(F32), c_im.astype(F32))
    g, p = lam.shape[1], lam.shape[2]
    h = b_bar.shape[-1]
    taus = jnp.arange(tc + 1, dtype=F32)
    pw = jnp.exp(lam_dt[..., None] * taus)
    kern = jnp.einsum('dgop,dgpt,dgpi->dgtoi', cmat, pw, b_bar, precision=hp).real
    eye_h = jnp.eye(h, dtype=F32)
    kd = kern[0][:, 0] + kern[1][:, 0] + d_skip.astype(F32).reshape(g, h)[:, :, None] * eye_h
    by_lag = jnp.concatenate([jnp.flip(kern[1][:, 1:tc], axis=1), kd[:, None], kern[0][:, 1:tc]], axis=1)
    by_lag = by_lag.transpose(0, 2, 3, 1)
    padded = jnp.concatenate([by_lag, jnp.zeros((g, h, h, 1), F32)], axis=-1)
    skew = jnp.tile(padded, (1, 1, 1, tc))[..., :tc * (2 * tc - 1)].reshape(g, h, h, tc, 2 * tc - 1)
    full = skew[..., tc - 1:]
    toep = full.transpose(0, 3, 2, 4, 1).reshape(g, tc * h, tc * h)

    pf = jnp.flip(pw[0][:, :, :tc], axis=2)
    pb = pw[1][:, :, :tc]
    sf = pf[:, :, :, None] * b_bar[0][:, :, None, :]
    sb = pb[:, :, :, None] * b_bar[1][:, :, None, :]
    parts = [sf.real, sf.imag, sb.real, sb.imag]
    ws = jnp.stack([x.transpose(0, 2, 3, 1).reshape(g, tc * h, p) for x in parts], axis=2)
    odd = (jnp.arange(g) % 2)[:, None, None, None]
    zeros = jnp.zeros_like(ws)
    ws = jnp.where(odd == 0, jnp.concatenate([ws, zeros], -1), jnp.concatenate([zeros, ws], -1))

    ef = cmat[0].transpose(0, 2, 1)[:, :, None, :] * pw[0][:, :, 1:tc + 1][:, :, :, None]
    eb = cmat[1].transpose(0, 2, 1)[:, :, None, :] * jnp.flip(pw[1][:, :, 1:tc + 1], axis=2)[:, :, :, None]
    cparts = [ef.real, -ef.imag, eb.real, -eb.imag]
    wc = jnp.stack([x.reshape(g, p, tc * h) for x in cparts], axis=1)
    zc = jnp.zeros_like(wc)
    odd4 = (jnp.arange(g) % 2)[:, None, None, None]
    wc = jnp.where(odd4 == 0, jnp.concatenate([wc, zc], 2), jnp.concatenate([zc, wc], 2))
    wc = wc.reshape(g, 4 * 2 * p, tc * h)

    af = pw[0][:, :, tc].reshape(1, g * p)
    ab = pw[1][:, :, tc].reshape(1, g * p)
    decay = jnp.concatenate([af.real, af.imag, ab.real, ab.imag], axis=0)
    return toep.astype(BF16), ws.reshape(g, tc * h, 4 * 2 * p).astype(BF16), wc.astype(BF16), decay


def _s5_intra_kernel(u_ref, toep_ref, ws_ref, y_ref, s0_ref, s1_ref, s2_ref, s3_ref):
    u0 = u_ref[0]
    u1 = u_ref[1]
    y_ref[0] = jnp.dot(u0, toep_ref[0], preferred_element_type=F32)
    y_ref[1] = jnp.dot(u1, toep_ref[1], preferred_element_type=F32)
    st = (jnp.dot(u0, ws_ref[0], preferred_element_type=F32)
          + jnp.dot(u1, ws_ref[1], preferred_element_type=F32))
    for k, ref in enumerate((s0_ref, s1_ref, s2_ref, s3_ref)):
        ref[...] = st[:, k * LANES:(k + 1) * LANES]


def _s5_intra(ug, toep, ws, *, tr):
    g, nc, kw = ug.shape
    npair = g // 2
    st_shape = jax.ShapeDtypeStruct((nc, npair * LANES), F32)
    st_spec = pl.BlockSpec((tr, LANES), lambda pr, r: (r, pr))
    return pl.pallas_call(
        _s5_intra_kernel,
        grid=(npair, nc // tr),
        in_specs=[pl.BlockSpec((2, tr, kw), lambda pr, r: (pr, r, 0)),
                  pl.BlockSpec((2, kw, kw), lambda pr, r: (pr, 0, 0)),
                  pl.BlockSpec((2, kw, 4 * LANES), lambda pr, r: (pr, 0, 0))],
        out_specs=[pl.BlockSpec((2, tr, kw), lambda pr, r: (pr, r, 0)), st_spec, st_spec, st_spec, st_spec],
        out_shape=[jax.ShapeDtypeStruct((g, nc, kw), F32), st_shape, st_shape, st_shape, st_shape],
        compiler_params=_cparams(("parallel", "arbitrary")),
        name="s5_intra",
    )(ug, toep, ws)


def _s5_scan_kernel(decay_ref, cfr_ref, cfi_ref, cbr_ref, cbi_ref, sfr_ref, sfi_ref, sbr_ref, sbi_ref):
    n = cfr_ref.shape[0]
    w = cfr_ref.shape[1]
    afr = decay_ref[0:1, :]
    afi = decay_ref[1:2, :]
    abr = decay_ref[2:3, :]
    abi = decay_ref[3:4, :]
    zero = jnp.zeros((1, w), F32)

    def fwd(c, st):
        re, im = st
        sfr_ref[pl.ds(c, 1), :] = re
        sfi_ref[pl.ds(c, 1), :] = im
        cr = cfr_ref[pl.ds(c, 1), :]
        ci = cfi_ref[pl.ds(c, 1), :]
        return afr * re - afi * im + cr, afr * im + afi * re + ci

    lax.fori_loop(0, n, fwd, (zero, zero))

    def bwd(kk, st):
        c = n - 1 - kk
        re, im = st
        sbr_ref[pl.ds(c, 1), :] = re
        sbi_ref[pl.ds(c, 1), :] = im
        cr = cbr_ref[pl.ds(c, 1), :]
        ci = cbi_ref[pl.ds(c, 1), :]
        return abr * re - abi * im + cr, abr * im + abi * re + ci

    lax.fori_loop(0, n, bwd, (zero, zero))


def _s5_scan(decay, contribs, *, chunk0, n_seq, cps):
    w = contribs[0].shape[1]
    b0 = chunk0 // cps
    spec_in = pl.BlockSpec((cps, w), lambda b: (b0 + b, 0))
    spec_out = pl.BlockSpec((cps, w), lambda b: (b, 0))
    shape = jax.ShapeDtypeStruct((n_seq * cps, w), F32)
    return pl.pallas_call(
        _s5_scan_kernel,
        grid=(n_seq,),
        in_specs=[pl.BlockSpec((4, w), lambda b: (0, 0))] + [spec_in] * 4,
        out_specs=[spec_out] * 4,
        out_shape=[shape] * 4,
        compiler_params=_cparams(("parallel",)),
        name="s5_scan",
    )(decay, *contribs)


def _s5_out_kernel(y_ref, s0_ref, s1_ref, s2_ref, s3_ref, wc_ref, o_ref):
    st = jnp.concatenate([s0_ref[...], s1_ref[...], s2_ref[...], s3_ref[...]], axis=-1).astype(BF16)
    for k in range(2):
        y = y_ref[k] + jnp.dot(st, wc_ref[k], preferred_element_type=F32)
        o_ref[k] = 0.5 * y * (1.0 + lax.erf(y * (2.0 ** -0.5)))


def _s5_out(y_intra, states, wc, *, tr):
    g, nc, kw = y_intra.shape
    npair = g // 2
    st_spec = pl.BlockSpec((tr, LANES), lambda pr, r: (r, pr))
    return pl.pallas_call(
        _s5_out_kernel,
        grid=(npair, nc // tr),
        in_specs=[pl.BlockSpec((2, tr, kw), lambda pr, r: (pr, r, 0)), st_spec, st_spec, st_spec, st_spec,
                  pl.BlockSpec((2, 4 * LANES, kw), lambda pr, r: (pr, 0, 0))],
        out_specs=pl.BlockSpec((2, tr, kw), lambda pr, r: (pr, r, 0)),
        out_shape=jax.ShapeDtypeStruct((g, nc, kw), F32),
        compiler_params=_cparams(("parallel", "arbitrary")),
        name="s5_out",
    )(y_intra, *states, wc)


def _conv_kernel(prev_ref, cur_ref, next_ref, wdw_ref, bdw_ref, lng_ref, lnb_ref, wpw_ref, bn_ref, o_ref, scr,
                 *, n_prompt, n_seq, nb, rt):
    si = pl.program_id(0)
    i = pl.program_id(1)
    tm = cur_ref.shape[0]
    prev_ok = jnp.logical_or(i > 0, si > n_prompt)
    next_ok = jnp.logical_or(i < nb - 1, jnp.logical_and(si >= n_prompt, si < n_seq - 1))
    scr[0:HALO, :] = jnp.where(prev_ok, prev_ref[...], 0.0)
    scr[HALO:HALO + tm, :] = cur_ref[...]
    scr[HALO + tm:2 * HALO + tm, :] = jnp.where(next_ok, next_ref[...], 0.0)
    off = HALO - CONV_PAD
    sub = 8
    for r0 in range(0, tm, rt):
        slabs = [scr[r0 + sub * a:r0 + sub * a + rt + sub, :] for a in range((off + CONV_KERNEL - 1) // sub + 1)]
        acc = None
        for b in range(sub):
            part = None
            for a, slab in enumerate(slabs):
                kk = sub * a + b - off
                if 0 <= kk < CONV_KERNEL:
                    term = wdw_ref[kk:kk + 1, :] * slab
                    part = term if part is None else part + term
            part = part[b:b + rt, :]
            acc = part if acc is None else acc + part
        acc = acc + bdw_ref[...]
        mu = jnp.mean(acc, axis=-1, keepdims=True)
        cen = acc - mu
        var = jnp.mean(cen * cen, axis=-1, keepdims=True)
        y = cen * lax.rsqrt(var + EPS) * lng_ref[...] + lnb_ref[...]
        y = y * jax.nn.sigmoid(y)
        z = jnp.dot(y.astype(BF16), wpw_ref[...], preferred_element_type=F32)
        o_ref[r0:r0 + rt, :] = (z * _rsqrt_mean_sq(z) * bn_ref[...]).astype(o_ref.dtype)


def _conv_module(hc, w_dw, b_dw, ln_g, ln_b, w_pw_bf, bn_conv, *, n_prompt, lc, tm):
    t, cw = hc.shape
    nb = lc // tm
    s = t // lc
    hb = tm // HALO
    n_hblk = t // HALO
    kpad = 32

    def row(si, i):
        return si * nb + i

    wdw = jnp.zeros((kpad, cw), F32).at[:CONV_KERNEL].set(w_dw.astype(F32))
    vec = lambda a: a.astype(F32).reshape(1, cw)
    const = lambda shape: pl.BlockSpec(shape, lambda si, i: (0, 0))
    return pl.pallas_call(
        functools.partial(_conv_kernel, n_prompt=n_prompt, n_seq=s, nb=nb, rt=min(128, tm)),
        grid=(s, nb),
        in_specs=[pl.BlockSpec((HALO, cw), lambda si, i: (jnp.maximum(row(si, i) * hb - 1, 0), 0)),
                  pl.BlockSpec((tm, cw), lambda si, i: (row(si, i), 0)),
                  pl.BlockSpec((HALO, cw), lambda si, i: (jnp.minimum((row(si, i) + 1) * hb, n_hblk - 1), 0)),
                  const((kpad, cw)), const((1, cw)), const((1, cw)), const((1, cw)), const((cw, cw)),
                  const((1, cw))],
        out_specs=pl.BlockSpec((tm, cw), lambda si, i: (row(si, i), 0)),
        out_shape=jax.ShapeDtypeStruct((t, cw), BF16),
        scratch_shapes=[pltpu.VMEM((tm + 2 * HALO, cw), F32)],
        compiler_params=_cparams(("parallel", "arbitrary")),
        name="conv_module",
    )(hc, hc, hc, wdw, vec(b_dw), vec(ln_g), vec(ln_b), w_pw_bf, vec(bn_conv))


def _outproj_kernel(x_ref, attn_ref, ys_ref, cn_ref, mod_ref, bn_ref, wglu_ref, bglu_ref, wout_ref, npost_ref,
                    npre2_ref, rw_ref, rb_ref, x1_ref, h2_ref, ti_ref, tg_ref):
    aw = attn_ref.shape[1]
    sw = ys_ref.shape[1]
    a = attn_ref[...].astype(F32)
    an = a * _rsqrt_mean_sq(a) * bn_ref[:, 0:aw]
    y = ys_ref[...]
    z = jnp.dot(y.astype(BF16), wglu_ref[...], preferred_element_type=F32) + bglu_ref[...]
    sm = y * jax.nn.sigmoid(z)
    sn = sm * _rsqrt_mean_sq(sm) * bn_ref[:, aw:aw + sw]
    mix = (jnp.dot(an.astype(BF16), wout_ref[0:aw, :], preferred_element_type=F32)
           + jnp.dot(sn.astype(BF16), wout_ref[aw:aw + sw, :], preferred_element_type=F32)
           + jnp.dot(cn_ref[...], wout_ref[aw + sw:, :], preferred_element_type=F32))
    g1 = mod_ref[0, 2:3, :]
    sh2 = mod_ref[0, 3:4, :]
    sc2 = mod_ref[0, 4:5, :]
    x1 = x_ref[...] + g1 * (mix * _rsqrt_mean_sq(mix) * npost_ref[...])
    x1_ref[...] = x1
    h2 = x1 * _rsqrt_mean_sq(x1) * npre2_ref[...] * (1.0 + sc2) + sh2
    h2_ref[...] = h2
    h_hi = h2.astype(BF16)
    h_lo = (h2 - h_hi.astype(F32)).astype(BF16)
    both = jnp.dot(h_hi, rw_ref[...], preferred_element_type=F32)
    logits = (both[:, :LANES] + both[:, LANES:]
              + jnp.dot(h_lo, rw_ref[:, :LANES], preferred_element_type=F32) + rb_ref[...])
    tm = logits.shape[0]
    lane = lax.broadcasted_iota(I32, (tm, LANES), 1)
    vals = []
    ti = jnp.zeros((tm, LANES), I32)
    for kk in range(TOP_K):
        m = jnp.max(logits, axis=1, keepdims=True)
        ik = jnp.min(jnp.where(logits == m, lane, LANES), axis=1, keepdims=True)
        vals.append(m)
        ti = jnp.where(lane == kk, ik, ti)
        logits = jnp.where(lane == ik, NEG_BIG * 2.0, logits)
    es = [jnp.exp(v - vals[0]) for v in vals]
    inv = 1.0 / (es[0] + es[1] + es[2] + es[3])
    tg = jnp.zeros((tm, LANES), F32)
    for kk in range(TOP_K):
        tg = jnp.where(lane == kk, es[kk] * inv, tg)
    ti_ref[...] = ti.T[0:8, :]
    tg_ref[...] = tg


def _outproj(x, attn, ys, cn, mod, bn, wglu_bf, bglu, wout_bf, npost, npre2, rw_pad, rb_pad, *, lc, tm):
    t, d = x.shape
    nb = lc // tm
    s = t // lc
    aw, sw, cw = attn.shape[1], ys.shape[1], cn.shape[1]

    def row(si, i):
        return si * nb + i

    rows = lambda w: pl.BlockSpec((tm, w), lambda si, i: (row(si, i), 0))
    const = lambda shape: pl.BlockSpec(shape, lambda si, i: (0, 0))
    return pl.pallas_call(
        _outproj_kernel,
        grid=(s, nb),
        in_specs=[rows(d), rows(aw), rows(sw), rows(cw),
                  pl.BlockSpec((1, 6, d), lambda si, i: (si, 0, 0)),
                  const((1, d)), const((sw, sw)), const((1, sw)), const((d, d)), const((1, d)), const((1, d)),
                  const((d, 2 * LANES)), const((1, LANES))],
        out_specs=[rows(d), rows(d), pl.BlockSpec((8, tm), lambda si, i: (0, row(si, i))), rows(LANES)],
        out_shape=[jax.ShapeDtypeStruct((t, d), F32), jax.ShapeDtypeStruct((t, d), F32),
                   jax.ShapeDtypeStruct((8, t), I32), jax.ShapeDtypeStruct((t, LANES), F32)],
        compiler_params=_cparams(("parallel", "arbitrary")),
        name="outproj_router",
    )(x, attn, ys, cn, mod, bn, wglu_bf, bglu, wout_bf, npost, npre2, rw_pad, rb_pad)


def _route(topi, blk):
    t = topi.shape[1]
    n_assign = t * TOP_K
    flat_e = topi[:TOP_K].reshape(-1)
    eids = jnp.arange(N_EXPERTS, dtype=I32)
    onehot = (flat_e[:, None] == eids[None, :]).astype(I32)
    csum = jnp.cumsum(onehot, axis=0)
    counts = csum[-1]
    rank = jnp.sum((csum - onehot) * onehot, axis=1)
    padded = ((counts + blk - 1) // blk) * blk
    ends_p = jnp.cumsum(padded)
    starts_p = ends_p - padded
    dest = (jnp.sum(onehot * starts_p[None, :], axis=1) + rank).astype(I32)
    dest = jnp.pad(dest.reshape(TOP_K, t), ((0, 8 - TOP_K), (0, 0)))
    n_blocks = n_assign // blk + N_EXPERTS
    cap = n_blocks * blk
    n_used = (ends_p[-1] // blk).astype(I32)
    bidx = jnp.arange(n_blocks, dtype=I32)
    be_raw = jnp.sum((ends_p[None, :] <= (bidx * blk)[:, None]).astype(I32), axis=1)
    last_e = jnp.max(jnp.where(counts > 0, eids, 0))
    block_e = jnp.where(bidx < n_used, jnp.minimum(be_raw, N_EXPERTS - 1), last_e).astype(I32)
    fill_base = jnp.concatenate([starts_p + counts, ends_p[-1:]]).astype(I32)
    fill_n = jnp.concatenate([padded - counts, cap - ends_p[-1:]]).astype(I32)
    return dest, block_e, n_used.reshape(1), fill_base, fill_n


def _dispatch_kernel(fb_ref, fn_ref, dest_ref, h_ref, xs_hbm, zrow, sem, zsem):
    step = pl.program_id(0)
    tm = h_ref.shape[0]

    @pl.when(step == 0)
    def _():
        zrow[...] = jnp.zeros(zrow.shape, zrow.dtype)
        for e in range(N_EXPERTS + 1):
            base = fb_ref[e]
            n = fn_ref[e]

            def zissue(j, carry):
                pltpu.make_async_copy(zrow.at[pl.ds(0, 1)], xs_hbm.at[pl.ds(base + j, 1)], zsem).start()
                return carry

            def zwait(j, carry):
                pltpu.make_async_copy(zrow.at[pl.ds(0, 1)], xs_hbm.at[pl.ds(0, 1)], zsem).wait()
                return carry

            lax.fori_loop(0, n, zissue, 0)
            lax.fori_loop(0, n, zwait, 0)

    def issue(r, carry):
        for kk in range(TOP_K):
            pltpu.make_async_copy(h_ref.at[pl.ds(r, 1)],
                                  xs_hbm.at[pl.ds(dest_ref[kk, r], 1)], sem).start(priority=kk % 2)
        return carry

    lax.fori_loop(0, tm, issue, 0, unroll=4)
    for kk in range(TOP_K):
        pltpu.make_async_copy(h_ref, xs_hbm.at[pl.ds(0, tm)], sem).wait()


def _dispatch(h2, dest, fill_base, fill_n, *, cap, tm):
    t, d = h2.shape
    grid_spec = pltpu.PrefetchScalarGridSpec(
        num_scalar_prefetch=2,
        grid=(t // tm,),
        in_specs=[pl.BlockSpec((8, tm), lambda i, fb, fn: (0, i), memory_space=pltpu.SMEM),
                  pl.BlockSpec((tm, d), lambda i, fb, fn: (i, 0))],
        out_specs=pl.BlockSpec(memory_space=pl.ANY),
        scratch_shapes=[pltpu.VMEM((8, d), F32), pltpu.SemaphoreType.DMA(()), pltpu.SemaphoreType.DMA(())],
    )
    return pl.pallas_call(
        _dispatch_kernel,
        grid_spec=grid_spec,
        out_shape=jax.ShapeDtypeStruct((cap, d), F32),
        compiler_params=_cparams(("arbitrary",), disable_bounds_checks=True),
        name="moe_dispatch",
    )(fill_base, fill_n, dest, h2)


def _expert_kernel(be_ref, nu_ref, xs_ref, w1_ref, b1_ref, w2_ref, b2_ref, ys_ref, w1b, w2b):
    b = pl.program_id(0)
    de = w2_ref.shape[1]
    used = b < nu_ref[0]
    fresh = jnp.logical_or(b == 0, be_ref[b] != be_ref[jnp.maximum(b - 1, 0)])

    @pl.when(jnp.logical_and(used, fresh))
    def _():
        w1b[...] = w1_ref[0].astype(BF16)
        w2b[...] = w2_ref[0].astype(BF16)

    @pl.when(used)
    def _():
        a = jnp.dot(xs_ref[...].astype(BF16), w1b[...], preferred_element_type=F32) + b1_ref[0]
        gt = jnp.minimum(a[:, :de], SWIGLU_LIMIT)
        up = jnp.clip(a[:, de:], -SWIGLU_LIMIT, SWIGLU_LIMIT)
        act = (up + 1.0) * (gt * jax.nn.sigmoid(SWIGLU_ALPHA * gt))
        ys_ref[...] = jnp.dot(act.astype(BF16), w2b[...], preferred_element_type=F32) + b2_ref[0]

    @pl.when(jnp.logical_not(used))
    def _():
        ys_ref[...] = jnp.zeros(ys_ref.shape, ys_ref.dtype)


def _experts(xs, block_e, n_used, w1, b1, w2, b2, *, blk):
    cap, d = xs.shape
    ne, _, de2 = w1.shape
    de = w2.shape[1]
    nblk = cap // blk

    def xrow(b, be, nu):
        return (jnp.minimum(b, nu[0] - 1), 0)

    grid_spec = pltpu.PrefetchScalarGridSpec(
        num_scalar_prefetch=2,
        grid=(nblk,),
        in_specs=[pl.BlockSpec((blk, d), xrow),
                  pl.BlockSpec((1, d, de2), lambda b, be, nu: (be[b], 0, 0)),
                  pl.BlockSpec((1, 1, de2), lambda b, be, nu: (be[b], 0, 0)),
                  pl.BlockSpec((1, de, d), lambda b, be, nu: (be[b], 0, 0)),
                  pl.BlockSpec((1, 1, d), lambda b, be, nu: (be[b], 0, 0))],
        out_specs=pl.BlockSpec((blk, d), lambda b, be, nu: (b, 0)),
        scratch_shapes=[pltpu.VMEM((d, de2), BF16), pltpu.VMEM((de, d), BF16)],
    )
    return pl.pallas_call(
        _expert_kernel,
        grid_spec=grid_spec,
        out_shape=jax.ShapeDtypeStruct((cap, d), F32),
        compiler_params=_cparams(("arbitrary",), vmem_limit_bytes=EXPERT_VMEM_LIMIT),
        name="moe_experts",
    )(block_e, n_used, xs, w1, b1.reshape(ne, 1, de2), w2, b2.reshape(ne, 1, d))


def _combine_kernel(dest_ref, gate_ref, x1_ref, mod_ref, npost_ref, ys_hbm, o_ref, buf, sem):
    tm = x1_ref.shape[0]

    def issue(r, carry):
        for kk in range(TOP_K):
            pltpu.make_async_copy(ys_hbm.at[pl.ds(dest_ref[kk, r], 1)],
                                  buf.at[kk, pl.ds(r, 1)], sem).start()
        return carry

    lax.fori_loop(0, tm, issue, 0, unroll=4)
    for kk in range(TOP_K):
        pltpu.make_async_copy(ys_hbm.at[pl.ds(0, tm)], buf.at[kk], sem).wait()
    gate = gate_ref[...]
    ff = gate[:, 0:1] * buf[0]
    for kk in range(1, TOP_K):
        ff = ff + gate[:, kk:kk + 1] * buf[kk]
    g2 = mod_ref[0, 5:6, :]
    o_ref[...] = x1_ref[...] + g2 * (ff * _rsqrt_mean_sq(ff) * npost_ref[...])


def _combine(ys, dest, gate, x1, mod, npost, *, lc, tm, seq0=0, n_seq=None):
    t, d = x1.shape
    nb = lc // tm
    s = t // lc - seq0 if n_seq is None else n_seq

    def row(si, i):
        return (seq0 + si) * nb + i

    return pl.pallas_call(
        _combine_kernel,
        grid=(s, nb),
        in_specs=[pl.BlockSpec((8, tm), lambda si, i: (0, row(si, i)), memory_space=pltpu.SMEM),
                  pl.BlockSpec((tm, LANES), lambda si, i: (row(si, i), 0)),
                  pl.BlockSpec((tm, d), lambda si, i: (row(si, i), 0)),
                  pl.BlockSpec((1, 6, d), lambda si, i: (seq0 + si, 0, 0)),
                  pl.BlockSpec((1, d), lambda si, i: (0, 0)),
                  pl.BlockSpec(memory_space=pl.ANY)],
        out_specs=pl.BlockSpec((tm, d), lambda si, i: (si * nb + i, 0)),
        out_shape=jax.ShapeDtypeStruct((s * lc, d), F32),
        scratch_shapes=[pltpu.VMEM((TOP_K, tm, d), F32), pltpu.SemaphoreType.DMA(())],
        compiler_params=_cparams(("arbitrary", "arbitrary"), disable_bounds_checks=True),
        name="moe_combine",
    )(dest, gate, x1, mod, npost, ys)


def _rope_tables(n_pos):
    rows = n_pos // GRID_W
    row_idx = jnp.repeat(jnp.arange(rows, dtype=F32), GRID_W)
    col_idx = jnp.tile(jnp.arange(GRID_W, dtype=F32), rows)
    inv = 1.0 / (ROPE_THETA ** (jnp.arange(0, ROPE_AXIS_DIM, 2, dtype=F32) / ROPE_AXIS_DIM))
    ar = row_idx[:, None] * inv[None, :]
    ac = col_idx[:, None] * inv[None, :]
    cos = jnp.concatenate([jnp.cos(ar), jnp.cos(ar), jnp.cos(ac), jnp.cos(ac)], axis=-1)
    sin = jnp.concatenate([-jnp.sin(ar), jnp.sin(ar), -jnp.sin(ac), jnp.sin(ac)], axis=-1)
    reps = LANES // HEAD_DIM
    return jnp.tile(cos, (1, reps)), jnp.tile(sin, (1, reps))


def _tiles(lc):
    tm = min(512, lc)
    tq = min(256, lc)
    return tm, tq


def kernel(x_prompt, x_sample, c_prompt, c_sample, w_ada, b_ada, norm_pre_mix, norm_post_mix, norm_pre_ffn, norm_post_ffn, w_in, q_norm, k_norm, ssm_lam_re, ssm_lam_im, ssm_log_dt, ssm_b_re, ssm_b_im, ssm_c_re, ssm_c_im, ssm_d, ssm_w_glu, ssm_b_glu, conv_w_dw, conv_b_dw, conv_ln_g, conv_ln_b, conv_w_pw, branch_norm, w_out, router_w, router_b, expert_w1, expert_b1, expert_w2, expert_b2):
    bsz, lc, d = x_prompt.shape
    dbsz, dl, _ = x_sample.shape
    assert dbsz == 1 and dl % lc == 0 and lc % S5_CHUNK == 0 and lc % GRID_W == 0
    n_sample = dl // lc
    n_seq = bsz + n_sample
    t_prompt = bsz * lc
    t = t_prompt + dl
    depth = w_ada.shape[0]
    ssm_w = ssm_d.shape[1]
    conv_w = conv_b_dw.shape[1]
    n_groups = ssm_w // SSM_GROUP
    tm, tq = _tiles(lc)
    tc = S5_CHUNK
    nc = t // tc
    cps = lc // tc
    kw = tc * SSM_GROUP
    assert t_prompt % dl == 0
    blk = min(MOE_BLOCK, t * TOP_K // N_EXPERTS)

    x = jnp.concatenate([x_prompt.reshape(t_prompt, d), x_sample.reshape(dl, d)], axis=0)
    c_all = jnp.concatenate([c_prompt, jnp.repeat(c_sample, n_sample, axis=0)], axis=0)
    cos_t, sin_t = _rope_tables(max(lc, dl))
    bd = (jnp.arange(LANES)[:, None] // HEAD_DIM == jnp.arange(LANES)[None, :] // HEAD_DIM).astype(BF16)
    reps = LANES // HEAD_DIM
    vec = lambda a: a.astype(F32).reshape(1, -1)
    s5_w = jax.vmap(lambda *p: _s5_weights(*p, tc))(ssm_lam_re, ssm_lam_im, ssm_log_dt, ssm_b_re, ssm_b_im,
                                                    ssm_c_re, ssm_c_im, ssm_d)

    for li in range(depth):
        mod = _modulation(c_all, w_ada[li], b_ada[li]).reshape(n_seq, 6, d)
        qkn = jnp.concatenate([jnp.tile(vec(q_norm[li]) * (HEAD_DIM ** -0.5 * math.log2(math.e)),
                                        (ATTN_W // LANES, reps)),
                               jnp.tile(vec(k_norm[li]), (1, reps)),
                               jnp.zeros((8 - ATTN_W // LANES - 1, LANES), F32)], axis=0)
        q_t, k, v_t, u, hc = _inproj(x, mod, vec(norm_pre_mix[li]), w_in[li].astype(BF16), qkn, cos_t, sin_t, bd,
                                     n_prompt=bsz, lc=lc, tm=tm, ssm_w=ssm_w, conv_w=conv_w)

        attn = jnp.concatenate([
            _attention(q_t, k, v_t, row0=0, n_seq=bsz, lkv=lc, tq=min(2 * tq, lc), tk=tm),
            _attention(q_t, k, v_t, row0=t_prompt, n_seq=1, lkv=dl, tq=tq, tk=tm)], axis=0)

        toep, ws, wc, decay = [w[li] for w in s5_w]
        ug = u.reshape(nc, tc, n_groups, SSM_GROUP).transpose(2, 0, 1, 3).reshape(n_groups, nc, kw)
        tr = math.gcd(nc, 384)
        y_intra, *contribs = _s5_intra(ug, toep, ws, tr=tr)
        st_p = _s5_scan(decay, contribs, chunk0=0, n_seq=bsz, cps=cps)
        st_s = _s5_scan(decay, contribs, chunk0=t_prompt // tc, n_seq=1, cps=dl // tc)
        states = [jnp.concatenate([a, b], axis=0) for a, b in zip(st_p, st_s)]
        yg = _s5_out(y_intra, states, wc, tr=tr)
        ys = yg.reshape(n_groups, nc, tc, SSM_GROUP).transpose(1, 2, 0, 3).reshape(t, ssm_w)

        bn = vec(branch_norm[li])
        cn = _conv_module(hc, conv_w_dw[li], conv_b_dw[li], conv_ln_g[li], conv_ln_b[li],
                          conv_w_pw[li].astype(BF16), bn[0, ATTN_W + ssm_w:], n_prompt=bsz, lc=lc, tm=tm)

        rw_pad = jnp.zeros((d, LANES), F32).at[:, :N_EXPERTS].set(router_w[li].astype(F32))
        rb_pad = jnp.full((1, LANES), NEG_BIG, F32).at[0, :N_EXPERTS].set(router_b[li].astype(F32))
        rw_hi = rw_pad.astype(BF16)
        rw_pad = jnp.concatenate([rw_hi, (rw_pad - rw_hi.astype(F32)).astype(BF16)], axis=1)
        x1, h2, topi, topg = _outproj(x, attn, ys, cn, mod, bn, ssm_w_glu[li].astype(BF16), vec(ssm_b_glu[li]),
                                      w_out[li].astype(BF16), vec(norm_post_mix[li]), vec(norm_pre_ffn[li]),
                                      rw_pad, rb_pad, lc=lc, tm=tm)

        dest, block_e, n_used, fill_base, fill_n = _route(topi, blk)
        xs = _dispatch(h2, dest, fill_base, fill_n, cap=block_e.shape[0] * blk, tm=tm)
        ye = _experts(xs, block_e, n_used, expert_w1[li].astype(F32), expert_b1[li].astype(F32),
                      expert_w2[li].astype(F32), expert_b2[li].astype(F32), blk=blk)
        if li + 1 < depth:
            x = _combine(ye, dest, topg, x1, mod, vec(norm_post_ffn[li]), lc=lc, tm=tm)
        else:
            out_p = _combine(ye, dest, topg, x1, mod, vec(norm_post_ffn[li]), lc=lc, tm=tm, seq0=0, n_seq=bsz)
            out_s = _combine(ye, dest, topg, x1, mod, vec(norm_post_ffn[li]), lc=lc, tm=tm, seq0=bsz,
                             n_seq=n_sample)

    return (out_p.reshape(bsz, lc, d), out_s.reshape(dbsz, dl, d))
```

```python
import functools
import math

import jax
import jax.numpy as jnp
from jax import lax
from jax.experimental import pallas as pl
from jax.experimental.pallas import tpu as pltpu

F32 = jnp.float32
BF16 = jnp.bfloat16
I32 = jnp.int32

HEAD_DIM = 64
N_HEADS = 8
N_KV_HEADS = 2
GQA_GROUP = N_HEADS // N_KV_HEADS
ATTN_W = N_HEADS * HEAD_DIM
KV_W = N_KV_HEADS * HEAD_DIM
SSM_GROUP = 16
SSM_STATE = 64
CONV_KERNEL = 31
CONV_PAD = CONV_KERNEL // 2
GRID_W = 64
ROPE_AXIS_DIM = HEAD_DIM // 2
ROPE_THETA = 10000.0
N_EXPERTS = 32
TOP_K = 4
SWIGLU_LIMIT = 7.0
SWIGLU_ALPHA = 1.702
EPS = 1e-6

LANES = 128
HALO = 16
V_ROWS = HEAD_DIM + 16
S5_CHUNK = 32
MOE_BLOCK = 512
NEG_BIG = -1e30
VMEM_LIMIT = 48 * 1024 * 1024
EXPERT_VMEM_LIMIT = 56 * 1024 * 1024


def _cparams(sem, vmem_limit_bytes=VMEM_LIMIT, **kw):
    return pltpu.CompilerParams(dimension_semantics=sem, vmem_limit_bytes=vmem_limit_bytes, **kw)


def _rsqrt_mean_sq(x):
    return lax.rsqrt(jnp.mean(x * x, axis=-1, keepdims=True) + EPS)


def _mod_kernel(c_ref, w_ref, b_ref, o_ref):
    c = c_ref[...]
    s = c * jax.nn.sigmoid(c)
    o_ref[...] = jnp.dot(s, w_ref[...], preferred_element_type=F32,
                         precision=lax.Precision.HIGHEST) + b_ref[...]


def _modulation(c_all, w_ada, b_ada):
    s, d = c_all.shape
    n = w_ada.shape[1]
    tn = n // 4
    return pl.pallas_call(
        _mod_kernel,
        grid=(n // tn,),
        in_specs=[pl.BlockSpec((s, d), lambda j: (0, 0)),
                  pl.BlockSpec((d, tn), lambda j: (0, j)),
                  pl.BlockSpec((1, tn), lambda j: (0, j))],
        out_specs=pl.BlockSpec((s, tn), lambda j: (0, j)),
        out_shape=jax.ShapeDtypeStruct((s, n), F32),
        compiler_params=_cparams(("arbitrary",)),
        name="adaln_mod",
    )(c_all, w_ada, b_ada.reshape(1, n))


def _inproj_kernel(x_ref, mod_ref, npre_ref, w_ref, qkn_ref, cos_ref, sin_ref, bd_ref,
                   q_ref, k_ref, v_ref, u_ref, hc_ref):
    x = x_ref[...]
    sh = mod_ref[0, 0:1, :]
    sc = mod_ref[0, 1:2, :]
    h = x * _rsqrt_mean_sq(x) * npre_ref[...] * (1.0 + sc) + sh
    proj = jnp.dot(h.astype(BF16), w_ref[...], preferred_element_type=F32)
    tm = x.shape[0]
    cos = cos_ref[...]
    sin = sin_ref[...]
    lane = lax.broadcasted_iota(I32, (tm, LANES), 1)
    first_half = (lane % ROPE_AXIS_DIM) < (ROPE_AXIS_DIM // 2)
    n_qk = (ATTN_W + KV_W) // LANES
    for c in range(n_qk):
        xc = proj[:, c * LANES:(c + 1) * LANES]
        ms = jnp.dot((xc * xc).astype(BF16), bd_ref[...], preferred_element_type=F32) * (1.0 / HEAD_DIM)
        xn = xc * lax.rsqrt(ms + EPS) * qkn_ref[c:c + 1, :]
        half = ROPE_AXIS_DIM // 2
        swapped = jnp.where(first_half, pltpu.roll(xn, LANES - half, 1), pltpu.roll(xn, half, 1))
        rot = xn * cos + swapped * sin
        if c < ATTN_W // LANES:
            rot_t = rot.T.astype(BF16)
            q_ref[2 * c] = rot_t[:HEAD_DIM, :]
            q_ref[2 * c + 1] = rot_t[HEAD_DIM:, :]
        else:
            rot = rot.astype(BF16)
            k_ref[0] = rot[:, :HEAD_DIM]
            k_ref[1] = rot[:, HEAD_DIM:]
    o2 = ATTN_W + KV_W
    v_t = proj[:, o2:o2 + KV_W].T.astype(BF16)
    ones = jnp.ones((V_ROWS - HEAD_DIM, tm), BF16)
    for hh in range(N_KV_HEADS):
        v_ref[hh, 0, 0:HEAD_DIM, :] = v_t[hh * HEAD_DIM:(hh + 1) * HEAD_DIM, :]
        v_ref[hh, 0, HEAD_DIM:V_ROWS, :] = ones
    o3 = o2 + KV_W
    ssm_w = u_ref.shape[1]
    u_ref[...] = proj[:, o3:o3 + ssm_w].astype(BF16)
    o4 = o3 + ssm_w
    conv_w = hc_ref.shape[1]
    a = proj[:, o4:o4 + conv_w]
    gate = proj[:, o4 + conv_w:o4 + 2 * conv_w]
    hc_ref[...] = a * jax.nn.sigmoid(gate)


def _inproj(x, mod, norm_pre, w_in_bf, qkn, cos_t, sin_t, bd, *, n_prompt, lc, tm, ssm_w, conv_w):
    t, d = x.shape
    nb = lc // tm
    s = t // lc
    in_w = w_in_bf.shape[1]

    def row(si, i):
        return si * nb + i

    def pos(si, i):
        return jnp.where(si < n_prompt, i, (si - n_prompt) * nb + i)

    return pl.pallas_call(
        _inproj_kernel,
        grid=(s, nb),
        in_specs=[pl.BlockSpec((tm, d), lambda si, i: (row(si, i), 0)),
                  pl.BlockSpec((1, 6, d), lambda si, i: (si, 0, 0)),
                  pl.BlockSpec((1, d), lambda si, i: (0, 0)),
                  pl.BlockSpec((d, in_w), lambda si, i: (0, 0)),
                  pl.BlockSpec((8, LANES), lambda si, i: (0, 0)),
                  pl.BlockSpec((tm, LANES), lambda si, i: (pos(si, i), 0)),
                  pl.BlockSpec((tm, LANES), lambda si, i: (pos(si, i), 0)),
                  pl.BlockSpec((LANES, LANES), lambda si, i: (0, 0))],
        out_specs=[pl.BlockSpec((N_HEADS, HEAD_DIM, tm), lambda si, i: (0, 0, row(si, i))),
                   pl.BlockSpec((N_KV_HEADS, tm, HEAD_DIM), lambda si, i: (0, row(si, i), 0)),
                   pl.BlockSpec((N_KV_HEADS, 1, V_ROWS, tm), lambda si, i: (0, row(si, i), 0, 0)),
                   pl.BlockSpec((tm, ssm_w), lambda si, i: (row(si, i), 0)),
                   pl.BlockSpec((tm, conv_w), lambda si, i: (row(si, i), 0))],
        out_shape=[jax.ShapeDtypeStruct((N_HEADS, HEAD_DIM, t), BF16),
                   jax.ShapeDtypeStruct((N_KV_HEADS, t, HEAD_DIM), BF16),
                   jax.ShapeDtypeStruct((N_KV_HEADS, t // tm, V_ROWS, tm), BF16),
                   jax.ShapeDtypeStruct((t, ssm_w), BF16),
                   jax.ShapeDtypeStruct((t, conv_w), F32)],
        compiler_params=_cparams(("parallel", "arbitrary")),
        name="inproj",
    )(x, mod, norm_pre, w_in_bf, qkn, cos_t, sin_t, bd)


def _attn_kernel(q_ref, k_ref, v_ref, o_ref, s_ref, m_ref, acc_ref, *, tk):
    g, hd, tq = q_ref.shape
    q_t = jnp.concatenate([q_ref[i] for i in range(g)], axis=-1)
    n = k_ref.shape[1] // tk
    m_ref[...] = jnp.full(m_ref.shape, NEG_BIG, F32)
    acc_ref[...] = jnp.zeros(acc_ref.shape, F32)

    def scores(j):
        kc = k_ref[0, pl.ds(pl.multiple_of(j * tk, tk), tk), :]
        return jnp.dot(kc, q_t, preferred_element_type=F32)

    def step(j, slot, prefetch):
        if prefetch:
            s_ref[1 - slot] = scores(j + 1)
        s = s_ref[slot]
        m_prev = m_ref[...]
        m_next = jnp.maximum(m_prev, jnp.max(s, axis=0, keepdims=True))
        alpha = jnp.exp2(m_prev - m_next)
        p = jnp.exp2(s - m_next)
        acc_ref[...] = acc_ref[...] * alpha + jnp.dot(v_ref[0, j], p.astype(BF16), preferred_element_type=F32)
        m_ref[...] = m_next

    s_ref[0] = scores(0)
    n_pairs = (n - 1) // 2

    def body(jj, carry):
        step(2 * jj, 0, True)
        step(2 * jj + 1, 1, True)
        return carry

    lax.fori_loop(0, n_pairs, body, 0)
    if n - 2 * n_pairs == 2:
        step(2 * n_pairs, 0, True)
        step(2 * n_pairs + 1, 1, False)
    else:
        step(2 * n_pairs, 0, False)
    o_t = acc_ref[0:hd, :] * (1.0 / acc_ref[hd:hd + 1, :])
    o_ref[...] = jnp.concatenate([o_t[:, i * tq:(i + 1) * tq] for i in range(g)], axis=0).T.astype(o_ref.dtype)


def _attention(q_t, k, v_t, *, row0, n_seq, lkv, tq, tk):
    nq = lkv // tq
    qb0 = row0 // tq
    kb0 = row0 // lkv
    cols = GQA_GROUP * tq
    return pl.pallas_call(
        functools.partial(_attn_kernel, tk=tk),
        grid=(n_seq, N_KV_HEADS, nq),
        in_specs=[pl.BlockSpec((GQA_GROUP, HEAD_DIM, tq), lambda b, g, i: (g, 0, qb0 + b * nq + i)),
                  pl.BlockSpec((1, lkv, HEAD_DIM), lambda b, g, i: (g, kb0 + b, 0)),
                  pl.BlockSpec((1, lkv // tk, V_ROWS, tk), lambda b, g, i: (g, kb0 + b, 0, 0))],
        out_specs=pl.BlockSpec((tq, GQA_GROUP * HEAD_DIM), lambda b, g, i: (b * nq + i, g)),
        out_shape=jax.ShapeDtypeStruct((n_seq * lkv, ATTN_W), BF16),
        scratch_shapes=[pltpu.VMEM((2, tk, cols), F32), pltpu.VMEM((1, cols), F32),
                        pltpu.VMEM((V_ROWS, cols), F32)],
        compiler_params=_cparams(("parallel", "parallel", "arbitrary")),
        name="attention",
    )(q_t, k, v_t)


def _s5_weights(lam_re, lam_im, log_dt, b_re, b_im, c_re, c_im, d_skip, tc):
    hp = lax.Precision.HIGHEST
    lam = lax.complex(lam_re.astype(F32), lam_im.astype(F32))
    dt = jnp.exp(log_dt.astype(F32))[..., None]
    lam_dt = lam * dt
    lam_bar = jnp.exp(lam_dt)
    b_bar = ((lam_bar - 1.0) / lam)[..., None] * lax.complex(b_re.astype(F32), b_im.astype(F32))
    cmat = lax.complex(c_re.astype(F32), c_im.astype(F32))
    g, p = lam.shape[1], lam.shape[2]
    h = b_bar.shape[-1]
    taus = jnp.arange(tc + 1, dtype=F32)
    pw = jnp.exp(lam_dt[..., None] * taus)
    kern = jnp.einsum('dgop,dgpt,dgpi->dgtoi', cmat, pw, b_bar, precision=hp).real
    eye_h = jnp.eye(h, dtype=F32)
    kd = kern[0][:, 0] + kern[1][:, 0] + d_skip.astype(F32).reshape(g, h)[:, :, None] * eye_h
    by_lag = jnp.concatenate([jnp.flip(kern[1][:, 1:tc], axis=1), kd[:, None], kern[0][:, 1:tc]], axis=1)
    by_lag = by_lag.transpose(0, 2, 3, 1)
    padded = jnp.concatenate([by_lag, jnp.zeros((g, h, h, 1), F32)], axis=-1)
    skew = jnp.tile(padded, (1, 1, 1, tc))[..., :tc * (2 * tc - 1)].reshape(g, h, h, tc, 2 * tc - 1)
    full = skew[..., tc - 1:]
    toep = full.transpose(0, 3, 2, 4, 1).reshape(g, tc * h, tc * h)

    pf = jnp.flip(pw[0][:, :, :tc], axis=2)
    pb = pw[1][:, :, :tc]
    sf = pf[:, :, :, None] * b_bar[0][:, :, None, :]
    sb = pb[:, :, :, None] * b_bar[1][:, :, None, :]
    parts = [sf.real, sf.imag, sb.real, sb.imag]
    ws = jnp.stack([x.transpose(0, 2, 3, 1).reshape(g, tc * h, p) for x in parts], axis=2)
    odd = (jnp.arange(g) % 2)[:, None, None, None]
    zeros = jnp.zeros_like(ws)
    ws = jnp.where(odd == 0, jnp.concatenate([ws, zeros], -1), jnp.concatenate([zeros, ws], -1))

    ef = cmat[0].transpose(0, 2, 1)[:, :, None, :] * pw[0][:, :, 1:tc + 1][:, :, :, None]
    eb = cmat[1].transpose(0, 2, 1)[:, :, None, :] * jnp.flip(pw[1][:, :, 1:tc + 1], axis=2)[:, :, :, None]
    cparts = [ef.real, -ef.imag, eb.real, -eb.imag]
    wc = jnp.stack([x.reshape(g, p, tc * h) for x in cparts], axis=1)
    zc = jnp.zeros_like(wc)
    odd4 = (jnp.arange(g) % 2)[:, None, None, None]
    wc = jnp.where(odd4 == 0, jnp.concatenate([wc, zc], 2), jnp.concatenate([zc, wc], 2))
    wc = wc.reshape(g, 4 * 2 * p, tc * h)

    af = pw[0][:, :, tc].reshape(1, g * p)
    ab = pw[1][:, :, tc].reshape(1, g * p)
    decay = jnp.concatenate([af.real, af.imag, ab.real, ab.imag], axis=0)
    return toep.astype(BF16), ws.reshape(g, tc * h, 4 * 2 * p).astype(BF16), wc.astype(BF16), decay


def _s5_intra_kernel(u_ref, toep_ref, ws_ref, y_ref, s0_ref, s1_ref, s2_ref, s3_ref):
    u0 = u_ref[0]
    u1 = u_ref[1]
    y_ref[0] = jnp.dot(u0, toep_ref[0], preferred_element_type=F32)
    y_ref[1] = jnp.dot(u1, toep_ref[1], preferred_element_type=F32)
    st = (jnp.dot(u0, ws_ref[0], preferred_element_type=F32)
          + jnp.dot(u1, ws_ref[1], preferred_element_type=F32))
    for k, ref in enumerate((s0_ref, s1_ref, s2_ref, s3_ref)):
        ref[...] = st[:, k * LANES:(k + 1) * LANES]


def _s5_intra(ug, toep, ws, *, tr):
    g, nc, kw = ug.shape
    npair = g // 2
    st_shape = jax.ShapeDtypeStruct((nc, npair * LANES), F32)
    st_spec = pl.BlockSpec((tr, LANES), lambda pr, r: (r, pr))
    return pl.pallas_call(
        _s5_intra_kernel,
        grid=(npair, nc // tr),
        in_specs=[pl.BlockSpec((2, tr, kw), lambda pr, r: (pr, r, 0)),
                  pl.BlockSpec((2, kw, kw), lambda pr, r: (pr, 0, 0)),
                  pl.BlockSpec((2, kw, 4 * LANES), lambda pr, r: (pr, 0, 0))],
        out_specs=[pl.BlockSpec((2, tr, kw), lambda pr, r: (pr, r, 0)), st_spec, st_spec, st_spec, st_spec],
        out_shape=[jax.ShapeDtypeStruct((g, nc, kw), F32), st_shape, st_shape, st_shape, st_shape],
        compiler_params=_cparams(("parallel", "arbitrary")),
        name="s5_intra",
    )(ug, toep, ws)


def _s5_scan_kernel(decay_ref, cfr_ref, cfi_ref, cbr_ref, cbi_ref, sfr_ref, sfi_ref, sbr_ref, sbi_ref):
    n = cfr_ref.shape[0]
    w = cfr_ref.shape[1]
    afr = decay_ref[0:1, :]
    afi = decay_ref[1:2, :]
    abr = decay_ref[2:3, :]
    abi = decay_ref[3:4, :]
    zero = jnp.zeros((1, w), F32)

    def fwd(c, st):
        re, im = st
        sfr_ref[pl.ds(c, 1), :] = re
        sfi_ref[pl.ds(c, 1), :] = im
        cr = cfr_ref[pl.ds(c, 1), :]
        ci = cfi_ref[pl.ds(c, 1), :]
        return afr * re - afi * im + cr, afr * im + afi * re + ci

    lax.fori_loop(0, n, fwd, (zero, zero))

    def bwd(kk, st):
        c = n - 1 - kk
        re, im = st
        sbr_ref[pl.ds(c, 1), :] = re
        sbi_ref[pl.ds(c, 1), :] = im
        cr = cbr_ref[pl.ds(c, 1), :]
        ci = cbi_ref[pl.ds(c, 1), :]
        return abr * re - abi * im + cr, abr * im + abi * re + ci

    lax.fori_loop(0, n, bwd, (zero, zero))


def _s5_scan(decay, contribs, *, chunk0, n_seq, cps):
    w = contribs[0].shape[1]
    b0 = chunk0 // cps
    spec_in = pl.BlockSpec((cps, w), lambda b: (b0 + b, 0))
    spec_out = pl.BlockSpec((cps, w), lambda b: (b, 0))
    shape = jax.ShapeDtypeStruct((n_seq * cps, w), F32)
    return pl.pallas_call(
        _s5_scan_kernel,
        grid=(n_seq,),
        in_specs=[pl.BlockSpec((4, w), lambda b: (0, 0))] + [spec_in] * 4,
        out_specs=[spec_out] * 4,
        out_shape=[shape] * 4,
        compiler_params=_cparams(("parallel",)),
        name="s5_scan",
    )(decay, *contribs)


def _s5_out_kernel(y_ref, s0_ref, s1_ref, s2_ref, s3_ref, wc_ref, o_ref):
    st = jnp.concatenate([s0_ref[...], s1_ref[...], s2_ref[...], s3_ref[...]], axis=-1).astype(BF16)
    for k in range(2):
        y = y_ref[k] + jnp.dot(st, wc_ref[k], preferred_element_type=F32)
        o_ref[k] = 0.5 * y * (1.0 + lax.erf(y * (2.0 ** -0.5)))


def _s5_out(y_intra, states, wc, *, tr):
    g, nc, kw = y_intra.shape
    npair = g // 2
    st_spec = pl.BlockSpec((tr, LANES), lambda pr, r: (r, pr))
    return pl.pallas_call(
        _s5_out_kernel,
        grid=(npair, nc // tr),
        in_specs=[pl.BlockSpec((2, tr, kw), lambda pr, r: (pr, r, 0)), st_spec, st_spec, st_spec, st_spec,
                  pl.BlockSpec((2, 4 * LANES, kw), lambda pr, r: (pr, 0, 0))],
        out_specs=pl.BlockSpec((2, tr, kw), lambda pr, r: (pr, r, 0)),
        out_shape=jax.ShapeDtypeStruct((g, nc, kw), F32),
        compiler_params=_cparams(("parallel", "arbitrary")),
        name="s5_out",
    )(y_intra, *states, wc)


def _conv_kernel(prev_ref, cur_ref, next_ref, wdw_ref, bdw_ref, lng_ref, lnb_ref, wpw_ref, bn_ref, o_ref, scr,
                 *, n_prompt, n_seq, nb, rt):
    si = pl.program_id(0)
    i = pl.program_id(1)
    tm = cur_ref.shape[0]
    prev_ok = jnp.logical_or(i > 0, si > n_prompt)
    next_ok = jnp.logical_or(i < nb - 1, jnp.logical_and(si >= n_prompt, si < n_seq - 1))
    scr[0:HALO, :] = jnp.where(prev_ok, prev_ref[...], 0.0)
    scr[HALO:HALO + tm, :] = cur_ref[...]
    scr[HALO + tm:2 * HALO + tm, :] = jnp.where(next_ok, next_ref[...], 0.0)
    off = HALO - CONV_PAD
    sub = 8
    for r0 in range(0, tm, rt):
        slabs = [scr[r0 + sub * a:r0 + sub * a + rt + sub, :] for a in range((off + CONV_KERNEL - 1) // sub + 1)]
        acc = None
        for b in range(sub):
            part = None
            for a, slab in enumerate(slabs):
                kk = sub * a + b - off
                if 0 <= kk < CONV_KERNEL:
                    term = wdw_ref[kk:kk + 1, :] * slab
                    part = term if part is None else part + term
            part = part[b:b + rt, :]
            acc = part if acc is None else acc + part
        acc = acc + bdw_ref[...]
        mu = jnp.mean(acc, axis=-1, keepdims=True)
        cen = acc - mu
        var = jnp.mean(cen * cen, axis=-1, keepdims=True)
        y = cen * lax.rsqrt(var + EPS) * lng_ref[...] + lnb_ref[...]
        y = y * jax.nn.sigmoid(y)
        z = jnp.dot(y.astype(BF16), wpw_ref[...], preferred_element_type=F32)
        o_ref[r0:r0 + rt, :] = (z * _rsqrt_mean_sq(z) * bn_ref[...]).astype(o_ref.dtype)


def _conv_module(hc, w_dw, b_dw, ln_g, ln_b, w_pw_bf, bn_conv, *, n_prompt, lc, tm):
    t, cw = hc.shape
    nb = lc // tm
    s = t // lc
    hb = tm // HALO
    n_hblk = t // HALO
    kpad = 32

    def row(si, i):
        return si * nb + i

    wdw = jnp.zeros((kpad, cw), F32).at[:CONV_KERNEL].set(w_dw.astype(F32))
    vec = lambda a: a.astype(F32).reshape(1, cw)
    const = lambda shape: pl.BlockSpec(shape, lambda si, i: (0, 0))
    return pl.pallas_call(
        functools.partial(_conv_kernel, n_prompt=n_prompt, n_seq=s, nb=nb, rt=min(128, tm)),
        grid=(s, nb),
        in_specs=[pl.BlockSpec((HALO, cw), lambda si, i: (jnp.maximum(row(si, i) * hb - 1, 0), 0)),
                  pl.BlockSpec((tm, cw), lambda si, i: (row(si, i), 0)),
                  pl.BlockSpec((HALO, cw), lambda si, i: (jnp.minimum((row(si, i) + 1) * hb, n_hblk - 1), 0)),
                  const((kpad, cw)), const((1, cw)), const((1, cw)), const((1, cw)), const((cw, cw)),
                  const((1, cw))],
        out_specs=pl.BlockSpec((tm, cw), lambda si, i: (row(si, i), 0)),
        out_shape=jax.ShapeDtypeStruct((t, cw), BF16),
        scratch_shapes=[pltpu.VMEM((tm + 2 * HALO, cw), F32)],
        compiler_params=_cparams(("parallel", "arbitrary")),
        name="conv_module",
    )(hc, hc, hc, wdw, vec(b_dw), vec(ln_g), vec(ln_b), w_pw_bf, vec(bn_conv))


def _outproj_kernel(x_ref, attn_p_ref, attn_s_ref, ys_ref, cn_ref, mod_ref, bn_ref, wglu_ref, bglu_ref, wout_ref,
                    npost_ref, npre2_ref, rw_ref, rb_ref, x1_ref, h2_ref, ti_ref, tg_ref, *, n_prompt):
    aw = attn_p_ref.shape[1]
    sw = ys_ref.shape[1]
    a = jnp.where(pl.program_id(0) < n_prompt, attn_p_ref[...], attn_s_ref[...]).astype(F32)
    an = a * _rsqrt_mean_sq(a) * bn_ref[:, 0:aw]
    y = ys_ref[...]
    z = jnp.dot(y.astype(BF16), wglu_ref[...], preferred_element_type=F32) + bglu_ref[...]
    sm = y * jax.nn.sigmoid(z)
    sn = sm * _rsqrt_mean_sq(sm) * bn_ref[:, aw:aw + sw]
    mix = (jnp.dot(an.astype(BF16), wout_ref[0:aw, :], preferred_element_type=F32)
           + jnp.dot(sn.astype(BF16), wout_ref[aw:aw + sw, :], preferred_element_type=F32)
           + jnp.dot(cn_ref[...], wout_ref[aw + sw:, :], preferred_element_type=F32))
    g1 = mod_ref[0, 2:3, :]
    sh2 = mod_ref[0, 3:4, :]
    sc2 = mod_ref[0, 4:5, :]
    x1 = x_ref[...] + g1 * (mix * _rsqrt_mean_sq(mix) * npost_ref[...])
    x1_ref[...] = x1
    h2 = x1 * _rsqrt_mean_sq(x1) * npre2_ref[...] * (1.0 + sc2) + sh2
    h2_ref[...] = h2
    h_hi = h2.astype(BF16)
    h_lo = (h2 - h_hi.astype(F32)).astype(BF16)
    both = jnp.dot(h_hi, rw_ref[...], preferred_element_type=F32)
    logits = (both[:, :LANES] + both[:, LANES:]
              + jnp.dot(h_lo, rw_ref[:, :LANES], preferred_element_type=F32) + rb_ref[...])
    tm = logits.shape[0]
    lane = lax.broadcasted_iota(I32, (tm, LANES), 1)
    vals = []
    ti = jnp.zeros((tm, LANES), I32)
    for kk in range(TOP_K):
        m = jnp.max(logits, axis=1, keepdims=True)
        ik = jnp.min(jnp.where(logits == m, lane, LANES), axis=1, keepdims=True)
        vals.append(m)
        ti = jnp.where(lane == kk, ik, ti)
        logits = jnp.where(lane == ik, NEG_BIG * 2.0, logits)
    es = [jnp.exp(v - vals[0]) for v in vals]
    inv = 1.0 / (es[0] + es[1] + es[2] + es[3])
    tg = jnp.zeros((tm, LANES), F32)
    for kk in range(TOP_K):
        tg = jnp.where(lane == kk, es[kk] * inv, tg)
    ti_ref[...] = ti.T[0:8, :]
    tg_ref[...] = tg


def _outproj(x, attn_p, attn_s, ys, cn, mod, bn, wglu_bf, bglu, wout_bf, npost, npre2, rw_pad, rb_pad, *, lc, tm):
    t, d = x.shape
    nb = lc // tm
    s = t // lc
    aw, sw, cw = attn_p.shape[1], ys.shape[1], cn.shape[1]
    nbp = attn_p.shape[0] // tm
    n_prompt = nbp // nb

    def row(si, i):
        return si * nb + i

    rows = lambda w: pl.BlockSpec((tm, w), lambda si, i: (row(si, i), 0))
    const = lambda shape: pl.BlockSpec(shape, lambda si, i: (0, 0))
    return pl.pallas_call(
        functools.partial(_outproj_kernel, n_prompt=n_prompt),
        grid=(s, nb),
        in_specs=[rows(d),
                  pl.BlockSpec((tm, aw), lambda si, i: (jnp.minimum(row(si, i), nbp - 1), 0)),
                  pl.BlockSpec((tm, aw), lambda si, i: (jnp.maximum(row(si, i) - nbp, 0), 0)),
                  rows(sw), rows(cw),
                  pl.BlockSpec((1, 6, d), lambda si, i: (si, 0, 0)),
                  const((1, d)), const((sw, sw)), const((1, sw)), const((d, d)), const((1, d)), const((1, d)),
                  const((d, 2 * LANES)), const((1, LANES))],
        out_specs=[rows(d), rows(d), pl.BlockSpec((8, tm), lambda si, i: (0, row(si, i))), rows(LANES)],
        out_shape=[jax.ShapeDtypeStruct((t, d), F32), jax.ShapeDtypeStruct((t, d), F32),
                   jax.ShapeDtypeStruct((8, t), I32), jax.ShapeDtypeStruct((t, LANES), F32)],
        compiler_params=_cparams(("parallel", "arbitrary")),
        name="outproj_router",
    )(x, attn_p, attn_s, ys, cn, mod, bn, wglu_bf, bglu, wout_bf, npost, npre2, rw_pad, rb_pad)


def _route(topi, blk):
    t = topi.shape[1]
    n_assign = t * TOP_K
    flat_e = topi[:TOP_K].reshape(-1)
    eids = jnp.arange(N_EXPERTS, dtype=I32)
    onehot = (flat_e[:, None] == eids[None, :]).astype(I32)
    csum = jnp.cumsum(onehot, axis=0)
    counts = csum[-1]
    rank = jnp.sum((csum - onehot) * onehot, axis=1)
    padded = ((counts + blk - 1) // blk) * blk
    ends_p = jnp.cumsum(padded)
    starts_p = ends_p - padded
    dest = (jnp.sum(onehot * starts_p[None, :], axis=1) + rank).astype(I32)
    dest = jnp.pad(dest.reshape(TOP_K, t), ((0, 8 - TOP_K), (0, 0)))
    n_blocks = n_assign // blk + N_EXPERTS
    cap = n_blocks * blk
    n_used = (ends_p[-1] // blk).astype(I32)
    bidx = jnp.arange(n_blocks, dtype=I32)
    be_raw = jnp.sum((ends_p[None, :] <= (bidx * blk)[:, None]).astype(I32), axis=1)
    last_e = jnp.max(jnp.where(counts > 0, eids, 0))
    block_e = jnp.where(bidx < n_used, jnp.minimum(be_raw, N_EXPERTS - 1), last_e).astype(I32)
    fill_base = jnp.concatenate([starts_p + counts, ends_p[-1:]]).astype(I32)
    fill_n = jnp.concatenate([padded - counts, cap - ends_p[-1:]]).astype(I32)
    return dest, block_e, n_used.reshape(1), fill_base, fill_n


def _dispatch_kernel(fb_ref, fn_ref, dest_ref, h_ref, xs_hbm, zrow, sem, zsem):
    step = pl.program_id(0)
    tm = h_ref.shape[0]

    @pl.when(step == 0)
    def _():
        zrow[...] = jnp.zeros(zrow.shape, zrow.dtype)
        for e in range(N_EXPERTS + 1):
            base = fb_ref[e]
            n = fn_ref[e]

            def zissue(j, carry):
                pltpu.make_async_copy(zrow.at[pl.ds(0, 1)], xs_hbm.at[pl.ds(base + j, 1)], zsem).start()
                return carry

            def zwait(j, carry):
                pltpu.make_async_copy(zrow.at[pl.ds(0, 1)], xs_hbm.at[pl.ds(0, 1)], zsem).wait()
                return carry

            lax.fori_loop(0, n, zissue, 0)
            lax.fori_loop(0, n, zwait, 0)

    def issue(r, carry):
        for kk in range(TOP_K):
            pltpu.make_async_copy(h_ref.at[pl.ds(r, 1)],
                                  xs_hbm.at[pl.ds(dest_ref[kk, r], 1)], sem).start(priority=kk % 2)
        return carry

    lax.fori_loop(0, tm, issue, 0, unroll=4)
    for kk in range(TOP_K):
        pltpu.make_async_copy(h_ref, xs_hbm.at[pl.ds(0, tm)], sem).wait()


def _dispatch(h2, dest, fill_base, fill_n, *, cap, tm):
    t, d = h2.shape
    grid_spec = pltpu.PrefetchScalarGridSpec(
        num_scalar_prefetch=2,
        grid=(t // tm,),
        in_specs=[pl.BlockSpec((8, tm), lambda i, fb, fn: (0, i), memory_space=pltpu.SMEM),
                  pl.BlockSpec((tm, d), lambda i, fb, fn: (i, 0))],
        out_specs=pl.BlockSpec(memory_space=pl.ANY),
        scratch_shapes=[pltpu.VMEM((8, d), F32), pltpu.SemaphoreType.DMA(()), pltpu.SemaphoreType.DMA(())],
    )
    return pl.pallas_call(
        _dispatch_kernel,
        grid_spec=grid_spec,
        out_shape=jax.ShapeDtypeStruct((cap, d), F32),
        compiler_params=_cparams(("arbitrary",), disable_bounds_checks=True),
        name="moe_dispatch",
    )(fill_base, fill_n, dest, h2)


def _expert_kernel(be_ref, nu_ref, xs_ref, w1_ref, b1_ref, w2_ref, b2_ref, ys_ref, w1b, w2b):
    b = pl.program_id(0)
    de = w2_ref.shape[0]
    used = b < nu_ref[0]
    fresh = jnp.logical_or(b == 0, be_ref[b] != be_ref[jnp.maximum(b - 1, 0)])

    @pl.when(jnp.logical_and(used, fresh))
    def _():
        w1b[...] = w1_ref[...].astype(BF16)
        w2b[...] = w2_ref[...].astype(BF16)

    @pl.when(used)
    def _():
        a = jnp.dot(xs_ref[...].astype(BF16), w1b[...], preferred_element_type=F32) + b1_ref[...]
        gt = jnp.minimum(a[:, :de], SWIGLU_LIMIT)
        up = jnp.clip(a[:, de:], -SWIGLU_LIMIT, SWIGLU_LIMIT)
        act = (up + 1.0) * (gt * jax.nn.sigmoid(SWIGLU_ALPHA * gt))
        ys_ref[...] = jnp.dot(act.astype(BF16), w2b[...], preferred_element_type=F32) + b2_ref[...]

    @pl.when(jnp.logical_not(used))
    def _():
        ys_ref[...] = jnp.zeros(ys_ref.shape, ys_ref.dtype)


def _experts(xs, block_e, n_used, w1, b1, w2, b2, *, layer, blk):
    cap, d = xs.shape
    nl, ne, _, de2 = w1.shape
    de = w2.shape[2]
    nblk = cap // blk

    def xrow(b, be, nu):
        return (jnp.minimum(b, nu[0] - 1), 0)

    grid_spec = pltpu.PrefetchScalarGridSpec(
        num_scalar_prefetch=2,
        grid=(nblk,),
        in_specs=[pl.BlockSpec((blk, d), xrow),
                  pl.BlockSpec((None, None, d, de2), lambda b, be, nu: (layer, be[b], 0, 0)),
                  pl.BlockSpec((None, None, 1, de2), lambda b, be, nu: (layer, be[b], 0, 0)),
                  pl.BlockSpec((None, None, de, d), lambda b, be, nu: (layer, be[b], 0, 0)),
                  pl.BlockSpec((None, None, 1, d), lambda b, be, nu: (layer, be[b], 0, 0))],
        out_specs=pl.BlockSpec((blk, d), lambda b, be, nu: (b, 0)),
        scratch_shapes=[pltpu.VMEM((d, de2), BF16), pltpu.VMEM((de, d), BF16)],
    )
    return pl.pallas_call(
        _expert_kernel,
        grid_spec=grid_spec,
        out_shape=jax.ShapeDtypeStruct((cap, d), F32),
        compiler_params=_cparams(("arbitrary",), vmem_limit_bytes=EXPERT_VMEM_LIMIT),
        name="moe_experts",
    )(block_e, n_used, xs, w1, b1.reshape(nl, ne, 1, de2), w2, b2.reshape(nl, ne, 1, d))


def _combine_kernel(dest_ref, gate_ref, x1_ref, mod_ref, npost_ref, ys_hbm, o_ref, buf, sem):
    tm = x1_ref.shape[0]

    def issue(r, carry):
        for kk in range(TOP_K):
            pltpu.make_async_copy(ys_hbm.at[pl.ds(dest_ref[kk, r], 1)],
                                  buf.at[kk, pl.ds(r, 1)], sem).start(priority=kk % 2)
        return carry

    lax.fori_loop(0, tm, issue, 0, unroll=4)
    for kk in range(TOP_K):
        pltpu.make_async_copy(ys_hbm.at[pl.ds(0, tm)], buf.at[kk], sem).wait()
    gate = gate_ref[...]
    ff = gate[:, 0:1] * buf[0]
    for kk in range(1, TOP_K):
        ff = ff + gate[:, kk:kk + 1] * buf[kk]
    g2 = mod_ref[0, 5:6, :]
    o_ref[...] = x1_ref[...] + g2 * (ff * _rsqrt_mean_sq(ff) * npost_ref[...])


def _combine(ys, dest, gate, x1, mod, npost, *, lc, tm, seq0=0, n_seq=None):
    t, d = x1.shape
    nb = lc // tm
    s = t // lc - seq0 if n_seq is None else n_seq

    def row(si, i):
        return (seq0 + si) * nb + i

    return pl.pallas_call(
        _combine_kernel,
        grid=(s, nb),
        in_specs=[pl.BlockSpec((8, tm), lambda si, i: (0, row(si, i)), memory_space=pltpu.SMEM),
                  pl.BlockSpec((tm, LANES), lambda si, i: (row(si, i), 0)),
                  pl.BlockSpec((tm, d), lambda si, i: (row(si, i), 0)),
                  pl.BlockSpec((1, 6, d), lambda si, i: (seq0 + si, 0, 0)),
                  pl.BlockSpec((1, d), lambda si, i: (0, 0)),
                  pl.BlockSpec(memory_space=pl.ANY)],
        out_specs=pl.BlockSpec((tm, d), lambda si, i: (si * nb + i, 0)),
        out_shape=jax.ShapeDtypeStruct((s * lc, d), F32),
        scratch_shapes=[pltpu.VMEM((TOP_K, tm, d), F32), pltpu.SemaphoreType.DMA(())],
        compiler_params=_cparams(("arbitrary", "arbitrary"), disable_bounds_checks=True),
        name="moe_combine",
    )(dest, gate, x1, mod, npost, ys)


def _rope_tables(n_pos):
    rows = n_pos // GRID_W
    row_idx = jnp.repeat(jnp.arange(rows, dtype=F32), GRID_W)
    col_idx = jnp.tile(jnp.arange(GRID_W, dtype=F32), rows)
    inv = 1.0 / (ROPE_THETA ** (jnp.arange(0, ROPE_AXIS_DIM, 2, dtype=F32) / ROPE_AXIS_DIM))
    ar = row_idx[:, None] * inv[None, :]
    ac = col_idx[:, None] * inv[None, :]
    cos = jnp.concatenate([jnp.cos(ar), jnp.cos(ar), jnp.cos(ac), jnp.cos(ac)], axis=-1)
    sin = jnp.concatenate([-jnp.sin(ar), jnp.sin(ar), -jnp.sin(ac), jnp.sin(ac)], axis=-1)
    reps = LANES // HEAD_DIM
    return jnp.tile(cos, (1, reps)), jnp.tile(sin, (1, reps))


def _tiles(lc):
    tm = min(512, lc)
    tq = min(256, lc)
    return tm, tq


def kernel(x_prompt, x_sample, c_prompt, c_sample, w_ada, b_ada, norm_pre_mix, norm_post_mix, norm_pre_ffn, norm_post_ffn, w_in, q_norm, k_norm, ssm_lam_re, ssm_lam_im, ssm_log_dt, ssm_b_re, ssm_b_im, ssm_c_re, ssm_c_im, ssm_d, ssm_w_glu, ssm_b_glu, conv_w_dw, conv_b_dw, conv_ln_g, conv_ln_b, conv_w_pw, branch_norm, w_out, router_w, router_b, expert_w1, expert_b1, expert_w2, expert_b2):
    bsz, lc, d = x_prompt.shape
    dbsz, dl, _ = x_sample.shape
    assert dbsz == 1 and dl % lc == 0 and lc % S5_CHUNK == 0 and lc % GRID_W == 0
    n_sample = dl // lc
    n_seq = bsz + n_sample
    t_prompt = bsz * lc
    t = t_prompt + dl
    depth = w_ada.shape[0]
    ssm_w = ssm_d.shape[1]
    conv_w = conv_b_dw.shape[1]
    n_groups = ssm_w // SSM_GROUP
    tm, tq = _tiles(lc)
    tc = S5_CHUNK
    nc = t // tc
    cps = lc // tc
    kw = tc * SSM_GROUP
    assert t_prompt % dl == 0
    blk = min(MOE_BLOCK, t * TOP_K // N_EXPERTS)

    x = jnp.concatenate([x_prompt.reshape(t_prompt, d), x_sample.reshape(dl, d)], axis=0)
    c_all = jnp.concatenate([c_prompt, jnp.repeat(c_sample, n_sample, axis=0)], axis=0)
    cos_t, sin_t = _rope_tables(max(lc, dl))
    bd = (jnp.arange(LANES)[:, None] // HEAD_DIM == jnp.arange(LANES)[None, :] // HEAD_DIM).astype(BF16)
    reps = LANES // HEAD_DIM
    vec = lambda a: a.astype(F32).reshape(1, -1)
    s5_w = jax.vmap(lambda *p: _s5_weights(*p, tc))(ssm_lam_re, ssm_lam_im, ssm_log_dt, ssm_b_re, ssm_b_im,
                                                    ssm_c_re, ssm_c_im, ssm_d)

    for li in range(depth):
        mod = _modulation(c_all, w_ada[li], b_ada[li]).reshape(n_seq, 6, d)
        qkn = jnp.concatenate([jnp.tile(vec(q_norm[li]) * (HEAD_DIM ** -0.5 * math.log2(math.e)),
                                        (ATTN_W // LANES, reps)),
                               jnp.tile(vec(k_norm[li]), (1, reps)),
                               jnp.zeros((8 - ATTN_W // LANES - 1, LANES), F32)], axis=0)
        q_t, k, v_t, u, hc = _inproj(x, mod, vec(norm_pre_mix[li]), w_in[li].astype(BF16), qkn, cos_t, sin_t, bd,
                                     n_prompt=bsz, lc=lc, tm=tm, ssm_w=ssm_w, conv_w=conv_w)

        attn_p = _attention(q_t, k, v_t, row0=0, n_seq=bsz, lkv=lc, tq=min(2 * tq, lc), tk=tm)
        attn_s = _attention(q_t, k, v_t, row0=t_prompt, n_seq=1, lkv=dl, tq=tq, tk=tm)

        toep, ws, wc, decay = [w[li] for w in s5_w]
        ug = u.reshape(nc, tc, n_groups, SSM_GROUP).transpose(2, 0, 1, 3).reshape(n_groups, nc, kw)
        tr = math.gcd(nc, 384)
        y_intra, *contribs = _s5_intra(ug, toep, ws, tr=tr)
        st_p = _s5_scan(decay, contribs, chunk0=0, n_seq=bsz, cps=cps)
        st_s = _s5_scan(decay, contribs, chunk0=t_prompt // tc, n_seq=1, cps=dl // tc)
        states = [jnp.concatenate([a, b], axis=0) for a, b in zip(st_p, st_s)]
        yg = _s5_out(y_intra, states, wc, tr=tr)
        ys = yg.reshape(n_groups, nc, tc, SSM_GROUP).transpose(1, 2, 0, 3).reshape(t, ssm_w)

        bn = vec(branch_norm[li])
        cn = _conv_module(hc, conv_w_dw[li], conv_b_dw[li], conv_ln_g[li], conv_ln_b[li],
                          conv_w_pw[li].astype(BF16), bn[0, ATTN_W + ssm_w:], n_prompt=bsz, lc=lc, tm=tm)

        rw_pad = jnp.zeros((d, LANES), F32).at[:, :N_EXPERTS].set(router_w[li].astype(F32))
        rb_pad = jnp.full((1, LANES), NEG_BIG, F32).at[0, :N_EXPERTS].set(router_b[li].astype(F32))
        rw_hi = rw_pad.astype(BF16)
        rw_pad = jnp.concatenate([rw_hi, (rw_pad - rw_hi.astype(F32)).astype(BF16)], axis=1)
        x1, h2, topi, topg = _outproj(x, attn_p, attn_s, ys, cn, mod, bn, ssm_w_glu[li].astype(BF16), vec(ssm_b_glu[li]),
                                      w_out[li].astype(BF16), vec(norm_post_mix[li]), vec(norm_pre_ffn[li]),
                                      rw_pad, rb_pad, lc=lc, tm=tm)

        dest, block_e, n_used, fill_base, fill_n = _route(topi, blk)
        xs = _dispatch(h2, dest, fill_base, fill_n, cap=block_e.shape[0] * blk, tm=tm)
        ye = _experts(xs, block_e, n_used, expert_w1.astype(F32), expert_b1.astype(F32),
                      expert_w2.astype(F32), expert_b2.astype(F32), layer=li, blk=blk)
        if li + 1 < depth:
            x = _combine(ye, dest, topg, x1, mod, vec(norm_post_ffn[li]), lc=lc, tm=tm)
        else:
            out_p = _combine(ye, dest, topg, x1, mod, vec(norm_post_ffn[li]), lc=lc, tm=tm, seq0=0, n_seq=bsz)
            out_s = _combine(ye, dest, topg, x1, mod, vec(norm_post_ffn[li]), lc=lc, tm=tm, seq0=bsz,
                             n_seq=n_sample)

    return (out_p.reshape(bsz, lc, d), out_s.reshape(dbsz, dl, d))
```

```python
import functools
import math

import jax
import jax.numpy as jnp
from jax import lax
from jax.experimental import pallas as pl
from jax.experimental.pallas import tpu as pltpu

F32 = jnp.float32
BF16 = jnp.bfloat16
I32 = jnp.int32

HEAD_DIM = 64
N_HEADS = 8
N_KV_HEADS = 2
GQA_GROUP = N_HEADS // N_KV_HEADS
ATTN_W = N_HEADS * HEAD_DIM
KV_W = N_KV_HEADS * HEAD_DIM
SSM_GROUP = 16
SSM_STATE = 64
CONV_KERNEL = 31
CONV_PAD = CONV_KERNEL // 2
GRID_W = 64
ROPE_AXIS_DIM = HEAD_DIM // 2
ROPE_THETA = 10000.0
N_EXPERTS = 32
TOP_K = 4
SWIGLU_LIMIT = 7.0
SWIGLU_ALPHA = 1.702
EPS = 1e-6

LANES = 128
HALO = 16
V_ROWS = HEAD_DIM + 16
S5_CHUNK = 32
MOE_BLOCK = 512
EXPERT_COLS = 512
NEG_BIG = -1e30
VMEM_LIMIT = 48 * 1024 * 1024
EXPERT_VMEM_LIMIT = 56 * 1024 * 1024


def _cparams(sem, vmem_limit_bytes=VMEM_LIMIT, **kw):
    return pltpu.CompilerParams(dimension_semantics=sem, vmem_limit_bytes=vmem_limit_bytes, **kw)


def _rsqrt_mean_sq(x):
    return lax.rsqrt(jnp.mean(x * x, axis=-1, keepdims=True) + EPS)


def _mod_kernel(c_ref, w_ref, b_ref, o_ref):
    c = c_ref[...]
    s = c * jax.nn.sigmoid(c)
    o_ref[...] = jnp.dot(s, w_ref[...], preferred_element_type=F32,
                         precision=lax.Precision.HIGHEST) + b_ref[...]


def _modulation(c_all, w_ada, b_ada):
    s, d = c_all.shape
    n = w_ada.shape[1]
    tn = n // 4
    return pl.pallas_call(
        _mod_kernel,
        grid=(n // tn,),
        in_specs=[pl.BlockSpec((s, d), lambda j: (0, 0)),
                  pl.BlockSpec((d, tn), lambda j: (0, j)),
                  pl.BlockSpec((1, tn), lambda j: (0, j))],
        out_specs=pl.BlockSpec((s, tn), lambda j: (0, j)),
        out_shape=jax.ShapeDtypeStruct((s, n), F32),
        compiler_params=_cparams(("arbitrary",)),
        name="adaln_mod",
    )(c_all, w_ada, b_ada.reshape(1, n))


def _inproj_kernel(x_ref, mod_ref, npre_ref, w_ref, qkn_ref, cos_ref, sin_ref, bd_ref,
                   q_ref, k_ref, v_ref, u_ref, hc_ref):
    x = x_ref[...]
    sh = mod_ref[0, 0:1, :]
    sc = mod_ref[0, 1:2, :]
    h = x * _rsqrt_mean_sq(x) * npre_ref[...] * (1.0 + sc) + sh
    proj = jnp.dot(h.astype(BF16), w_ref[...], preferred_element_type=F32)
    tm = x.shape[0]
    cos = cos_ref[...]
    sin = sin_ref[...]
    lane = lax.broadcasted_iota(I32, (tm, LANES), 1)
    first_half = (lane % ROPE_AXIS_DIM) < (ROPE_AXIS_DIM // 2)
    n_qk = (ATTN_W + KV_W) // LANES
    for c in range(n_qk):
        xc = proj[:, c * LANES:(c + 1) * LANES]
        ms = jnp.dot((xc * xc).astype(BF16), bd_ref[...], preferred_element_type=F32) * (1.0 / HEAD_DIM)
        xn = xc * lax.rsqrt(ms + EPS) * qkn_ref[c:c + 1, :]
        half = ROPE_AXIS_DIM // 2
        swapped = jnp.where(first_half, pltpu.roll(xn, LANES - half, 1), pltpu.roll(xn, half, 1))
        rot = xn * cos + swapped * sin
        if c < ATTN_W // LANES:
            rot_t = rot.T.astype(BF16)
            q_ref[2 * c] = rot_t[:HEAD_DIM, :]
            q_ref[2 * c + 1] = rot_t[HEAD_DIM:, :]
        else:
            rot = rot.astype(BF16)
            k_ref[0] = rot[:, :HEAD_DIM]
            k_ref[1] = rot[:, HEAD_DIM:]
    o2 = ATTN_W + KV_W
    v_t = proj[:, o2:o2 + KV_W].T.astype(BF16)
    ones = jnp.ones((V_ROWS - HEAD_DIM, tm), BF16)
    for hh in range(N_KV_HEADS):
        v_ref[hh, 0, 0:HEAD_DIM, :] = v_t[hh * HEAD_DIM:(hh + 1) * HEAD_DIM, :]
        v_ref[hh, 0, HEAD_DIM:V_ROWS, :] = ones
    o3 = o2 + KV_W
    ssm_w = u_ref.shape[1]
    u_ref[...] = proj[:, o3:o3 + ssm_w].astype(BF16)
    o4 = o3 + ssm_w
    conv_w = hc_ref.shape[1]
    a = proj[:, o4:o4 + conv_w]
    gate = proj[:, o4 + conv_w:o4 + 2 * conv_w]
    hc_ref[...] = a * jax.nn.sigmoid(gate)


def _inproj(x, mod, norm_pre, w_in_bf, qkn, cos_t, sin_t, bd, *, n_prompt, lc, tm, ssm_w, conv_w):
    t, d = x.shape
    nb = lc // tm
    s = t // lc
    in_w = w_in_bf.shape[1]

    def row(si, i):
        return si * nb + i

    def pos(si, i):
        return jnp.where(si < n_prompt, i, (si - n_prompt) * nb + i)

    return pl.pallas_call(
        _inproj_kernel,
        grid=(s, nb),
        in_specs=[pl.BlockSpec((tm, d), lambda si, i: (row(si, i), 0)),
                  pl.BlockSpec((1, 6, d), lambda si, i: (si, 0, 0)),
                  pl.BlockSpec((1, d), lambda si, i: (0, 0)),
                  pl.BlockSpec((d, in_w), lambda si, i: (0, 0)),
                  pl.BlockSpec((8, LANES), lambda si, i: (0, 0)),
                  pl.BlockSpec((tm, LANES), lambda si, i: (pos(si, i), 0)),
                  pl.BlockSpec((tm, LANES), lambda si, i: (pos(si, i), 0)),
                  pl.BlockSpec((LANES, LANES), lambda si, i: (0, 0))],
        out_specs=[pl.BlockSpec((N_HEADS, HEAD_DIM, tm), lambda si, i: (0, 0, row(si, i))),
                   pl.BlockSpec((N_KV_HEADS, tm, HEAD_DIM), lambda si, i: (0, row(si, i), 0)),
                   pl.BlockSpec((N_KV_HEADS, 1, V_ROWS, tm), lambda si, i: (0, row(si, i), 0, 0)),
                   pl.BlockSpec((tm, ssm_w), lambda si, i: (row(si, i), 0)),
                   pl.BlockSpec((tm, conv_w), lambda si, i: (row(si, i), 0))],
        out_shape=[jax.ShapeDtypeStruct((N_HEADS, HEAD_DIM, t), BF16),
                   jax.ShapeDtypeStruct((N_KV_HEADS, t, HEAD_DIM), BF16),
                   jax.ShapeDtypeStruct((N_KV_HEADS, t // tm, V_ROWS, tm), BF16),
                   jax.ShapeDtypeStruct((t, ssm_w), BF16),
                   jax.ShapeDtypeStruct((t, conv_w), F32)],
        compiler_params=_cparams(("parallel", "arbitrary")),
        name="inproj",
    )(x, mod, norm_pre, w_in_bf, qkn, cos_t, sin_t, bd)


def _attn_kernel(q_ref, k_ref, v_ref, o_ref, s_ref, m_ref, acc_ref, *, tk):
    g, hd, tq = q_ref.shape
    q_t = jnp.concatenate([q_ref[i] for i in range(g)], axis=-1)
    n = k_ref.shape[1] // tk
    m_ref[...] = jnp.full(m_ref.shape, NEG_BIG, F32)
    acc_ref[...] = jnp.zeros(acc_ref.shape, F32)

    def scores(j):
        kc = k_ref[0, pl.ds(pl.multiple_of(j * tk, tk), tk), :]
        return jnp.dot(kc, q_t, preferred_element_type=F32)

    def step(j, slot, prefetch):
        if prefetch:
            s_ref[1 - slot] = scores(j + 1)
        s = s_ref[slot]
        m_prev = m_ref[...]
        m_next = jnp.maximum(m_prev, jnp.max(s, axis=0, keepdims=True))
        alpha = jnp.exp2(m_prev - m_next)
        p = jnp.exp2(s - m_next)
        acc_ref[...] = acc_ref[...] * alpha + jnp.dot(v_ref[0, j], p.astype(BF16), preferred_element_type=F32)
        m_ref[...] = m_next

    s_ref[0] = scores(0)
    n_pairs = (n - 1) // 2

    def body(jj, carry):
        step(2 * jj, 0, True)
        step(2 * jj + 1, 1, True)
        return carry

    lax.fori_loop(0, n_pairs, body, 0)
    if n - 2 * n_pairs == 2:
        step(2 * n_pairs, 0, True)
        step(2 * n_pairs + 1, 1, False)
    else:
        step(2 * n_pairs, 0, False)
    o_t = acc_ref[0:hd, :] * (1.0 / acc_ref[hd:hd + 1, :])
    o_ref[...] = jnp.concatenate([o_t[:, i * tq:(i + 1) * tq] for i in range(g)], axis=0).T.astype(o_ref.dtype)


def _attention(q_t, k, v_t, *, row0, n_seq, lkv, tq, tk):
    nq = lkv // tq
    qb0 = row0 // tq
    kb0 = row0 // lkv
    cols = GQA_GROUP * tq
    return pl.pallas_call(
        functools.partial(_attn_kernel, tk=tk),
        grid=(n_seq, N_KV_HEADS, nq),
        in_specs=[pl.BlockSpec((GQA_GROUP, HEAD_DIM, tq), lambda b, g, i: (g, 0, qb0 + b * nq + i)),
                  pl.BlockSpec((1, lkv, HEAD_DIM), lambda b, g, i: (g, kb0 + b, 0)),
                  pl.BlockSpec((1, lkv // tk, V_ROWS, tk), lambda b, g, i: (g, kb0 + b, 0, 0))],
        out_specs=pl.BlockSpec((tq, GQA_GROUP * HEAD_DIM), lambda b, g, i: (b * nq + i, g)),
        out_shape=jax.ShapeDtypeStruct((n_seq * lkv, ATTN_W), BF16),
        scratch_shapes=[pltpu.VMEM((2, tk, cols), F32), pltpu.VMEM((1, cols), F32),
                        pltpu.VMEM((V_ROWS, cols), F32)],
        compiler_params=_cparams(("parallel", "parallel", "arbitrary")),
        name="attention",
    )(q_t, k, v_t)


def _s5_weights(lam_re, lam_im, log_dt, b_re, b_im, c_re, c_im, d_skip, tc):
    hp = lax.Precision.HIGHEST
    lam = lax.complex(lam_re.astype(F32), lam_im.astype(F32))
    dt = jnp.exp(log_dt.astype(F32))[..., None]
    lam_dt = lam * dt
    lam_bar = jnp.exp(lam_dt)
    b_bar = ((lam_bar - 1.0) / lam)[..., None] * lax.complex(b_re.astype(F32), b_im.astype(F32))
    cmat = lax.complex(c_re.astype(F32), c_im.astype(F32))
    g, p = lam.shape[1], lam.shape[2]
    h = b_bar.shape[-1]
    taus = jnp.arange(tc + 1, dtype=F32)
    pw = jnp.exp(lam_dt[..., None] * taus)
    kern = jnp.einsum('dgop,dgpt,dgpi->dgtoi', cmat, pw, b_bar, precision=hp).real
    eye_h = jnp.eye(h, dtype=F32)
    kd = kern[0][:, 0] + kern[1][:, 0] + d_skip.astype(F32).reshape(g, h)[:, :, None] * eye_h
    by_lag = jnp.concatenate([jnp.flip(kern[1][:, 1:tc], axis=1), kd[:, None], kern[0][:, 1:tc]], axis=1)
    by_lag = by_lag.transpose(0, 2, 3, 1)
    padded = jnp.concatenate([by_lag, jnp.zeros((g, h, h, 1), F32)], axis=-1)
    skew = jnp.tile(padded, (1, 1, 1, tc))[..., :tc * (2 * tc - 1)].reshape(g, h, h, tc, 2 * tc - 1)
    full = skew[..., tc - 1:]
    toep = full.transpose(0, 3, 2, 4, 1).reshape(g, tc * h, tc * h)

    pf = jnp.flip(pw[0][:, :, :tc], axis=2)
    pb = pw[1][:, :, :tc]
    sf = pf[:, :, :, None] * b_bar[0][:, :, None, :]
    sb = pb[:, :, :, None] * b_bar[1][:, :, None, :]
    parts = [sf.real, sf.imag, sb.real, sb.imag]
    ws = jnp.stack([x.transpose(0, 2, 3, 1).reshape(g, tc * h, p) for x in parts], axis=2)
    odd = (jnp.arange(g) % 2)[:, None, None, None]
    zeros = jnp.zeros_like(ws)
    ws = jnp.where(odd == 0, jnp.concatenate([ws, zeros], -1), jnp.concatenate([zeros, ws], -1))

    ef = cmat[0].transpose(0, 2, 1)[:, :, None, :] * pw[0][:, :, 1:tc + 1][:, :, :, None]
    eb = cmat[1].transpose(0, 2, 1)[:, :, None, :] * jnp.flip(pw[1][:, :, 1:tc + 1], axis=2)[:, :, :, None]
    cparts = [ef.real, -ef.imag, eb.real, -eb.imag]
    wc = jnp.stack([x.reshape(g, p, tc * h) for x in cparts], axis=1)
    zc = jnp.zeros_like(wc)
    odd4 = (jnp.arange(g) % 2)[:, None, None, None]
    wc = jnp.where(odd4 == 0, jnp.concatenate([wc, zc], 2), jnp.concatenate([zc, wc], 2))
    wc = wc.reshape(g, 4 * 2 * p, tc * h)

    af = pw[0][:, :, tc].reshape(1, g * p)
    ab = pw[1][:, :, tc].reshape(1, g * p)
    decay = jnp.concatenate([af.real, af.imag, ab.real, ab.imag], axis=0)
    return toep.astype(BF16), ws.reshape(g, tc * h, 4 * 2 * p).astype(BF16), wc.astype(BF16), decay


def _s5_intra_kernel(u_ref, toep_ref, ws_ref, y_ref, s0_ref, s1_ref, s2_ref, s3_ref):
    u0 = u_ref[0]
    u1 = u_ref[1]
    y_ref[0] = jnp.dot(u0, toep_ref[0], preferred_element_type=F32)
    y_ref[1] = jnp.dot(u1, toep_ref[1], preferred_element_type=F32)
    st = (jnp.dot(u0, ws_ref[0], preferred_element_type=F32)
          + jnp.dot(u1, ws_ref[1], preferred_element_type=F32))
    for k, ref in enumerate((s0_ref, s1_ref, s2_ref, s3_ref)):
        ref[...] = st[:, k * LANES:(k + 1) * LANES]


def _s5_intra(ug, toep, ws, *, tr):
    g, nc, kw = ug.shape
    npair = g // 2
    st_shape = jax.ShapeDtypeStruct((nc, npair * LANES), F32)
    st_spec = pl.BlockSpec((tr, LANES), lambda pr, r: (r, pr))
    return pl.pallas_call(
        _s5_intra_kernel,
        grid=(npair, nc // tr),
        in_specs=[pl.BlockSpec((2, tr, kw), lambda pr, r: (pr, r, 0)),
                  pl.BlockSpec((2, kw, kw), lambda pr, r: (pr, 0, 0)),
                  pl.BlockSpec((2, kw, 4 * LANES), lambda pr, r: (pr, 0, 0))],
        out_specs=[pl.BlockSpec((2, tr, kw), lambda pr, r: (pr, r, 0)), st_spec, st_spec, st_spec, st_spec],
        out_shape=[jax.ShapeDtypeStruct((g, nc, kw), F32), st_shape, st_shape, st_shape, st_shape],
        compiler_params=_cparams(("parallel", "arbitrary")),
        name="s5_intra",
    )(ug, toep, ws)


def _s5_scan_kernel(decay_ref, cfr_ref, cfi_ref, cbr_ref, cbi_ref, sfr_ref, sfi_ref, sbr_ref, sbi_ref):
    n = cfr_ref.shape[0]
    w = cfr_ref.shape[1]
    afr = decay_ref[0:1, :]
    afi = decay_ref[1:2, :]
    abr = decay_ref[2:3, :]
    abi = decay_ref[3:4, :]
    zero = jnp.zeros((1, w), F32)

    def fwd(c, st):
        re, im = st
        sfr_ref[pl.ds(c, 1), :] = re
        sfi_ref[pl.ds(c, 1), :] = im
        cr = cfr_ref[pl.ds(c, 1), :]
        ci = cfi_ref[pl.ds(c, 1), :]
        return afr * re - afi * im + cr, afr * im + afi * re + ci

    lax.fori_loop(0, n, fwd, (zero, zero))

    def bwd(kk, st):
        c = n - 1 - kk
        re, im = st
        sbr_ref[pl.ds(c, 1), :] = re
        sbi_ref[pl.ds(c, 1), :] = im
        cr = cbr_ref[pl.ds(c, 1), :]
        ci = cbi_ref[pl.ds(c, 1), :]
        return abr * re - abi * im + cr, abr * im + abi * re + ci

    lax.fori_loop(0, n, bwd, (zero, zero))


def _s5_scan(decay, contribs, *, chunk0, n_seq, cps):
    w = contribs[0].shape[1]
    b0 = chunk0 // cps
    spec_in = pl.BlockSpec((cps, w), lambda b: (b0 + b, 0))
    spec_out = pl.BlockSpec((cps, w), lambda b: (b, 0))
    shape = jax.ShapeDtypeStruct((n_seq * cps, w), F32)
    return pl.pallas_call(
        _s5_scan_kernel,
        grid=(n_seq,),
        in_specs=[pl.BlockSpec((4, w), lambda b: (0, 0))] + [spec_in] * 4,
        out_specs=[spec_out] * 4,
        out_shape=[shape] * 4,
        compiler_params=_cparams(("parallel",)),
        name="s5_scan",
    )(decay, *contribs)


def _s5_out_kernel(y_ref, s0_ref, s1_ref, s2_ref, s3_ref, wc_ref, o_ref):
    st = jnp.concatenate([s0_ref[...], s1_ref[...], s2_ref[...], s3_ref[...]], axis=-1).astype(BF16)
    for k in range(2):
        y = y_ref[k] + jnp.dot(st, wc_ref[k], preferred_element_type=F32)
        o_ref[k] = 0.5 * y * (1.0 + lax.erf(y * (2.0 ** -0.5)))


def _s5_out(y_intra, states, wc, *, tr):
    g, nc, kw = y_intra.shape
    npair = g // 2
    st_spec = pl.BlockSpec((tr, LANES), lambda pr, r: (r, pr))
    return pl.pallas_call(
        _s5_out_kernel,
        grid=(npair, nc // tr),
        in_specs=[pl.BlockSpec((2, tr, kw), lambda pr, r: (pr, r, 0)), st_spec, st_spec, st_spec, st_spec,
                  pl.BlockSpec((2, 4 * LANES, kw), lambda pr, r: (pr, 0, 0))],
        out_specs=pl.BlockSpec((2, tr, kw), lambda pr, r: (pr, r, 0)),
        out_shape=jax.ShapeDtypeStruct((g, nc, kw), F32),
        compiler_params=_cparams(("parallel", "arbitrary")),
        name="s5_out",
    )(y_intra, *states, wc)


def _conv_kernel(prev_ref, cur_ref, next_ref, wdw_ref, bdw_ref, lng_ref, lnb_ref, wpw_ref, bn_ref, o_ref, scr,
                 *, n_prompt, n_seq, nb, rt):
    si = pl.program_id(0)
    i = pl.program_id(1)
    tm = cur_ref.shape[0]
    prev_ok = jnp.logical_or(i > 0, si > n_prompt)
    next_ok = jnp.logical_or(i < nb - 1, jnp.logical_and(si >= n_prompt, si < n_seq - 1))
    scr[0:HALO, :] = jnp.where(prev_ok, prev_ref[...], 0.0)
    scr[HALO:HALO + tm, :] = cur_ref[...]
    scr[HALO + tm:2 * HALO + tm, :] = jnp.where(next_ok, next_ref[...], 0.0)
    off = HALO - CONV_PAD
    sub = 8
    for r0 in range(0, tm, rt):
        slabs = [scr[r0 + sub * a:r0 + sub * a + rt + sub, :] for a in range((off + CONV_KERNEL - 1) // sub + 1)]
        acc = None
        for b in range(sub):
            part = None
            for a, slab in enumerate(slabs):
                kk = sub * a + b - off
                if 0 <= kk < CONV_KERNEL:
                    term = wdw_ref[kk:kk + 1, :] * slab
                    part = term if part is None else part + term
            part = part[b:b + rt, :]
            acc = part if acc is None else acc + part
        acc = acc + bdw_ref[...]
        mu = jnp.mean(acc, axis=-1, keepdims=True)
        cen = acc - mu
        var = jnp.mean(cen * cen, axis=-1, keepdims=True)
        y = cen * lax.rsqrt(var + EPS) * lng_ref[...] + lnb_ref[...]
        y = y * jax.nn.sigmoid(y)
        z = jnp.dot(y.astype(BF16), wpw_ref[...], preferred_element_type=F32)
        o_ref[r0:r0 + rt, :] = (z * _rsqrt_mean_sq(z) * bn_ref[...]).astype(o_ref.dtype)


def _conv_module(hc, w_dw, b_dw, ln_g, ln_b, w_pw_bf, bn_conv, *, n_prompt, lc, tm):
    t, cw = hc.shape
    nb = lc // tm
    s = t // lc
    hb = tm // HALO
    n_hblk = t // HALO
    kpad = 32

    def row(si, i):
        return si * nb + i

    wdw = jnp.zeros((kpad, cw), F32).at[:CONV_KERNEL].set(w_dw.astype(F32))
    vec = lambda a: a.astype(F32).reshape(1, cw)
    const = lambda shape: pl.BlockSpec(shape, lambda si, i: (0, 0))
    return pl.pallas_call(
        functools.partial(_conv_kernel, n_prompt=n_prompt, n_seq=s, nb=nb, rt=min(128, tm)),
        grid=(s, nb),
        in_specs=[pl.BlockSpec((HALO, cw), lambda si, i: (jnp.maximum(row(si, i) * hb - 1, 0), 0)),
                  pl.BlockSpec((tm, cw), lambda si, i: (row(si, i), 0)),
                  pl.BlockSpec((HALO, cw), lambda si, i: (jnp.minimum((row(si, i) + 1) * hb, n_hblk - 1), 0)),
                  const((kpad, cw)), const((1, cw)), const((1, cw)), const((1, cw)), const((cw, cw)),
                  const((1, cw))],
        out_specs=pl.BlockSpec((tm, cw), lambda si, i: (row(si, i), 0)),
        out_shape=jax.ShapeDtypeStruct((t, cw), BF16),
        scratch_shapes=[pltpu.VMEM((tm + 2 * HALO, cw), F32)],
        compiler_params=_cparams(("parallel", "arbitrary")),
        name="conv_module",
    )(hc, hc, hc, wdw, vec(b_dw), vec(ln_g), vec(ln_b), w_pw_bf, vec(bn_conv))


def _outproj_kernel(x_ref, attn_p_ref, attn_s_ref, ys_ref, cn_ref, mod_ref, bn_ref, wglu_ref, bglu_ref, wout_ref,
                    npost_ref, npre2_ref, rw_ref, rb_ref, x1_ref, h2_ref, ti_ref, tg_ref, *, n_prompt):
    aw = attn_p_ref.shape[1]
    sw = ys_ref.shape[1]
    a = jnp.where(pl.program_id(0) < n_prompt, attn_p_ref[...], attn_s_ref[...]).astype(F32)
    an = a * _rsqrt_mean_sq(a) * bn_ref[:, 0:aw]
    y = ys_ref[...]
    z = jnp.dot(y.astype(BF16), wglu_ref[...], preferred_element_type=F32) + bglu_ref[...]
    sm = y * jax.nn.sigmoid(z)
    sn = sm * _rsqrt_mean_sq(sm) * bn_ref[:, aw:aw + sw]
    mix = (jnp.dot(an.astype(BF16), wout_ref[0:aw, :], preferred_element_type=F32)
           + jnp.dot(sn.astype(BF16), wout_ref[aw:aw + sw, :], preferred_element_type=F32)
           + jnp.dot(cn_ref[...], wout_ref[aw + sw:, :], preferred_element_type=F32))
    g1 = mod_ref[0, 2:3, :]
    sh2 = mod_ref[0, 3:4, :]
    sc2 = mod_ref[0, 4:5, :]
    x1 = x_ref[...] + g1 * (mix * _rsqrt_mean_sq(mix) * npost_ref[...])
    x1_ref[...] = x1
    h2 = x1 * _rsqrt_mean_sq(x1) * npre2_ref[...] * (1.0 + sc2) + sh2
    h2_ref[...] = h2
    h_hi = h2.astype(BF16)
    h_lo = (h2 - h_hi.astype(F32)).astype(BF16)
    both = jnp.dot(h_hi, rw_ref[...], preferred_element_type=F32)
    logits = (both[:, :LANES] + both[:, LANES:]
              + jnp.dot(h_lo, rw_ref[:, :LANES], preferred_element_type=F32) + rb_ref[...])
    tm = logits.shape[0]
    lane = lax.broadcasted_iota(I32, (tm, LANES), 1)
    vals = []
    ti = jnp.zeros((tm, LANES), I32)
    for kk in range(TOP_K):
        m = jnp.max(logits, axis=1, keepdims=True)
        ik = jnp.min(jnp.where(logits == m, lane, LANES), axis=1, keepdims=True)
        vals.append(m)
        ti = jnp.where(lane == kk, ik, ti)
        logits = jnp.where(lane == ik, NEG_BIG * 2.0, logits)
    es = [jnp.exp(v - vals[0]) for v in vals]
    inv = 1.0 / (es[0] + es[1] + es[2] + es[3])
    tg = jnp.zeros((tm, LANES), F32)
    for kk in range(TOP_K):
        tg = jnp.where(lane == kk, es[kk] * inv, tg)
    ti_ref[...] = ti.T[0:8, :]
    tg_ref[...] = tg


def _outproj(x, attn_p, attn_s, ys, cn, mod, bn, wglu_bf, bglu, wout_bf, npost, npre2, rw_pad, rb_pad, *, lc, tm):
    t, d = x.shape
    nb = lc // tm
    s = t // lc
    aw, sw, cw = attn_p.shape[1], ys.shape[1], cn.shape[1]
    nbp = attn_p.shape[0] // tm
    n_prompt = nbp // nb

    def row(si, i):
        return si * nb + i

    rows = lambda w: pl.BlockSpec((tm, w), lambda si, i: (row(si, i), 0))
    const = lambda shape: pl.BlockSpec(shape, lambda si, i: (0, 0))
    return pl.pallas_call(
        functools.partial(_outproj_kernel, n_prompt=n_prompt),
        grid=(s, nb),
        in_specs=[rows(d),
                  pl.BlockSpec((tm, aw), lambda si, i: (jnp.minimum(row(si, i), nbp - 1), 0)),
                  pl.BlockSpec((tm, aw), lambda si, i: (jnp.maximum(row(si, i) - nbp, 0), 0)),
                  rows(sw), rows(cw),
                  pl.BlockSpec((1, 6, d), lambda si, i: (si, 0, 0)),
                  const((1, d)), const((sw, sw)), const((1, sw)), const((d, d)), const((1, d)), const((1, d)),
                  const((d, 2 * LANES)), const((1, LANES))],
        out_specs=[rows(d), rows(d), pl.BlockSpec((8, tm), lambda si, i: (0, row(si, i))), rows(LANES)],
        out_shape=[jax.ShapeDtypeStruct((t, d), F32), jax.ShapeDtypeStruct((t, d), F32),
                   jax.ShapeDtypeStruct((8, t), I32), jax.ShapeDtypeStruct((t, LANES), F32)],
        compiler_params=_cparams(("parallel", "arbitrary")),
        name="outproj_router",
    )(x, attn_p, attn_s, ys, cn, mod, bn, wglu_bf, bglu, wout_bf, npost, npre2, rw_pad, rb_pad)


def _route(topi, blk):
    t = topi.shape[1]
    n_assign = t * TOP_K
    flat_e = topi[:TOP_K].reshape(-1)
    eids = jnp.arange(N_EXPERTS, dtype=I32)
    onehot = (flat_e[:, None] == eids[None, :]).astype(I32)
    cb = math.gcd(n_assign, 512)
    tri = (jnp.arange(cb)[:, None] >= jnp.arange(cb)[None, :]).astype(BF16)
    within = jnp.einsum('ij,bjc->bic', tri, onehot.reshape(n_assign // cb, cb, N_EXPERTS).astype(BF16),
                        preferred_element_type=F32).astype(I32)
    block_tot = within[:, -1, :]
    csum = (within + (jnp.cumsum(block_tot, axis=0) - block_tot)[:, None, :]).reshape(n_assign, N_EXPERTS)
    counts = csum[-1]
    rank = jnp.sum((csum - onehot) * onehot, axis=1)
    padded = ((counts + blk - 1) // blk) * blk
    ends_p = jnp.cumsum(padded)
    starts_p = ends_p - padded
    dest = (jnp.sum(onehot * starts_p[None, :], axis=1) + rank).astype(I32)
    dest = jnp.pad(dest.reshape(TOP_K, t), ((0, 8 - TOP_K), (0, 0)))
    n_blocks = n_assign // blk + N_EXPERTS
    cap = n_blocks * blk
    n_used = (ends_p[-1] // blk).astype(I32)
    bidx = jnp.arange(n_blocks, dtype=I32)
    be_raw = jnp.sum((ends_p[None, :] <= (bidx * blk)[:, None]).astype(I32), axis=1)
    last_e = jnp.max(jnp.where(counts > 0, eids, 0))
    block_e = jnp.where(bidx < n_used, jnp.minimum(be_raw, N_EXPERTS - 1), last_e).astype(I32)
    fill_base = jnp.concatenate([starts_p + counts, ends_p[-1:]]).astype(I32)
    fill_n = jnp.concatenate([padded - counts, cap - ends_p[-1:]]).astype(I32)
    return dest, block_e, n_used.reshape(1), fill_base, fill_n


def _dispatch_kernel(fb_ref, fn_ref, dest_ref, h_ref, xs_hbm, zrow, sem, zsem):
    step = pl.program_id(0)
    tm = h_ref.shape[0]

    @pl.when(step == 0)
    def _():
        zrow[...] = jnp.zeros(zrow.shape, zrow.dtype)
        for e in range(N_EXPERTS + 1):
            base = fb_ref[e]
            n = fn_ref[e]

            def zissue(j, carry):
                pltpu.make_async_copy(zrow.at[pl.ds(0, 1)], xs_hbm.at[pl.ds(base + j, 1)], zsem).start()
                return carry

            def zwait(j, carry):
                pltpu.make_async_copy(zrow.at[pl.ds(0, 1)], xs_hbm.at[pl.ds(0, 1)], zsem).wait()
                return carry

            lax.fori_loop(0, n, zissue, 0)
            lax.fori_loop(0, n, zwait, 0)

    def issue(r, carry):
        for kk in range(TOP_K):
            pltpu.make_async_copy(h_ref.at[pl.ds(r, 1)],
                                  xs_hbm.at[pl.ds(dest_ref[kk, r], 1)], sem).start(priority=kk % 2)
        return carry

    lax.fori_loop(0, tm, issue, 0, unroll=4)
    for kk in range(TOP_K):
        pltpu.make_async_copy(h_ref, xs_hbm.at[pl.ds(0, tm)], sem).wait()


def _dispatch(h2, dest, fill_base, fill_n, *, cap, tm):
    t, d = h2.shape
    grid_spec = pltpu.PrefetchScalarGridSpec(
        num_scalar_prefetch=2,
        grid=(t // tm,),
        in_specs=[pl.BlockSpec((8, tm), lambda i, fb, fn: (0, i), memory_space=pltpu.SMEM),
                  pl.BlockSpec((tm, d), lambda i, fb, fn: (i, 0))],
        out_specs=pl.BlockSpec(memory_space=pl.ANY),
        scratch_shapes=[pltpu.VMEM((8, d), F32), pltpu.SemaphoreType.DMA(()), pltpu.SemaphoreType.DMA(())],
    )
    return pl.pallas_call(
        _dispatch_kernel,
        grid_spec=grid_spec,
        out_shape=jax.ShapeDtypeStruct((cap, d), F32),
        compiler_params=_cparams(("arbitrary",), disable_bounds_checks=True),
        name="moe_dispatch",
    )(fill_base, fill_n, dest, h2)


def _expert_kernel(be_ref, nu_ref, xs_ref, w1_ref, b1_ref, w2_ref, b2_ref, ys_ref, w1b, w2b):
    b = pl.program_id(0)
    de = w2_ref.shape[0]
    used = b < nu_ref[0]
    fresh = jnp.logical_or(b == 0, be_ref[b] != be_ref[jnp.maximum(b - 1, 0)])

    @pl.when(jnp.logical_and(used, fresh))
    def _():
        w1b[...] = w1_ref[...].astype(BF16)
        w2b[...] = w2_ref[...].astype(BF16)

    @pl.when(used)
    def _():
        x = xs_ref[...].astype(BF16)
        y = None
        for c0 in range(0, de, EXPERT_COLS):
            c1 = c0 + EXPERT_COLS
            gt = jnp.dot(x, w1b[:, c0:c1], preferred_element_type=F32) + b1_ref[:, c0:c1]
            up = jnp.dot(x, w1b[:, de + c0:de + c1], preferred_element_type=F32) + b1_ref[:, de + c0:de + c1]
            gt = jnp.minimum(gt, SWIGLU_LIMIT)
            up = jnp.clip(up, -SWIGLU_LIMIT, SWIGLU_LIMIT)
            act = (up + 1.0) * (gt * jax.nn.sigmoid(SWIGLU_ALPHA * gt))
            part = jnp.dot(act.astype(BF16), w2b[c0:c1, :], preferred_element_type=F32)
            y = part if y is None else y + part
        ys_ref[...] = y + b2_ref[...]

    @pl.when(jnp.logical_not(used))
    def _():
        ys_ref[...] = jnp.zeros(ys_ref.shape, ys_ref.dtype)


def _experts(xs, block_e, n_used, w1, b1, w2, b2, *, layer, blk):
    cap, d = xs.shape
    nl, ne, _, de2 = w1.shape
    de = w2.shape[2]
    nblk = cap // blk

    def xrow(b, be, nu):
        return (jnp.minimum(b, nu[0] - 1), 0)

    grid_spec = pltpu.PrefetchScalarGridSpec(
        num_scalar_prefetch=2,
        grid=(nblk,),
        in_specs=[pl.BlockSpec((blk, d), xrow),
                  pl.BlockSpec((None, None, d, de2), lambda b, be, nu: (layer, be[b], 0, 0)),
                  pl.BlockSpec((None, None, 1, de2), lambda b, be, nu: (layer, be[b], 0, 0)),
                  pl.BlockSpec((None, None, de, d), lambda b, be, nu: (layer, be[b], 0, 0)),
                  pl.BlockSpec((None, None, 1, d), lambda b, be, nu: (layer, be[b], 0, 0))],
        out_specs=pl.BlockSpec((blk, d), lambda b, be, nu: (b, 0)),
        scratch_shapes=[pltpu.VMEM((d, de2), BF16), pltpu.VMEM((de, d), BF16)],
    )
    return pl.pallas_call(
        _expert_kernel,
        grid_spec=grid_spec,
        out_shape=jax.ShapeDtypeStruct((cap, d), F32),
        compiler_params=_cparams(("arbitrary",), vmem_limit_bytes=EXPERT_VMEM_LIMIT),
        name="moe_experts",
    )(block_e, n_used, xs, w1, b1.reshape(nl, ne, 1, de2), w2, b2.reshape(nl, ne, 1, d))


def _combine_kernel(dest_ref, gate_ref, x1_ref, mod_ref, npost_ref, ys_hbm, o_ref, buf, sem):
    tm = x1_ref.shape[0]

    def issue(r, carry):
        for kk in range(TOP_K):
            pltpu.make_async_copy(ys_hbm.at[pl.ds(dest_ref[kk, r], 1)],
                                  buf.at[kk, pl.ds(r, 1)], sem).start(priority=kk % 2)
        return carry

    lax.fori_loop(0, tm, issue, 0, unroll=4)
    for kk in range(TOP_K):
        pltpu.make_async_copy(ys_hbm.at[pl.ds(0, tm)], buf.at[kk], sem).wait()
    gate = gate_ref[...]
    ff = gate[:, 0:1] * buf[0]
    for kk in range(1, TOP_K):
        ff = ff + gate[:, kk:kk + 1] * buf[kk]
    g2 = mod_ref[0, 5:6, :]
    o_ref[...] = x1_ref[...] + g2 * (ff * _rsqrt_mean_sq(ff) * npost_ref[...])


def _combine(ys, dest, gate, x1, mod, npost, *, lc, tm, seq0=0, n_seq=None):
    t, d = x1.shape
    nb = lc // tm
    s = t // lc - seq0 if n_seq is None else n_seq

    def row(si, i):
        return (seq0 + si) * nb + i

    return pl.pallas_call(
        _combine_kernel,
        grid=(s, nb),
        in_specs=[pl.BlockSpec((8, tm), lambda si, i: (0, row(si, i)), memory_space=pltpu.SMEM),
                  pl.BlockSpec((tm, LANES), lambda si, i: (row(si, i), 0)),
                  pl.BlockSpec((tm, d), lambda si, i: (row(si, i), 0)),
                  pl.BlockSpec((1, 6, d), lambda si, i: (seq0 + si, 0, 0)),
                  pl.BlockSpec((1, d), lambda si, i: (0, 0)),
                  pl.BlockSpec(memory_space=pl.ANY)],
        out_specs=pl.BlockSpec((tm, d), lambda si, i: (si * nb + i, 0)),
        out_shape=jax.ShapeDtypeStruct((s * lc, d), F32),
        scratch_shapes=[pltpu.VMEM((TOP_K, tm, d), F32), pltpu.SemaphoreType.DMA(())],
        compiler_params=_cparams(("arbitrary", "arbitrary"), disable_bounds_checks=True),
        name="moe_combine",
    )(dest, gate, x1, mod, npost, ys)


def _rope_tables(n_pos):
    rows = n_pos // GRID_W
    row_idx = jnp.repeat(jnp.arange(rows, dtype=F32), GRID_W)
    col_idx = jnp.tile(jnp.arange(GRID_W, dtype=F32), rows)
    inv = 1.0 / (ROPE_THETA ** (jnp.arange(0, ROPE_AXIS_DIM, 2, dtype=F32) / ROPE_AXIS_DIM))
    ar = row_idx[:, None] * inv[None, :]
    ac = col_idx[:, None] * inv[None, :]
    cos = jnp.concatenate([jnp.cos(ar), jnp.cos(ar), jnp.cos(ac), jnp.cos(ac)], axis=-1)
    sin = jnp.concatenate([-jnp.sin(ar), jnp.sin(ar), -jnp.sin(ac), jnp.sin(ac)], axis=-1)
    reps = LANES // HEAD_DIM
    return jnp.tile(cos, (1, reps)), jnp.tile(sin, (1, reps))


def _tiles(lc):
    tm = min(512, lc)
    tq = min(256, lc)
    return tm, tq


def kernel(x_prompt, x_sample, c_prompt, c_sample, w_ada, b_ada, norm_pre_mix, norm_post_mix, norm_pre_ffn, norm_post_ffn, w_in, q_norm, k_norm, ssm_lam_re, ssm_lam_im, ssm_log_dt, ssm_b_re, ssm_b_im, ssm_c_re, ssm_c_im, ssm_d, ssm_w_glu, ssm_b_glu, conv_w_dw, conv_b_dw, conv_ln_g, conv_ln_b, conv_w_pw, branch_norm, w_out, router_w, router_b, expert_w1, expert_b1, expert_w2, expert_b2):
    bsz, lc, d = x_prompt.shape
    dbsz, dl, _ = x_sample.shape
    assert dbsz == 1 and dl % lc == 0 and lc % S5_CHUNK == 0 and lc % GRID_W == 0
    n_sample = dl // lc
    n_seq = bsz + n_sample
    t_prompt = bsz * lc
    t = t_prompt + dl
    depth = w_ada.shape[0]
    ssm_w = ssm_d.shape[1]
    conv_w = conv_b_dw.shape[1]
    n_groups = ssm_w // SSM_GROUP
    tm, tq = _tiles(lc)
    tc = S5_CHUNK
    nc = t // tc
    cps = lc // tc
    kw = tc * SSM_GROUP
    assert t_prompt % dl == 0
    blk = min(MOE_BLOCK, t * TOP_K // N_EXPERTS)

    x = jnp.concatenate([x_prompt.reshape(t_prompt, d), x_sample.reshape(dl, d)], axis=0)
    c_all = jnp.concatenate([c_prompt, jnp.repeat(c_sample, n_sample, axis=0)], axis=0)
    cos_t, sin_t = _rope_tables(max(lc, dl))
    bd = (jnp.arange(LANES)[:, None] // HEAD_DIM == jnp.arange(LANES)[None, :] // HEAD_DIM).astype(BF16)
    reps = LANES // HEAD_DIM
    vec = lambda a: a.astype(F32).reshape(1, -1)
    s5_w = jax.vmap(lambda *p: _s5_weights(*p, tc))(ssm_lam_re, ssm_lam_im, ssm_log_dt, ssm_b_re, ssm_b_im,
                                                    ssm_c_re, ssm_c_im, ssm_d)

    for li in range(depth):
        mod = _modulation(c_all, w_ada[li], b_ada[li]).reshape(n_seq, 6, d)
        qkn = jnp.concatenate([jnp.tile(vec(q_norm[li]) * (HEAD_DIM ** -0.5 * math.log2(math.e)),
                                        (ATTN_W // LANES, reps)),
                               jnp.tile(vec(k_norm[li]), (1, reps)),
                               jnp.zeros((8 - ATTN_W // LANES - 1, LANES), F32)], axis=0)
        q_t, k, v_t, u, hc = _inproj(x, mod, vec(norm_pre_mix[li]), w_in[li].astype(BF16), qkn, cos_t, sin_t, bd,
                                     n_prompt=bsz, lc=lc, tm=tm, ssm_w=ssm_w, conv_w=conv_w)

        attn_p = _attention(q_t, k, v_t, row0=0, n_seq=bsz, lkv=lc, tq=min(2 * tq, lc), tk=tm)
        attn_s = _attention(q_t, k, v_t, row0=t_prompt, n_seq=1, lkv=dl, tq=tq, tk=tm)

        toep, ws, wc, decay = [w[li] for w in s5_w]
        ug = u.reshape(nc, tc, n_groups, SSM_GROUP).transpose(2, 0, 1, 3).reshape(n_groups, nc, kw)
        tr = math.gcd(nc, 384)
        y_intra, *contribs = _s5_intra(ug, toep, ws, tr=tr)
        st_p = _s5_scan(decay, contribs, chunk0=0, n_seq=bsz, cps=cps)
        st_s = _s5_scan(decay, contribs, chunk0=t_prompt // tc, n_seq=1, cps=dl // tc)
        states = [jnp.concatenate([a, b], axis=0) for a, b in zip(st_p, st_s)]
        yg = _s5_out(y_intra, states, wc, tr=tr)
        ys = yg.reshape(n_groups, nc, tc, SSM_GROUP).transpose(1, 2, 0, 3).reshape(t, ssm_w)

        bn = vec(branch_norm[li])
        cn = _conv_module(hc, conv_w_dw[li], conv_b_dw[li], conv_ln_g[li], conv_ln_b[li],
                          conv_w_pw[li].astype(BF16), bn[0, ATTN_W + ssm_w:], n_prompt=bsz, lc=lc, tm=tm)

        rw_pad = jnp.zeros((d, LANES), F32).at[:, :N_EXPERTS].set(router_w[li].astype(F32))
        rb_pad = jnp.full((1, LANES), NEG_BIG, F32).at[0, :N_EXPERTS].set(router_b[li].astype(F32))
        rw_hi = rw_pad.astype(BF16)
        rw_pad = jnp.concatenate([rw_hi, (rw_pad - rw_hi.astype(F32)).astype(BF16)], axis=1)
        x1, h2, topi, topg = _outproj(x, attn_p, attn_s, ys, cn, mod, bn, ssm_w_glu[li].astype(BF16), vec(ssm_b_glu[li]),
                                      w_out[li].astype(BF16), vec(norm_post_mix[li]), vec(norm_pre_ffn[li]),
                                      rw_pad, rb_pad, lc=lc, tm=tm)

        dest, block_e, n_used, fill_base, fill_n = _route(topi, blk)
        xs = _dispatch(h2, dest, fill_base, fill_n, cap=block_e.shape[0] * blk, tm=tm)
        ye = _experts(xs, block_e, n_used, expert_w1.astype(F32), expert_b1.astype(F32),
                      expert_w2.astype(F32), expert_b2.astype(F32), layer=li, blk=blk)
        if li + 1 < depth:
            x = _combine(ye, dest, topg, x1, mod, vec(norm_post_ffn[li]), lc=lc, tm=tm)
        else:
            out_p = _combine(ye, dest, topg, x1, mod, vec(norm_post_ffn[li]), lc=lc, tm=tm, seq0=0, n_seq=bsz)
            out_s = _combine(ye, dest, topg, x1, mod, vec(norm_post_ffn[li]), lc=lc, tm=tm, seq0=bsz,
                             n_seq=n_sample)

    return (out_p.reshape(bsz, lc, d), out_s.reshape(dbsz, dl, d))
```

```python
import functools
import math

import jax
import jax.numpy as jnp
from jax import lax
from jax.experimental import pallas as pl
from jax.experimental.pallas import tpu as pltpu

F32 = jnp.float32
BF16 = jnp.bfloat16
I32 = jnp.int32

HEAD_DIM = 64
N_HEADS = 8
N_KV_HEADS = 2
GQA_GROUP = N_HEADS // N_KV_HEADS
ATTN_W = N_HEADS * HEAD_DIM
KV_W = N_KV_HEADS * HEAD_DIM
SSM_GROUP = 16
SSM_STATE = 64
CONV_KERNEL = 31
CONV_PAD = CONV_KERNEL // 2
GRID_W = 64
ROPE_AXIS_DIM = HEAD_DIM // 2
ROPE_THETA = 10000.0
N_EXPERTS = 32
TOP_K = 4
SWIGLU_LIMIT = 7.0
SWIGLU_ALPHA = 1.702
EPS = 1e-6

LANES = 128
HALO = 16
V_ROWS = HEAD_DIM + 16
S5_CHUNK = 32
MOE_BLOCK = 512
EXPERT_COLS = 512
NEG_BIG = -1e30
VMEM_LIMIT = 48 * 1024 * 1024
EXPERT_VMEM_LIMIT = 56 * 1024 * 1024


def _cparams(sem, vmem_limit_bytes=VMEM_LIMIT, **kw):
    return pltpu.CompilerParams(dimension_semantics=sem, vmem_limit_bytes=vmem_limit_bytes, **kw)


def _rsqrt_mean_sq(x):
    return lax.rsqrt(jnp.mean(x * x, axis=-1, keepdims=True) + EPS)


def _mod_kernel(c_ref, w_ref, b_ref, o_ref):
    c = c_ref[...]
    s = c * jax.nn.sigmoid(c)
    o_ref[...] = jnp.dot(s, w_ref[...], preferred_element_type=F32,
                         precision=lax.Precision.HIGHEST) + b_ref[...]


def _modulation(c_all, w_ada, b_ada):
    s, d = c_all.shape
    n = w_ada.shape[1]
    tn = n // 4
    return pl.pallas_call(
        _mod_kernel,
        grid=(n // tn,),
        in_specs=[pl.BlockSpec((s, d), lambda j: (0, 0)),
                  pl.BlockSpec((d, tn), lambda j: (0, j)),
                  pl.BlockSpec((1, tn), lambda j: (0, j))],
        out_specs=pl.BlockSpec((s, tn), lambda j: (0, j)),
        out_shape=jax.ShapeDtypeStruct((s, n), F32),
        compiler_params=_cparams(("arbitrary",)),
        name="adaln_mod",
    )(c_all, w_ada, b_ada.reshape(1, n))


def _inproj_kernel(x_ref, mod_ref, npre_ref, w_ref, qkn_ref, cos_ref, sin_ref, bd_ref,
                   q_ref, k_ref, v_ref, u_ref, hc_ref):
    x = x_ref[...]
    sh = mod_ref[0, 0:1, :]
    sc = mod_ref[0, 1:2, :]
    h = x * _rsqrt_mean_sq(x) * npre_ref[...] * (1.0 + sc) + sh
    proj = jnp.dot(h.astype(BF16), w_ref[...], preferred_element_type=F32)
    tm = x.shape[0]
    cos = cos_ref[...]
    sin = sin_ref[...]
    lane = lax.broadcasted_iota(I32, (tm, LANES), 1)
    first_half = (lane % ROPE_AXIS_DIM) < (ROPE_AXIS_DIM // 2)
    n_qk = (ATTN_W + KV_W) // LANES
    for c in range(n_qk):
        xc = proj[:, c * LANES:(c + 1) * LANES]
        ms = jnp.dot((xc * xc).astype(BF16), bd_ref[...], preferred_element_type=F32) * (1.0 / HEAD_DIM)
        xn = xc * lax.rsqrt(ms + EPS) * qkn_ref[c:c + 1, :]
        half = ROPE_AXIS_DIM // 2
        swapped = jnp.where(first_half, pltpu.roll(xn, LANES - half, 1), pltpu.roll(xn, half, 1))
        rot = xn * cos + swapped * sin
        if c < ATTN_W // LANES:
            rot_t = rot.T.astype(BF16)
            q_ref[2 * c] = rot_t[:HEAD_DIM, :]
            q_ref[2 * c + 1] = rot_t[HEAD_DIM:, :]
        else:
            rot = rot.astype(BF16)
            k_ref[0] = rot[:, :HEAD_DIM]
            k_ref[1] = rot[:, HEAD_DIM:]
    o2 = ATTN_W + KV_W
    v_t = proj[:, o2:o2 + KV_W].T.astype(BF16)
    ones = jnp.ones((V_ROWS - HEAD_DIM, tm), BF16)
    for hh in range(N_KV_HEADS):
        v_ref[hh, 0, 0:HEAD_DIM, :] = v_t[hh * HEAD_DIM:(hh + 1) * HEAD_DIM, :]
        v_ref[hh, 0, HEAD_DIM:V_ROWS, :] = ones
    o3 = o2 + KV_W
    ssm_w = u_ref.shape[1]
    u_ref[...] = proj[:, o3:o3 + ssm_w].astype(BF16)
    o4 = o3 + ssm_w
    conv_w = hc_ref.shape[1]
    a = proj[:, o4:o4 + conv_w]
    gate = proj[:, o4 + conv_w:o4 + 2 * conv_w]
    hc_ref[...] = a * jax.nn.sigmoid(gate)


def _inproj(x, mod, norm_pre, w_in_bf, qkn, cos_t, sin_t, bd, *, n_prompt, lc, tm, ssm_w, conv_w):
    t, d = x.shape
    nb = lc // tm
    s = t // lc
    in_w = w_in_bf.shape[1]

    def row(si, i):
        return si * nb + i

    def pos(si, i):
        return jnp.where(si < n_prompt, i, (si - n_prompt) * nb + i)

    return pl.pallas_call(
        _inproj_kernel,
        grid=(s, nb),
        in_specs=[pl.BlockSpec((tm, d), lambda si, i: (row(si, i), 0)),
                  pl.BlockSpec((1, 6, d), lambda si, i: (si, 0, 0)),
                  pl.BlockSpec((1, d), lambda si, i: (0, 0)),
                  pl.BlockSpec((d, in_w), lambda si, i: (0, 0)),
                  pl.BlockSpec((8, LANES), lambda si, i: (0, 0)),
                  pl.BlockSpec((tm, LANES), lambda si, i: (pos(si, i), 0)),
                  pl.BlockSpec((tm, LANES), lambda si, i: (pos(si, i), 0)),
                  pl.BlockSpec((LANES, LANES), lambda si, i: (0, 0))],
        out_specs=[pl.BlockSpec((N_HEADS, HEAD_DIM, tm), lambda si, i: (0, 0, row(si, i))),
                   pl.BlockSpec((N_KV_HEADS, tm, HEAD_DIM), lambda si, i: (0, row(si, i), 0)),
                   pl.BlockSpec((N_KV_HEADS, 1, V_ROWS, tm), lambda si, i: (0, row(si, i), 0, 0)),
                   pl.BlockSpec((tm, ssm_w), lambda si, i: (row(si, i), 0)),
                   pl.BlockSpec((tm, conv_w), lambda si, i: (row(si, i), 0))],
        out_shape=[jax.ShapeDtypeStruct((N_HEADS, HEAD_DIM, t), BF16),
                   jax.ShapeDtypeStruct((N_KV_HEADS, t, HEAD_DIM), BF16),
                   jax.ShapeDtypeStruct((N_KV_HEADS, t // tm, V_ROWS, tm), BF16),
                   jax.ShapeDtypeStruct((t, ssm_w), BF16),
                   jax.ShapeDtypeStruct((t, conv_w), F32)],
        compiler_params=_cparams(("parallel", "arbitrary")),
        name="inproj",
    )(x, mod, norm_pre, w_in_bf, qkn, cos_t, sin_t, bd)


def _attn_kernel(q_ref, k_ref, v_ref, o_ref, s_ref, m_ref, acc_ref, *, tk):
    g, hd, tq = q_ref.shape
    q_t = jnp.concatenate([q_ref[i] for i in range(g)], axis=-1)
    n = k_ref.shape[1] // tk
    m_ref[...] = jnp.full(m_ref.shape, NEG_BIG, F32)
    acc_ref[...] = jnp.zeros(acc_ref.shape, F32)

    def scores(j):
        kc = k_ref[0, pl.ds(pl.multiple_of(j * tk, tk), tk), :]
        return jnp.dot(kc, q_t, preferred_element_type=F32)

    def step(j, slot, prefetch):
        if prefetch:
            s_ref[1 - slot] = scores(j + 1)
        s = s_ref[slot]
        m_prev = m_ref[...]
        m_next = jnp.maximum(m_prev, jnp.max(s, axis=0, keepdims=True))
        alpha = jnp.exp2(m_prev - m_next)
        p = jnp.exp2(s - m_next)
        acc_ref[...] = acc_ref[...] * alpha + jnp.dot(v_ref[0, j], p.astype(BF16), preferred_element_type=F32)
        m_ref[...] = m_next

    s_ref[0] = scores(0)
    n_pairs = (n - 1) // 2

    def body(jj, carry):
        step(2 * jj, 0, True)
        step(2 * jj + 1, 1, True)
        return carry

    lax.fori_loop(0, n_pairs, body, 0)
    if n - 2 * n_pairs == 2:
        step(2 * n_pairs, 0, True)
        step(2 * n_pairs + 1, 1, False)
    else:
        step(2 * n_pairs, 0, False)
    o_t = acc_ref[0:hd, :] * (1.0 / acc_ref[hd:hd + 1, :])
    o_ref[...] = jnp.concatenate([o_t[:, i * tq:(i + 1) * tq] for i in range(g)], axis=0).T.astype(o_ref.dtype)


def _attention(q_t, k, v_t, *, row0, n_seq, lkv, tq, tk):
    nq = lkv // tq
    qb0 = row0 // tq
    kb0 = row0 // lkv
    cols = GQA_GROUP * tq
    return pl.pallas_call(
        functools.partial(_attn_kernel, tk=tk),
        grid=(n_seq, N_KV_HEADS, nq),
        in_specs=[pl.BlockSpec((GQA_GROUP, HEAD_DIM, tq), lambda b, g, i: (g, 0, qb0 + b * nq + i)),
                  pl.BlockSpec((1, lkv, HEAD_DIM), lambda b, g, i: (g, kb0 + b, 0)),
                  pl.BlockSpec((1, lkv // tk, V_ROWS, tk), lambda b, g, i: (g, kb0 + b, 0, 0))],
        out_specs=pl.BlockSpec((tq, GQA_GROUP * HEAD_DIM), lambda b, g, i: (b * nq + i, g)),
        out_shape=jax.ShapeDtypeStruct((n_seq * lkv, ATTN_W), BF16),
        scratch_shapes=[pltpu.VMEM((2, tk, cols), F32), pltpu.VMEM((1, cols), F32),
                        pltpu.VMEM((V_ROWS, cols), F32)],
        compiler_params=_cparams(("parallel", "parallel", "arbitrary")),
        name="attention",
    )(q_t, k, v_t)


def _s5_weights(lam_re, lam_im, log_dt, b_re, b_im, c_re, c_im, d_skip, tc):
    hp = lax.Precision.HIGHEST
    lam = lax.complex(lam_re.astype(F32), lam_im.astype(F32))
    dt = jnp.exp(log_dt.astype(F32))[..., None]
    lam_dt = lam * dt
    lam_bar = jnp.exp(lam_dt)
    b_bar = ((lam_bar - 1.0) / lam)[..., None] * lax.complex(b_re.astype(F32), b_im.astype(F32))
    cmat = lax.complex(c_re.astype(F32), c_im.astype(F32))
    g, p = lam.shape[1], lam.shape[2]
    h = b_bar.shape[-1]
    taus = jnp.arange(tc + 1, dtype=F32)
    pw = jnp.exp(lam_dt[..., None] * taus)
    kern = jnp.einsum('dgop,dgpt,dgpi->dgtoi', cmat, pw, b_bar, precision=hp).real
    eye_h = jnp.eye(h, dtype=F32)
    kd = kern[0][:, 0] + kern[1][:, 0] + d_skip.astype(F32).reshape(g, h)[:, :, None] * eye_h
    by_lag = jnp.concatenate([jnp.flip(kern[1][:, 1:tc], axis=1), kd[:, None], kern[0][:, 1:tc]], axis=1)
    by_lag = by_lag.transpose(0, 2, 3, 1)
    padded = jnp.concatenate([by_lag, jnp.zeros((g, h, h, 1), F32)], axis=-1)
    skew = jnp.tile(padded, (1, 1, 1, tc))[..., :tc * (2 * tc - 1)].reshape(g, h, h, tc, 2 * tc - 1)
    full = skew[..., tc - 1:]
    toep = full.transpose(0, 3, 2, 4, 1).reshape(g, tc * h, tc * h)

    pf = jnp.flip(pw[0][:, :, :tc], axis=2)
    pb = pw[1][:, :, :tc]
    sf = pf[:, :, :, None] * b_bar[0][:, :, None, :]
    sb = pb[:, :, :, None] * b_bar[1][:, :, None, :]
    parts = [sf.real, sf.imag, sb.real, sb.imag]
    ws = jnp.stack([x.transpose(0, 2, 3, 1).reshape(g, tc * h, p) for x in parts], axis=2)
    odd = (jnp.arange(g) % 2)[:, None, None, None]
    zeros = jnp.zeros_like(ws)
    ws = jnp.where(odd == 0, jnp.concatenate([ws, zeros], -1), jnp.concatenate([zeros, ws], -1))

    ef = cmat[0].transpose(0, 2, 1)[:, :, None, :] * pw[0][:, :, 1:tc + 1][:, :, :, None]
    eb = cmat[1].transpose(0, 2, 1)[:, :, None, :] * jnp.flip(pw[1][:, :, 1:tc + 1], axis=2)[:, :, :, None]
    cparts = [ef.real, -ef.imag, eb.real, -eb.imag]
    wc = jnp.stack([x.reshape(g, p, tc * h) for x in cparts], axis=1)
    zc = jnp.zeros_like(wc)
    odd4 = (jnp.arange(g) % 2)[:, None, None, None]
    wc = jnp.where(odd4 == 0, jnp.concatenate([wc, zc], 2), jnp.concatenate([zc, wc], 2))
    wc = wc.reshape(g, 4 * 2 * p, tc * h)

    af = pw[0][:, :, tc].reshape(1, g * p)
    ab = pw[1][:, :, tc].reshape(1, g * p)
    decay = jnp.concatenate([af.real, af.imag, ab.real, ab.imag], axis=0)
    swap = lambda a: jnp.swapaxes(a, 1, 2).astype(BF16)
    return swap(toep), swap(ws.reshape(g, tc * h, 4 * 2 * p)), swap(wc), decay


def _s5_intra_kernel(x_ref, toep_ref, ws_ref, y_ref, s0_ref, s1_ref, s2_ref, s3_ref):
    tc, _, hh, tr = x_ref.shape
    xs = [x_ref[:, k].reshape(tc * hh, tr) for k in range(2)]
    for k in range(2):
        y_ref[:, k] = jnp.dot(toep_ref[k], xs[k], preferred_element_type=F32).reshape(tc, hh, tr)
    st = (jnp.dot(ws_ref[0], xs[0], preferred_element_type=F32)
          + jnp.dot(ws_ref[1], xs[1], preferred_element_type=F32))
    for k, ref in enumerate((s0_ref, s1_ref, s2_ref, s3_ref)):
        ref[...] = st[k * LANES:(k + 1) * LANES, :]


def _s5_intra(xt, toep_t, ws_t, *, tr):
    tc, g, hh, nc = xt.shape
    kw = tc * hh
    npair = g // 2
    st_shape = jax.ShapeDtypeStruct((npair * LANES, nc), F32)
    st_spec = pl.BlockSpec((LANES, tr), lambda pr, r: (pr, r))
    x_spec = pl.BlockSpec((tc, 2, hh, tr), lambda pr, r: (0, pr, 0, r))
    return pl.pallas_call(
        _s5_intra_kernel,
        grid=(npair, nc // tr),
        in_specs=[x_spec,
                  pl.BlockSpec((2, kw, kw), lambda pr, r: (pr, 0, 0)),
                  pl.BlockSpec((2, 4 * LANES, kw), lambda pr, r: (pr, 0, 0))],
        out_specs=[x_spec, st_spec, st_spec, st_spec, st_spec],
        out_shape=[jax.ShapeDtypeStruct((tc, g, hh, nc), F32), st_shape, st_shape, st_shape, st_shape],
        compiler_params=_cparams(("parallel", "arbitrary")),
        name="s5_intra",
    )(xt, toep_t, ws_t)


def _s5_scan_kernel(decay_ref, cfr_ref, cfi_ref, cbr_ref, cbi_ref, sfr_ref, sfi_ref, sbr_ref, sbi_ref):
    n = cfr_ref.shape[0]
    w = cfr_ref.shape[1]
    afr = decay_ref[0:1, :]
    afi = decay_ref[1:2, :]
    abr = decay_ref[2:3, :]
    abi = decay_ref[3:4, :]
    zero = jnp.zeros((1, w), F32)

    def fwd(c, st):
        re, im = st
        sfr_ref[pl.ds(c, 1), :] = re
        sfi_ref[pl.ds(c, 1), :] = im
        cr = cfr_ref[pl.ds(c, 1), :]
        ci = cfi_ref[pl.ds(c, 1), :]
        return afr * re - afi * im + cr, afr * im + afi * re + ci

    lax.fori_loop(0, n, fwd, (zero, zero))

    def bwd(kk, st):
        c = n - 1 - kk
        re, im = st
        sbr_ref[pl.ds(c, 1), :] = re
        sbi_ref[pl.ds(c, 1), :] = im
        cr = cbr_ref[pl.ds(c, 1), :]
        ci = cbi_ref[pl.ds(c, 1), :]
        return abr * re - abi * im + cr, abr * im + abi * re + ci

    lax.fori_loop(0, n, bwd, (zero, zero))


def _s5_scan(decay, contribs, *, chunk0, n_seq, cps):
    w = contribs[0].shape[1]
    b0 = chunk0 // cps
    spec_in = pl.BlockSpec((cps, w), lambda b: (b0 + b, 0))
    spec_out = pl.BlockSpec((cps, w), lambda b: (b, 0))
    shape = jax.ShapeDtypeStruct((n_seq * cps, w), F32)
    return pl.pallas_call(
        _s5_scan_kernel,
        grid=(n_seq,),
        in_specs=[pl.BlockSpec((4, w), lambda b: (0, 0))] + [spec_in] * 4,
        out_specs=[spec_out] * 4,
        out_shape=[shape] * 4,
        compiler_params=_cparams(("parallel",)),
        name="s5_scan",
    )(decay, *contribs)


def _s5_out_kernel(y_ref, s0_ref, s1_ref, s2_ref, s3_ref, wc_ref, o_ref):
    tc, _, hh, tr = y_ref.shape
    st = jnp.concatenate([s0_ref[...], s1_ref[...], s2_ref[...], s3_ref[...]], axis=0).astype(BF16)
    for k in range(2):
        y = y_ref[:, k] + jnp.dot(wc_ref[k], st, preferred_element_type=F32).reshape(tc, hh, tr)
        o_ref[:, k] = 0.5 * y * (1.0 + lax.erf(y * (2.0 ** -0.5)))


def _s5_out(y_t, states_t, wc_t, *, tr):
    tc, g, hh, nc = y_t.shape
    kw = tc * hh
    npair = g // 2
    st_spec = pl.BlockSpec((LANES, tr), lambda pr, r: (pr, r))
    y_spec = pl.BlockSpec((tc, 2, hh, tr), lambda pr, r: (0, pr, 0, r))
    return pl.pallas_call(
        _s5_out_kernel,
        grid=(npair, nc // tr),
        in_specs=[y_spec, st_spec, st_spec, st_spec, st_spec,
                  pl.BlockSpec((2, kw, 4 * LANES), lambda pr, r: (pr, 0, 0))],
        out_specs=y_spec,
        out_shape=jax.ShapeDtypeStruct((tc, g, hh, nc), F32),
        compiler_params=_cparams(("parallel", "arbitrary")),
        name="s5_out",
    )(y_t, *states_t, wc_t)


def _conv_kernel(prev_ref, cur_ref, next_ref, wdw_ref, bdw_ref, lng_ref, lnb_ref, wpw_ref, bn_ref, o_ref, scr,
                 *, n_prompt, n_seq, nb, rt):
    si = pl.program_id(0)
    i = pl.program_id(1)
    tm = cur_ref.shape[0]
    prev_ok = jnp.logical_or(i > 0, si > n_prompt)
    next_ok = jnp.logical_or(i < nb - 1, jnp.logical_and(si >= n_prompt, si < n_seq - 1))
    scr[0:HALO, :] = jnp.where(prev_ok, prev_ref[...], 0.0)
    scr[HALO:HALO + tm, :] = cur_ref[...]
    scr[HALO + tm:2 * HALO + tm, :] = jnp.where(next_ok, next_ref[...], 0.0)
    off = HALO - CONV_PAD
    sub = 8
    for r0 in range(0, tm, rt):
        slabs = [scr[r0 + sub * a:r0 + sub * a + rt + sub, :] for a in range((off + CONV_KERNEL - 1) // sub + 1)]
        acc = None
        for b in range(sub):
            part = None
            for a, slab in enumerate(slabs):
                kk = sub * a + b - off
                if 0 <= kk < CONV_KERNEL:
                    term = wdw_ref[kk:kk + 1, :] * slab
                    part = term if part is None else part + term
            part = part[b:b + rt, :]
            acc = part if acc is None else acc + part
        acc = acc + bdw_ref[...]
        mu = jnp.mean(acc, axis=-1, keepdims=True)
        cen = acc - mu
        var = jnp.mean(cen * cen, axis=-1, keepdims=True)
        y = cen * lax.rsqrt(var + EPS) * lng_ref[...] + lnb_ref[...]
        y = y * jax.nn.sigmoid(y)
        z = jnp.dot(y.astype(BF16), wpw_ref[...], preferred_element_type=F32)
        o_ref[r0:r0 + rt, :] = (z * _rsqrt_mean_sq(z) * bn_ref[...]).astype(o_ref.dtype)


def _conv_module(hc, w_dw, b_dw, ln_g, ln_b, w_pw_bf, bn_conv, *, n_prompt, lc, tm):
    t, cw = hc.shape
    nb = lc // tm
    s = t // lc
    hb = tm // HALO
    n_hblk = t // HALO
    kpad = 32

    def row(si, i):
        return si * nb + i

    wdw = jnp.zeros((kpad, cw), F32).at[:CONV_KERNEL].set(w_dw.astype(F32))
    vec = lambda a: a.astype(F32).reshape(1, cw)
    const = lambda shape: pl.BlockSpec(shape, lambda si, i: (0, 0))
    return pl.pallas_call(
        functools.partial(_conv_kernel, n_prompt=n_prompt, n_seq=s, nb=nb, rt=min(128, tm)),
        grid=(s, nb),
        in_specs=[pl.BlockSpec((HALO, cw), lambda si, i: (jnp.maximum(row(si, i) * hb - 1, 0), 0)),
                  pl.BlockSpec((tm, cw), lambda si, i: (row(si, i), 0)),
                  pl.BlockSpec((HALO, cw), lambda si, i: (jnp.minimum((row(si, i) + 1) * hb, n_hblk - 1), 0)),
                  const((kpad, cw)), const((1, cw)), const((1, cw)), const((1, cw)), const((cw, cw)),
                  const((1, cw))],
        out_specs=pl.BlockSpec((tm, cw), lambda si, i: (row(si, i), 0)),
        out_shape=jax.ShapeDtypeStruct((t, cw), BF16),
        scratch_shapes=[pltpu.VMEM((tm + 2 * HALO, cw), F32)],
        compiler_params=_cparams(("parallel", "arbitrary")),
        name="conv_module",
    )(hc, hc, hc, wdw, vec(b_dw), vec(ln_g), vec(ln_b), w_pw_bf, vec(bn_conv))


def _outproj_kernel(x_ref, attn_p_ref, attn_s_ref, ys_ref, cn_ref, mod_ref, bn_ref, wglu_ref, bglu_ref, wout_ref,
                    npost_ref, npre2_ref, rw_ref, rb_ref, x1_ref, h2_ref, ti_ref, tg_ref, *, n_prompt):
    aw = attn_p_ref.shape[1]
    sw = ys_ref.shape[1]
    a = jnp.where(pl.program_id(0) < n_prompt, attn_p_ref[...], attn_s_ref[...]).astype(F32)
    an = a * _rsqrt_mean_sq(a) * bn_ref[:, 0:aw]
    y = ys_ref[...]
    z = jnp.dot(y.astype(BF16), wglu_ref[...], preferred_element_type=F32) + bglu_ref[...]
    sm = y * jax.nn.sigmoid(z)
    sn = sm * _rsqrt_mean_sq(sm) * bn_ref[:, aw:aw + sw]
    mix = (jnp.dot(an.astype(BF16), wout_ref[0:aw, :], preferred_element_type=F32)
           + jnp.dot(sn.astype(BF16), wout_ref[aw:aw + sw, :], preferred_element_type=F32)
           + jnp.dot(cn_ref[...], wout_ref[aw + sw:, :], preferred_element_type=F32))
    g1 = mod_ref[0, 2:3, :]
    sh2 = mod_ref[0, 3:4, :]
    sc2 = mod_ref[0, 4:5, :]
    x1 = x_ref[...] + g1 * (mix * _rsqrt_mean_sq(mix) * npost_ref[...])
    x1_ref[...] = x1
    h2 = x1 * _rsqrt_mean_sq(x1) * npre2_ref[...] * (1.0 + sc2) + sh2
    h2_ref[...] = h2
    h_hi = h2.astype(BF16)
    h_lo = (h2 - h_hi.astype(F32)).astype(BF16)
    both = jnp.dot(h_hi, rw_ref[...], preferred_element_type=F32)
    logits = (both[:, :LANES] + both[:, LANES:]
              + jnp.dot(h_lo, rw_ref[:, :LANES], preferred_element_type=F32) + rb_ref[...])
    tm = logits.shape[0]
    lane = lax.broadcasted_iota(I32, (tm, LANES), 1)
    vals = []
    ti = jnp.zeros((tm, LANES), I32)
    for kk in range(TOP_K):
        m = jnp.max(logits, axis=1, keepdims=True)
        ik = jnp.min(jnp.where(logits == m, lane, LANES), axis=1, keepdims=True)
        vals.append(m)
        ti = jnp.where(lane == kk, ik, ti)
        logits = jnp.where(lane == ik, NEG_BIG * 2.0, logits)
    es = [jnp.exp(v - vals[0]) for v in vals]
    inv = 1.0 / (es[0] + es[1] + es[2] + es[3])
    tg = jnp.zeros((tm, LANES), F32)
    for kk in range(TOP_K):
        tg = jnp.where(lane == kk, es[kk] * inv, tg)
    ti_ref[...] = ti.T[0:8, :]
    tg_ref[...] = tg


def _outproj(x, attn_p, attn_s, ys, cn, mod, bn, wglu_bf, bglu, wout_bf, npost, npre2, rw_pad, rb_pad, *, lc, tm):
    t, d = x.shape
    nb = lc // tm
    s = t // lc
    aw, sw, cw = attn_p.shape[1], ys.shape[1], cn.shape[1]
    nbp = attn_p.shape[0] // tm
    n_prompt = nbp // nb

    def row(si, i):
        return si * nb + i

    rows = lambda w: pl.BlockSpec((tm, w), lambda si, i: (row(si, i), 0))
    const = lambda shape: pl.BlockSpec(shape, lambda si, i: (0, 0))
    return pl.pallas_call(
        functools.partial(_outproj_kernel, n_prompt=n_prompt),
        grid=(s, nb),
        in_specs=[rows(d),
                  pl.BlockSpec((tm, aw), lambda si, i: (jnp.minimum(row(si, i), nbp - 1), 0)),
                  pl.BlockSpec((tm, aw), lambda si, i: (jnp.maximum(row(si, i) - nbp, 0), 0)),
                  rows(sw), rows(cw),
                  pl.BlockSpec((1, 6, d), lambda si, i: (si, 0, 0)),
                  const((1, d)), const((sw, sw)), const((1, sw)), const((d, d)), const((1, d)), const((1, d)),
                  const((d, 2 * LANES)), const((1, LANES))],
        out_specs=[rows(d), rows(d), pl.BlockSpec((8, tm), lambda si, i: (0, row(si, i))), rows(LANES)],
        out_shape=[jax.ShapeDtypeStruct((t, d), F32), jax.ShapeDtypeStruct((t, d), F32),
                   jax.ShapeDtypeStruct((8, t), I32), jax.ShapeDtypeStruct((t, LANES), F32)],
        compiler_params=_cparams(("parallel", "arbitrary")),
        name="outproj_router",
    )(x, attn_p, attn_s, ys, cn, mod, bn, wglu_bf, bglu, wout_bf, npost, npre2, rw_pad, rb_pad)


def _route(topi, blk):
    t = topi.shape[1]
    n_assign = t * TOP_K
    flat_e = topi[:TOP_K].reshape(-1)
    eids = jnp.arange(N_EXPERTS, dtype=I32)
    onehot = (flat_e[:, None] == eids[None, :]).astype(I32)
    cb = math.gcd(n_assign, 512)
    tri = (jnp.arange(cb)[:, None] >= jnp.arange(cb)[None, :]).astype(BF16)
    within = jnp.einsum('ij,bjc->bic', tri, onehot.reshape(n_assign // cb, cb, N_EXPERTS).astype(BF16),
                        preferred_element_type=F32).astype(I32)
    block_tot = within[:, -1, :]
    csum = (within + (jnp.cumsum(block_tot, axis=0) - block_tot)[:, None, :]).reshape(n_assign, N_EXPERTS)
    counts = csum[-1]
    rank = jnp.sum((csum - onehot) * onehot, axis=1)
    padded = ((counts + blk - 1) // blk) * blk
    ends_p = jnp.cumsum(padded)
    starts_p = ends_p - padded
    dest = (jnp.sum(onehot * starts_p[None, :], axis=1) + rank).astype(I32)
    dest = jnp.pad(dest.reshape(TOP_K, t), ((0, 8 - TOP_K), (0, 0)))
    n_blocks = n_assign // blk + N_EXPERTS
    cap = n_blocks * blk
    n_used = (ends_p[-1] // blk).astype(I32)
    bidx = jnp.arange(n_blocks, dtype=I32)
    be_raw = jnp.sum((ends_p[None, :] <= (bidx * blk)[:, None]).astype(I32), axis=1)
    last_e = jnp.max(jnp.where(counts > 0, eids, 0))
    block_e = jnp.where(bidx < n_used, jnp.minimum(be_raw, N_EXPERTS - 1), last_e).astype(I32)
    fill_base = jnp.concatenate([starts_p + counts, ends_p[-1:]]).astype(I32)
    fill_n = jnp.concatenate([padded - counts, cap - ends_p[-1:]]).astype(I32)
    return dest, block_e, n_used.reshape(1), fill_base, fill_n


def _dispatch_kernel(fb_ref, fn_ref, dest_ref, h_ref, xs_hbm, zrow, sem, zsem):
    step = pl.program_id(0)
    tm = h_ref.shape[0]

    @pl.when(step == 0)
    def _():
        zrow[...] = jnp.zeros(zrow.shape, zrow.dtype)
        for e in range(N_EXPERTS + 1):
            base = fb_ref[e]
            n = fn_ref[e]

            def zissue(j, carry):
                pltpu.make_async_copy(zrow.at[pl.ds(0, 1)], xs_hbm.at[pl.ds(base + j, 1)], zsem).start()
                return carry

            def zwait(j, carry):
                pltpu.make_async_copy(zrow.at[pl.ds(0, 1)], xs_hbm.at[pl.ds(0, 1)], zsem).wait()
                return carry

            lax.fori_loop(0, n, zissue, 0)
            lax.fori_loop(0, n, zwait, 0)

    def issue(r, carry):
        for kk in range(TOP_K):
            pltpu.make_async_copy(h_ref.at[pl.ds(r, 1)],
                                  xs_hbm.at[pl.ds(dest_ref[kk, r], 1)], sem).start(priority=kk % 2)
        return carry

    lax.fori_loop(0, tm, issue, 0, unroll=4)
    for kk in range(TOP_K):
        pltpu.make_async_copy(h_ref, xs_hbm.at[pl.ds(0, tm)], sem).wait()


def _dispatch(h2, dest, fill_base, fill_n, *, cap, tm):
    t, d = h2.shape
    grid_spec = pltpu.PrefetchScalarGridSpec(
        num_scalar_prefetch=2,
        grid=(t // tm,),
        in_specs=[pl.BlockSpec((8, tm), lambda i, fb, fn: (0, i), memory_space=pltpu.SMEM),
                  pl.BlockSpec((tm, d), lambda i, fb, fn: (i, 0))],
        out_specs=pl.BlockSpec(memory_space=pl.ANY),
        scratch_shapes=[pltpu.VMEM((8, d), F32), pltpu.SemaphoreType.DMA(()), pltpu.SemaphoreType.DMA(())],
    )
    return pl.pallas_call(
        _dispatch_kernel,
        grid_spec=grid_spec,
        out_shape=jax.ShapeDtypeStruct((cap, d), F32),
        compiler_params=_cparams(("arbitrary",), disable_bounds_checks=True),
        name="moe_dispatch",
    )(fill_base, fill_n, dest, h2)


def _expert_kernel(be_ref, nu_ref, xs_ref, w1_ref, b1_ref, w2_ref, b2_ref, ys_ref, w1b, w2b):
    b = pl.program_id(0)
    de = w2_ref.shape[0]
    used = b < nu_ref[0]
    fresh = jnp.logical_or(b == 0, be_ref[b] != be_ref[jnp.maximum(b - 1, 0)])

    @pl.when(jnp.logical_and(used, fresh))
    def _():
        w1b[...] = w1_ref[...].astype(BF16)
        w2b[...] = w2_ref[...].astype(BF16)

    @pl.when(used)
    def _():
        x = xs_ref[...].astype(BF16)
        y = None
        for c0 in range(0, de, EXPERT_COLS):
            c1 = c0 + EXPERT_COLS
            gt = jnp.dot(x, w1b[:, c0:c1], preferred_element_type=F32) + b1_ref[:, c0:c1]
            up = jnp.dot(x, w1b[:, de + c0:de + c1], preferred_element_type=F32) + b1_ref[:, de + c0:de + c1]
            gt = jnp.minimum(gt, SWIGLU_LIMIT)
            up = jnp.clip(up, -SWIGLU_LIMIT, SWIGLU_LIMIT)
            act = (up + 1.0) * (gt * jax.nn.sigmoid(SWIGLU_ALPHA * gt))
            part = jnp.dot(act.astype(BF16), w2b[c0:c1, :], preferred_element_type=F32)
            y = part if y is None else y + part
        ys_ref[...] = y + b2_ref[...]

    @pl.when(jnp.logical_not(used))
    def _():
        ys_ref[...] = jnp.zeros(ys_ref.shape, ys_ref.dtype)


def _experts(xs, block_e, n_used, w1, b1, w2, b2, *, layer, blk):
    cap, d = xs.shape
    nl, ne, _, de2 = w1.shape
    de = w2.shape[2]
    nblk = cap // blk

    def xrow(b, be, nu):
        return (jnp.minimum(b, nu[0] - 1), 0)

    grid_spec = pltpu.PrefetchScalarGridSpec(
        num_scalar_prefetch=2,
        grid=(nblk,),
        in_specs=[pl.BlockSpec((blk, d), xrow),
                  pl.BlockSpec((None, None, d, de2), lambda b, be, nu: (layer, be[b], 0, 0)),
                  pl.BlockSpec((None, None, 1, de2), lambda b, be, nu: (layer, be[b], 0, 0)),
                  pl.BlockSpec((None, None, de, d), lambda b, be, nu: (layer, be[b], 0, 0)),
                  pl.BlockSpec((None, None, 1, d), lambda b, be, nu: (layer, be[b], 0, 0))],
        out_specs=pl.BlockSpec((blk, d), lambda b, be, nu: (b, 0)),
        scratch_shapes=[pltpu.VMEM((d, de2), BF16), pltpu.VMEM((de, d), BF16)],
    )
    return pl.pallas_call(
        _expert_kernel,
        grid_spec=grid_spec,
        out_shape=jax.ShapeDtypeStruct((cap, d), F32),
        compiler_params=_cparams(("arbitrary",), vmem_limit_bytes=EXPERT_VMEM_LIMIT),
        name="moe_experts",
    )(block_e, n_used, xs, w1, b1.reshape(nl, ne, 1, de2), w2, b2.reshape(nl, ne, 1, d))


def _combine_kernel(dest_ref, gate_ref, x1_ref, mod_ref, npost_ref, ys_hbm, o_ref, buf, sem):
    tm = x1_ref.shape[0]

    def issue(r, carry):
        for kk in range(TOP_K):
            pltpu.make_async_copy(ys_hbm.at[pl.ds(dest_ref[kk, r], 1)],
                                  buf.at[kk, pl.ds(r, 1)], sem).start(priority=kk % 2)
        return carry

    lax.fori_loop(0, tm, issue, 0, unroll=4)
    for kk in range(TOP_K):
        pltpu.make_async_copy(ys_hbm.at[pl.ds(0, tm)], buf.at[kk], sem).wait()
    gate = gate_ref[...]
    ff = gate[:, 0:1] * buf[0]
    for kk in range(1, TOP_K):
        ff = ff + gate[:, kk:kk + 1] * buf[kk]
    g2 = mod_ref[0, 5:6, :]
    o_ref[...] = x1_ref[...] + g2 * (ff * _rsqrt_mean_sq(ff) * npost_ref[...])


def _combine(ys, dest, gate, x1, mod, npost, *, lc, tm, seq0=0, n_seq=None):
    t, d = x1.shape
    nb = lc // tm
    s = t // lc - seq0 if n_seq is None else n_seq

    def row(si, i):
        return (seq0 + si) * nb + i

    return pl.pallas_call(
        _combine_kernel,
        grid=(s, nb),
        in_specs=[pl.BlockSpec((8, tm), lambda si, i: (0, row(si, i)), memory_space=pltpu.SMEM),
                  pl.BlockSpec((tm, LANES), lambda si, i: (row(si, i), 0)),
                  pl.BlockSpec((tm, d), lambda si, i: (row(si, i), 0)),
                  pl.BlockSpec((1, 6, d), lambda si, i: (seq0 + si, 0, 0)),
                  pl.BlockSpec((1, d), lambda si, i: (0, 0)),
                  pl.BlockSpec(memory_space=pl.ANY)],
        out_specs=pl.BlockSpec((tm, d), lambda si, i: (si * nb + i, 0)),
        out_shape=jax.ShapeDtypeStruct((s * lc, d), F32),
        scratch_shapes=[pltpu.VMEM((TOP_K, tm, d), F32), pltpu.SemaphoreType.DMA(())],
        compiler_params=_cparams(("arbitrary", "arbitrary"), disable_bounds_checks=True),
        name="moe_combine",
    )(dest, gate, x1, mod, npost, ys)


def _rope_tables(n_pos):
    rows = n_pos // GRID_W
    row_idx = jnp.repeat(jnp.arange(rows, dtype=F32), GRID_W)
    col_idx = jnp.tile(jnp.arange(GRID_W, dtype=F32), rows)
    inv = 1.0 / (ROPE_THETA ** (jnp.arange(0, ROPE_AXIS_DIM, 2, dtype=F32) / ROPE_AXIS_DIM))
    ar = row_idx[:, None] * inv[None, :]
    ac = col_idx[:, None] * inv[None, :]
    cos = jnp.concatenate([jnp.cos(ar), jnp.cos(ar), jnp.cos(ac), jnp.cos(ac)], axis=-1)
    sin = jnp.concatenate([-jnp.sin(ar), jnp.sin(ar), -jnp.sin(ac), jnp.sin(ac)], axis=-1)
    reps = LANES // HEAD_DIM
    return jnp.tile(cos, (1, reps)), jnp.tile(sin, (1, reps))


def _tiles(lc):
    tm = min(512, lc)
    tq = min(256, lc)
    return tm, tq


def kernel(x_prompt, x_sample, c_prompt, c_sample, w_ada, b_ada, norm_pre_mix, norm_post_mix, norm_pre_ffn, norm_post_ffn, w_in, q_norm, k_norm, ssm_lam_re, ssm_lam_im, ssm_log_dt, ssm_b_re, ssm_b_im, ssm_c_re, ssm_c_im, ssm_d, ssm_w_glu, ssm_b_glu, conv_w_dw, conv_b_dw, conv_ln_g, conv_ln_b, conv_w_pw, branch_norm, w_out, router_w, router_b, expert_w1, expert_b1, expert_w2, expert_b2):
    bsz, lc, d = x_prompt.shape
    dbsz, dl, _ = x_sample.shape
    assert dbsz == 1 and dl % lc == 0 and lc % S5_CHUNK == 0 and lc % GRID_W == 0
    n_sample = dl // lc
    n_seq = bsz + n_sample
    t_prompt = bsz * lc
    t = t_prompt + dl
    depth = w_ada.shape[0]
    ssm_w = ssm_d.shape[1]
    conv_w = conv_b_dw.shape[1]
    n_groups = ssm_w // SSM_GROUP
    tm, tq = _tiles(lc)
    tc = S5_CHUNK
    nc = t // tc
    cps = lc // tc
    kw = tc * SSM_GROUP
    assert t_prompt % dl == 0
    blk = min(MOE_BLOCK, t * TOP_K // N_EXPERTS)

    x = jnp.concatenate([x_prompt.reshape(t_prompt, d), x_sample.reshape(dl, d)], axis=0)
    c_all = jnp.concatenate([c_prompt, jnp.repeat(c_sample, n_sample, axis=0)], axis=0)
    cos_t, sin_t = _rope_tables(max(lc, dl))
    bd = (jnp.arange(LANES)[:, None] // HEAD_DIM == jnp.arange(LANES)[None, :] // HEAD_DIM).astype(BF16)
    reps = LANES // HEAD_DIM
    vec = lambda a: a.astype(F32).reshape(1, -1)
    s5_w = jax.vmap(lambda *p: _s5_weights(*p, tc))(ssm_lam_re, ssm_lam_im, ssm_log_dt, ssm_b_re, ssm_b_im,
                                                    ssm_c_re, ssm_c_im, ssm_d)

    for li in range(depth):
        mod = _modulation(c_all, w_ada[li], b_ada[li]).reshape(n_seq, 6, d)
        qkn = jnp.concatenate([jnp.tile(vec(q_norm[li]) * (HEAD_DIM ** -0.5 * math.log2(math.e)),
                                        (ATTN_W // LANES, reps)),
                               jnp.tile(vec(k_norm[li]), (1, reps)),
                               jnp.zeros((8 - ATTN_W // LANES - 1, LANES), F32)], axis=0)
        q_t, k, v_t, u, hc = _inproj(x, mod, vec(norm_pre_mix[li]), w_in[li].astype(BF16), qkn, cos_t, sin_t, bd,
                                     n_prompt=bsz, lc=lc, tm=tm, ssm_w=ssm_w, conv_w=conv_w)

        attn_p = _attention(q_t, k, v_t, row0=0, n_seq=bsz, lkv=lc, tq=min(2 * tq, lc), tk=tm)
        attn_s = _attention(q_t, k, v_t, row0=t_prompt, n_seq=1, lkv=dl, tq=tq, tk=tm)

        toep, ws, wc, decay = [w[li] for w in s5_w]
        xt = u.reshape(nc, tc * ssm_w).T.reshape(tc, n_groups, SSM_GROUP, nc)
        tr = nc if nc % LANES else math.gcd(nc, 768)
        y_t, *contribs_t = _s5_intra(xt, toep, ws, tr=tr)
        contribs = [a.T for a in contribs_t]
        st_p = _s5_scan(decay, contribs, chunk0=0, n_seq=bsz, cps=cps)
        st_s = _s5_scan(decay, contribs, chunk0=t_prompt // tc, n_seq=1, cps=dl // tc)
        states_t = [jnp.concatenate([a, b], axis=0).T for a, b in zip(st_p, st_s)]
        yg_t = _s5_out(y_t, states_t, wc, tr=tr)
        ys = yg_t.reshape(tc * ssm_w, nc).T.reshape(t, ssm_w)

        bn = vec(branch_norm[li])
        cn = _conv_module(hc, conv_w_dw[li], conv_b_dw[li], conv_ln_g[li], conv_ln_b[li],
                          conv_w_pw[li].astype(BF16), bn[0, ATTN_W + ssm_w:], n_prompt=bsz, lc=lc, tm=tm)

        rw_pad = jnp.zeros((d, LANES), F32).at[:, :N_EXPERTS].set(router_w[li].astype(F32))
        rb_pad = jnp.full((1, LANES), NEG_BIG, F32).at[0, :N_EXPERTS].set(router_b[li].astype(F32))
        rw_hi = rw_pad.astype(BF16)
        rw_pad = jnp.concatenate([rw_hi, (rw_pad - rw_hi.astype(F32)).astype(BF16)], axis=1)
        x1, h2, topi, topg = _outproj(x, attn_p, attn_s, ys, cn, mod, bn, ssm_w_glu[li].astype(BF16), vec(ssm_b_glu[li]),
                                      w_out[li].astype(BF16), vec(norm_post_mix[li]), vec(norm_pre_ffn[li]),
                                      rw_pad, rb_pad, lc=lc, tm=tm)

        dest, block_e, n_used, fill_base, fill_n = _route(topi, blk)
        xs = _dispatch(h2, dest, fill_base, fill_n, cap=block_e.shape[0] * blk, tm=tm)
        ye = _experts(xs, block_e, n_used, expert_w1.astype(F32), expert_b1.astype(F32),
                      expert_w2.astype(F32), expert_b2.astype(F32), layer=li, blk=blk)
        if li + 1 < depth:
            x = _combine(ye, dest, topg, x1, mod, vec(norm_post_ffn[li]), lc=lc, tm=tm)
        else:
            out_p = _combine(ye, dest, topg, x1, mod, vec(norm_post_ffn[li]), lc=lc, tm=tm, seq0=0, n_seq=bsz)
            out_s = _combine(ye, dest, topg, x1, mod, vec(norm_post_ffn[li]), lc=lc, tm=tm, seq0=bsz,
                             n_seq=n_sample)

    return (out_p.reshape(bsz, lc, d), out_s.reshape(dbsz, dl, d))
```

```python
import functools
import math

import jax
import jax.numpy as jnp
from jax import lax
from jax.experimental import pallas as pl
from jax.experimental.pallas import tpu as pltpu

F32 = jnp.float32
BF16 = jnp.bfloat16
I32 = jnp.int32

HEAD_DIM = 64
N_HEADS = 8
N_KV_HEADS = 2
GQA_GROUP = N_HEADS // N_KV_HEADS
ATTN_W = N_HEADS * HEAD_DIM
KV_W = N_KV_HEADS * HEAD_DIM
SSM_GROUP = 16
SSM_STATE = 64
CONV_KERNEL = 31
CONV_PAD = CONV_KERNEL // 2
GRID_W = 64
ROPE_AXIS_DIM = HEAD_DIM // 2
ROPE_THETA = 10000.0
N_EXPERTS = 32
TOP_K = 4
SWIGLU_LIMIT = 7.0
SWIGLU_ALPHA = 1.702
EPS = 1e-6

LANES = 128
HALO = 16
V_ROWS = HEAD_DIM + 16
S5_CHUNK = 32
MOE_BLOCK = 512
EXPERT_COLS = 512
NEG_BIG = -1e30
VMEM_LIMIT = 48 * 1024 * 1024
EXPERT_VMEM_LIMIT = 56 * 1024 * 1024


def _cparams(sem, vmem_limit_bytes=VMEM_LIMIT, **kw):
    return pltpu.CompilerParams(dimension_semantics=sem, vmem_limit_bytes=vmem_limit_bytes, **kw)


def _rsqrt_mean_sq(x):
    return lax.rsqrt(jnp.mean(x * x, axis=-1, keepdims=True) + EPS)


def _mod_kernel(c_ref, w_ref, b_ref, o_ref):
    c = c_ref[...]
    s = c * jax.nn.sigmoid(c)
    o_ref[...] = jnp.dot(s, w_ref[...], preferred_element_type=F32,
                         precision=lax.Precision.HIGHEST) + b_ref[...]


def _modulation(c_all, w_ada, b_ada):
    s, d = c_all.shape
    n = w_ada.shape[1]
    tn = n // 4
    return pl.pallas_call(
        _mod_kernel,
        grid=(n // tn,),
        in_specs=[pl.BlockSpec((s, d), lambda j: (0, 0)),
                  pl.BlockSpec((d, tn), lambda j: (0, j)),
                  pl.BlockSpec((1, tn), lambda j: (0, j))],
        out_specs=pl.BlockSpec((s, tn), lambda j: (0, j)),
        out_shape=jax.ShapeDtypeStruct((s, n), F32),
        compiler_params=_cparams(("arbitrary",)),
        name="adaln_mod",
    )(c_all, w_ada, b_ada.reshape(1, n))


def _inproj_kernel(x_ref, mod_ref, npre_ref, w_ref, qkn_ref, cos_ref, sin_ref, bd_ref,
                   q_ref, k_ref, v_ref, u_ref, hc_ref):
    x = x_ref[...]
    sh = mod_ref[0, 0:1, :]
    sc = mod_ref[0, 1:2, :]
    h = x * _rsqrt_mean_sq(x) * npre_ref[...] * (1.0 + sc) + sh
    proj = jnp.dot(h.astype(BF16), w_ref[...], preferred_element_type=F32)
    tm = x.shape[0]
    cos = cos_ref[...]
    sin = sin_ref[...]
    lane = lax.broadcasted_iota(I32, (tm, LANES), 1)
    first_half = (lane % ROPE_AXIS_DIM) < (ROPE_AXIS_DIM // 2)
    n_qk = (ATTN_W + KV_W) // LANES
    for c in range(n_qk):
        xc = proj[:, c * LANES:(c + 1) * LANES]
        ms = jnp.dot((xc * xc).astype(BF16), bd_ref[...], preferred_element_type=F32) * (1.0 / HEAD_DIM)
        xn = xc * lax.rsqrt(ms + EPS) * qkn_ref[c:c + 1, :]
        half = ROPE_AXIS_DIM // 2
        swapped = jnp.where(first_half, pltpu.roll(xn, LANES - half, 1), pltpu.roll(xn, half, 1))
        rot = xn * cos + swapped * sin
        if c < ATTN_W // LANES:
            rot_t = rot.T.astype(BF16)
            q_ref[2 * c] = rot_t[:HEAD_DIM, :]
            q_ref[2 * c + 1] = rot_t[HEAD_DIM:, :]
        else:
            rot = rot.astype(BF16)
            k_ref[0] = rot[:, :HEAD_DIM]
            k_ref[1] = rot[:, HEAD_DIM:]
    o2 = ATTN_W + KV_W
    v_t = proj[:, o2:o2 + KV_W].T.astype(BF16)
    ones = jnp.ones((V_ROWS - HEAD_DIM, tm), BF16)
    for hh in range(N_KV_HEADS):
        v_ref[hh, 0, 0:HEAD_DIM, :] = v_t[hh * HEAD_DIM:(hh + 1) * HEAD_DIM, :]
        v_ref[hh, 0, HEAD_DIM:V_ROWS, :] = ones
    o3 = o2 + KV_W
    ssm_w = u_ref.shape[1]
    u_ref[...] = proj[:, o3:o3 + ssm_w].astype(BF16)
    o4 = o3 + ssm_w
    conv_w = hc_ref.shape[1]
    a = proj[:, o4:o4 + conv_w]
    gate = proj[:, o4 + conv_w:o4 + 2 * conv_w]
    hc_ref[...] = a * jax.nn.sigmoid(gate)


def _inproj(x, mod, norm_pre, w_in_bf, qkn, cos_t, sin_t, bd, *, n_prompt, lc, tm, ssm_w, conv_w):
    t, d = x.shape
    nb = lc // tm
    s = t // lc
    in_w = w_in_bf.shape[1]

    def row(si, i):
        return si * nb + i

    def pos(si, i):
        return jnp.where(si < n_prompt, i, (si - n_prompt) * nb + i)

    return pl.pallas_call(
        _inproj_kernel,
        grid=(s, nb),
        in_specs=[pl.BlockSpec((tm, d), lambda si, i: (row(si, i), 0)),
                  pl.BlockSpec((1, 6, d), lambda si, i: (si, 0, 0)),
                  pl.BlockSpec((1, d), lambda si, i: (0, 0)),
                  pl.BlockSpec((d, in_w), lambda si, i: (0, 0)),
                  pl.BlockSpec((8, LANES), lambda si, i: (0, 0)),
                  pl.BlockSpec((tm, LANES), lambda si, i: (pos(si, i), 0)),
                  pl.BlockSpec((tm, LANES), lambda si, i: (pos(si, i), 0)),
                  pl.BlockSpec((LANES, LANES), lambda si, i: (0, 0))],
        out_specs=[pl.BlockSpec((N_HEADS, HEAD_DIM, tm), lambda si, i: (0, 0, row(si, i))),
                   pl.BlockSpec((N_KV_HEADS, tm, HEAD_DIM), lambda si, i: (0, row(si, i), 0)),
                   pl.BlockSpec((N_KV_HEADS, 1, V_ROWS, tm), lambda si, i: (0, row(si, i), 0, 0)),
                   pl.BlockSpec((tm, ssm_w), lambda si, i: (row(si, i), 0)),
                   pl.BlockSpec((tm, conv_w), lambda si, i: (row(si, i), 0))],
        out_shape=[jax.ShapeDtypeStruct((N_HEADS, HEAD_DIM, t), BF16),
                   jax.ShapeDtypeStruct((N_KV_HEADS, t, HEAD_DIM), BF16),
                   jax.ShapeDtypeStruct((N_KV_HEADS, t // tm, V_ROWS, tm), BF16),
                   jax.ShapeDtypeStruct((t, ssm_w), BF16),
                   jax.ShapeDtypeStruct((t, conv_w), F32)],
        compiler_params=_cparams(("parallel", "arbitrary")),
        name="inproj",
    )(x, mod, norm_pre, w_in_bf, qkn, cos_t, sin_t, bd)


def _attn_kernel(q_ref, k_ref, v_ref, o_ref, s_ref, m_ref, acc_ref, *, tk):
    g, hd, tq = q_ref.shape
    q_t = jnp.concatenate([q_ref[i] for i in range(g)], axis=-1)
    n = k_ref.shape[1] // tk
    m_ref[...] = jnp.full(m_ref.shape, NEG_BIG, F32)
    acc_ref[...] = jnp.zeros(acc_ref.shape, F32)

    def scores(j):
        kc = k_ref[0, pl.ds(pl.multiple_of(j * tk, tk), tk), :]
        return jnp.dot(kc, q_t, preferred_element_type=F32)

    def step(j, slot, prefetch):
        if prefetch:
            s_ref[1 - slot] = scores(j + 1)
        s = s_ref[slot]
        m_prev = m_ref[...]
        m_next = jnp.maximum(m_prev, jnp.max(s, axis=0, keepdims=True))
        alpha = jnp.exp2(m_prev - m_next)
        p = jnp.exp2(s - m_next)
        acc_ref[...] = acc_ref[...] * alpha + jnp.dot(v_ref[0, j], p.astype(BF16), preferred_element_type=F32)
        m_ref[...] = m_next

    s_ref[0] = scores(0)
    n_pairs = (n - 1) // 2

    def body(jj, carry):
        step(2 * jj, 0, True)
        step(2 * jj + 1, 1, True)
        return carry

    lax.fori_loop(0, n_pairs, body, 0)
    if n - 2 * n_pairs == 2:
        step(2 * n_pairs, 0, True)
        step(2 * n_pairs + 1, 1, False)
    else:
        step(2 * n_pairs, 0, False)
    o_t = acc_ref[0:hd, :] * (1.0 / acc_ref[hd:hd + 1, :])
    o_ref[...] = jnp.concatenate([o_t[:, i * tq:(i + 1) * tq] for i in range(g)], axis=0).T.astype(o_ref.dtype)


def _attention(q_t, k, v_t, *, row0, n_seq, lkv, tq, tk):
    nq = lkv // tq
    qb0 = row0 // tq
    kb0 = row0 // lkv
    cols = GQA_GROUP * tq
    return pl.pallas_call(
        functools.partial(_attn_kernel, tk=tk),
        grid=(n_seq, N_KV_HEADS, nq),
        in_specs=[pl.BlockSpec((GQA_GROUP, HEAD_DIM, tq), lambda b, g, i: (g, 0, qb0 + b * nq + i)),
                  pl.BlockSpec((1, lkv, HEAD_DIM), lambda b, g, i: (g, kb0 + b, 0)),
                  pl.BlockSpec((1, lkv // tk, V_ROWS, tk), lambda b, g, i: (g, kb0 + b, 0, 0))],
        out_specs=pl.BlockSpec((tq, GQA_GROUP * HEAD_DIM), lambda b, g, i: (b * nq + i, g)),
        out_shape=jax.ShapeDtypeStruct((n_seq * lkv, ATTN_W), BF16),
        scratch_shapes=[pltpu.VMEM((2, tk, cols), F32), pltpu.VMEM((1, cols), F32),
                        pltpu.VMEM((V_ROWS, cols), F32)],
        compiler_params=_cparams(("parallel", "parallel", "arbitrary")),
        name="attention",
    )(q_t, k, v_t)


def _s5_weights(lam_re, lam_im, log_dt, b_re, b_im, c_re, c_im, d_skip, tc):
    hp = lax.Precision.HIGHEST
    lam = lax.complex(lam_re.astype(F32), lam_im.astype(F32))
    dt = jnp.exp(log_dt.astype(F32))[..., None]
    lam_dt = lam * dt
    lam_bar = jnp.exp(lam_dt)
    b_bar = ((lam_bar - 1.0) / lam)[..., None] * lax.complex(b_re.astype(F32), b_im.astype(F32))
    cmat = lax.complex(c_re.astype(F32), c_im.astype(F32))
    g, p = lam.shape[1], lam.shape[2]
    h = b_bar.shape[-1]
    taus = jnp.arange(tc + 1, dtype=F32)
    pw = jnp.exp(lam_dt[..., None] * taus)
    kern = jnp.einsum('dgop,dgpt,dgpi->dgtoi', cmat, pw, b_bar, precision=hp).real
    eye_h = jnp.eye(h, dtype=F32)
    kd = kern[0][:, 0] + kern[1][:, 0] + d_skip.astype(F32).reshape(g, h)[:, :, None] * eye_h
    by_lag = jnp.concatenate([jnp.flip(kern[1][:, 1:tc], axis=1), kd[:, None], kern[0][:, 1:tc]], axis=1)
    by_lag = by_lag.transpose(0, 2, 3, 1)
    padded = jnp.concatenate([by_lag, jnp.zeros((g, h, h, 1), F32)], axis=-1)
    skew = jnp.tile(padded, (1, 1, 1, tc))[..., :tc * (2 * tc - 1)].reshape(g, h, h, tc, 2 * tc - 1)
    full = skew[..., tc - 1:]
    toep = full.transpose(0, 3, 2, 4, 1).reshape(g, tc * h, tc * h)

    pf = jnp.flip(pw[0][:, :, :tc], axis=2)
    pb = pw[1][:, :, :tc]
    sf = pf[:, :, :, None] * b_bar[0][:, :, None, :]
    sb = pb[:, :, :, None] * b_bar[1][:, :, None, :]
    parts = [sf.real, sf.imag, sb.real, sb.imag]
    ws = jnp.stack([x.transpose(0, 2, 3, 1).reshape(g, tc * h, p) for x in parts], axis=2)
    odd = (jnp.arange(g) % 2)[:, None, None, None]
    zeros = jnp.zeros_like(ws)
    ws = jnp.where(odd == 0, jnp.concatenate([ws, zeros], -1), jnp.concatenate([zeros, ws], -1))

    ef = cmat[0].transpose(0, 2, 1)[:, :, None, :] * pw[0][:, :, 1:tc + 1][:, :, :, None]
    eb = cmat[1].transpose(0, 2, 1)[:, :, None, :] * jnp.flip(pw[1][:, :, 1:tc + 1], axis=2)[:, :, :, None]
    cparts = [ef.real, -ef.imag, eb.real, -eb.imag]
    wc = jnp.stack([x.reshape(g, p, tc * h) for x in cparts], axis=1)
    zc = jnp.zeros_like(wc)
    odd4 = (jnp.arange(g) % 2)[:, None, None, None]
    wc = jnp.where(odd4 == 0, jnp.concatenate([wc, zc], 2), jnp.concatenate([zc, wc], 2))
    wc = wc.reshape(g, 4 * 2 * p, tc * h)

    af = pw[0][:, :, tc].reshape(1, g * p)
    ab = pw[1][:, :, tc].reshape(1, g * p)
    decay = jnp.concatenate([af.real, af.imag, ab.real, ab.imag], axis=0)
    swap = lambda a: jnp.swapaxes(a, 1, 2).astype(BF16)
    return swap(toep), swap(ws.reshape(g, tc * h, 4 * 2 * p)), swap(wc), decay


def _s5_intra_kernel(x_ref, toep_ref, ws_ref, y_ref, s0_ref, s1_ref, s2_ref, s3_ref):
    tc, _, hh, tr = x_ref.shape
    xs = [x_ref[:, k].reshape(tc * hh, tr) for k in range(2)]
    for k in range(2):
        y_ref[:, k] = jnp.dot(toep_ref[k], xs[k], preferred_element_type=F32).reshape(tc, hh, tr)
    st = (jnp.dot(ws_ref[0], xs[0], preferred_element_type=F32)
          + jnp.dot(ws_ref[1], xs[1], preferred_element_type=F32))
    for k, ref in enumerate((s0_ref, s1_ref, s2_ref, s3_ref)):
        ref[...] = st[k * LANES:(k + 1) * LANES, :]


def _s5_intra(xt, toep_t, ws_t, *, tr):
    tc, g, hh, nc = xt.shape
    kw = tc * hh
    npair = g // 2
    st_shape = jax.ShapeDtypeStruct((npair * LANES, nc), F32)
    st_spec = pl.BlockSpec((LANES, tr), lambda pr, r: (pr, r))
    x_spec = pl.BlockSpec((tc, 2, hh, tr), lambda pr, r: (0, pr, 0, r))
    return pl.pallas_call(
        _s5_intra_kernel,
        grid=(npair, nc // tr),
        in_specs=[x_spec,
                  pl.BlockSpec((2, kw, kw), lambda pr, r: (pr, 0, 0)),
                  pl.BlockSpec((2, 4 * LANES, kw), lambda pr, r: (pr, 0, 0))],
        out_specs=[x_spec, st_spec, st_spec, st_spec, st_spec],
        out_shape=[jax.ShapeDtypeStruct((tc, g, hh, nc), F32), st_shape, st_shape, st_shape, st_shape],
        compiler_params=_cparams(("parallel", "arbitrary")),
        name="s5_intra",
    )(xt, toep_t, ws_t)


def _s5_scan_kernel(decay_ref, cfr_ref, cfi_ref, cbr_ref, cbi_ref, sfr_ref, sfi_ref, sbr_ref, sbi_ref):
    n = cfr_ref.shape[0]
    w = cfr_ref.shape[1]
    afr = decay_ref[0:1, :]
    afi = decay_ref[1:2, :]
    abr = decay_ref[2:3, :]
    abi = decay_ref[3:4, :]
    zero = jnp.zeros((1, w), F32)

    def fwd(c, st):
        re, im = st
        sfr_ref[pl.ds(c, 1), :] = re
        sfi_ref[pl.ds(c, 1), :] = im
        cr = cfr_ref[pl.ds(c, 1), :]
        ci = cfi_ref[pl.ds(c, 1), :]
        return afr * re - afi * im + cr, afr * im + afi * re + ci

    lax.fori_loop(0, n, fwd, (zero, zero))

    def bwd(kk, st):
        c = n - 1 - kk
        re, im = st
        sbr_ref[pl.ds(c, 1), :] = re
        sbi_ref[pl.ds(c, 1), :] = im
        cr = cbr_ref[pl.ds(c, 1), :]
        ci = cbi_ref[pl.ds(c, 1), :]
        return abr * re - abi * im + cr, abr * im + abi * re + ci

    lax.fori_loop(0, n, bwd, (zero, zero))


def _s5_scan(decay, contribs, *, chunk0, n_seq, cps):
    w = contribs[0].shape[1]
    b0 = chunk0 // cps
    spec_in = pl.BlockSpec((cps, w), lambda b: (b0 + b, 0))
    spec_out = pl.BlockSpec((cps, w), lambda b: (b, 0))
    shape = jax.ShapeDtypeStruct((n_seq * cps, w), F32)
    return pl.pallas_call(
        _s5_scan_kernel,
        grid=(n_seq,),
        in_specs=[pl.BlockSpec((4, w), lambda b: (0, 0))] + [spec_in] * 4,
        out_specs=[spec_out] * 4,
        out_shape=[shape] * 4,
        compiler_params=_cparams(("parallel",)),
        name="s5_scan",
    )(decay, *contribs)


def _s5_out_kernel(y_ref, s0_ref, s1_ref, s2_ref, s3_ref, wc_ref, o_ref):
    tc, _, hh, tr = y_ref.shape
    st = jnp.concatenate([s0_ref[...], s1_ref[...], s2_ref[...], s3_ref[...]], axis=0).astype(BF16)
    for k in range(2):
        y = y_ref[:, k] + jnp.dot(wc_ref[k], st, preferred_element_type=F32).reshape(tc, hh, tr)
        o_ref[:, k] = 0.5 * y * (1.0 + lax.erf(y * (2.0 ** -0.5)))


def _s5_out(y_t, states_t, wc_t, *, tr):
    tc, g, hh, nc = y_t.shape
    kw = tc * hh
    npair = g // 2
    st_spec = pl.BlockSpec((LANES, tr), lambda pr, r: (pr, r))
    y_spec = pl.BlockSpec((tc, 2, hh, tr), lambda pr, r: (0, pr, 0, r))
    return pl.pallas_call(
        _s5_out_kernel,
        grid=(npair, nc // tr),
        in_specs=[y_spec, st_spec, st_spec, st_spec, st_spec,
                  pl.BlockSpec((2, kw, 4 * LANES), lambda pr, r: (pr, 0, 0))],
        out_specs=y_spec,
        out_shape=jax.ShapeDtypeStruct((tc, g, hh, nc), F32),
        compiler_params=_cparams(("parallel", "arbitrary")),
        name="s5_out",
    )(y_t, *states_t, wc_t)


def _conv_kernel(prev_ref, cur_ref, next_ref, wdw_ref, bdw_ref, lng_ref, lnb_ref, wpw_ref, bn_ref, o_ref, scr,
                 *, n_prompt, n_seq, nb, rt):
    si = pl.program_id(0)
    i = pl.program_id(1)
    tm = cur_ref.shape[0]
    prev_ok = jnp.logical_or(i > 0, si > n_prompt)
    next_ok = jnp.logical_or(i < nb - 1, jnp.logical_and(si >= n_prompt, si < n_seq - 1))
    scr[0:HALO, :] = jnp.where(prev_ok, prev_ref[...], 0.0)
    scr[HALO:HALO + tm, :] = cur_ref[...]
    scr[HALO + tm:2 * HALO + tm, :] = jnp.where(next_ok, next_ref[...], 0.0)
    off = HALO - CONV_PAD
    sub = 8
    for r0 in range(0, tm, rt):
        slabs = [scr[r0 + sub * a:r0 + sub * a + rt + sub, :] for a in range((off + CONV_KERNEL - 1) // sub + 1)]
        acc = None
        for b in range(sub):
            part = None
            for a, slab in enumerate(slabs):
                kk = sub * a + b - off
                if 0 <= kk < CONV_KERNEL:
                    term = wdw_ref[kk:kk + 1, :] * slab
                    part = term if part is None else part + term
            part = part[b:b + rt, :]
            acc = part if acc is None else acc + part
        acc = acc + bdw_ref[...]
        mu = jnp.mean(acc, axis=-1, keepdims=True)
        cen = acc - mu
        var = jnp.mean(cen * cen, axis=-1, keepdims=True)
        y = cen * lax.rsqrt(var + EPS) * lng_ref[...] + lnb_ref[...]
        y = y * jax.nn.sigmoid(y)
        z = jnp.dot(y.astype(BF16), wpw_ref[...], preferred_element_type=F32)
        o_ref[r0:r0 + rt, :] = (z * _rsqrt_mean_sq(z) * bn_ref[...]).astype(o_ref.dtype)


def _conv_module(hc, w_dw, b_dw, ln_g, ln_b, w_pw_bf, bn_conv, *, n_prompt, lc, tm):
    t, cw = hc.shape
    nb = lc // tm
    s = t // lc
    hb = tm // HALO
    n_hblk = t // HALO
    kpad = 32

    def row(si, i):
        return si * nb + i

    wdw = jnp.zeros((kpad, cw), F32).at[:CONV_KERNEL].set(w_dw.astype(F32))
    vec = lambda a: a.astype(F32).reshape(1, cw)
    const = lambda shape: pl.BlockSpec(shape, lambda si, i: (0, 0))
    return pl.pallas_call(
        functools.partial(_conv_kernel, n_prompt=n_prompt, n_seq=s, nb=nb, rt=min(128, tm)),
        grid=(s, nb),
        in_specs=[pl.BlockSpec((HALO, cw), lambda si, i: (jnp.maximum(row(si, i) * hb - 1, 0), 0)),
                  pl.BlockSpec((tm, cw), lambda si, i: (row(si, i), 0)),
                  pl.BlockSpec((HALO, cw), lambda si, i: (jnp.minimum((row(si, i) + 1) * hb, n_hblk - 1), 0)),
                  const((kpad, cw)), const((1, cw)), const((1, cw)), const((1, cw)), const((cw, cw)),
                  const((1, cw))],
        out_specs=pl.BlockSpec((tm, cw), lambda si, i: (row(si, i), 0)),
        out_shape=jax.ShapeDtypeStruct((t, cw), BF16),
        scratch_shapes=[pltpu.VMEM((tm + 2 * HALO, cw), F32)],
        compiler_params=_cparams(("parallel", "arbitrary")),
        name="conv_module",
    )(hc, hc, hc, wdw, vec(b_dw), vec(ln_g), vec(ln_b), w_pw_bf, vec(bn_conv))


def _outproj_kernel(x_ref, attn_p_ref, attn_s_ref, ys_ref, perm_ref, cn_ref, mod_ref, bn_ref, wglu_ref, bglu_ref,
                    wout_ref, npost_ref, npre2_ref, rw_ref, rb_ref, x1_ref, h2_ref, ti_ref, tg_ref,
                    *, n_prompt, tc):
    aw = attn_p_ref.shape[1]
    sw = wglu_ref.shape[0]
    a = jnp.where(pl.program_id(0) < n_prompt, attn_p_ref[...], attn_s_ref[...]).astype(F32)
    an = a * _rsqrt_mean_sq(a) * bn_ref[:, 0:aw]
    blk = ys_ref[...]
    y = jnp.concatenate([blk[:, i * sw:(i + 1) * sw] for i in range(tc)], axis=0)
    z = jnp.dot(y.astype(BF16), wglu_ref[...], preferred_element_type=F32) + bglu_ref[...]
    sm = y * jax.nn.sigmoid(z)
    sn = sm * _rsqrt_mean_sq(sm) * bn_ref[:, aw:aw + sw]
    sn = jnp.dot(perm_ref[...], sn.astype(BF16), preferred_element_type=F32)
    mix = (jnp.dot(an.astype(BF16), wout_ref[0:aw, :], preferred_element_type=F32)
           + jnp.dot(sn.astype(BF16), wout_ref[aw:aw + sw, :], preferred_element_type=F32)
           + jnp.dot(cn_ref[...], wout_ref[aw + sw:, :], preferred_element_type=F32))
    g1 = mod_ref[0, 2:3, :]
    sh2 = mod_ref[0, 3:4, :]
    sc2 = mod_ref[0, 4:5, :]
    x1 = x_ref[...] + g1 * (mix * _rsqrt_mean_sq(mix) * npost_ref[...])
    x1_ref[...] = x1
    h2 = x1 * _rsqrt_mean_sq(x1) * npre2_ref[...] * (1.0 + sc2) + sh2
    h2_ref[...] = h2
    h_hi = h2.astype(BF16)
    h_lo = (h2 - h_hi.astype(F32)).astype(BF16)
    both = jnp.dot(h_hi, rw_ref[...], preferred_element_type=F32)
    logits = (both[:, :LANES] + both[:, LANES:]
              + jnp.dot(h_lo, rw_ref[:, :LANES], preferred_element_type=F32) + rb_ref[...])
    tm = logits.shape[0]
    lane = lax.broadcasted_iota(I32, (tm, LANES), 1)
    vals = []
    ti = jnp.zeros((tm, LANES), I32)
    for kk in range(TOP_K):
        m = jnp.max(logits, axis=1, keepdims=True)
        ik = jnp.min(jnp.where(logits == m, lane, LANES), axis=1, keepdims=True)
        vals.append(m)
        ti = jnp.where(lane == kk, ik, ti)
        logits = jnp.where(lane == ik, NEG_BIG * 2.0, logits)
    es = [jnp.exp(v - vals[0]) for v in vals]
    inv = 1.0 / (es[0] + es[1] + es[2] + es[3])
    tg = jnp.zeros((tm, LANES), F32)
    for kk in range(TOP_K):
        tg = jnp.where(lane == kk, es[kk] * inv, tg)
    ti_ref[...] = ti.T[0:8, :]
    tg_ref[...] = tg


def _outproj(x, attn_p, attn_s, ys_c, cn, mod, bn, wglu_bf, bglu, wout_bf, npost, npre2, rw_pad, rb_pad,
             *, lc, tm, tc):
    t, d = x.shape
    nb = lc // tm
    s = t // lc
    aw, sw, cw = attn_p.shape[1], ys_c.shape[1] // tc, cn.shape[1]
    nbp = attn_p.shape[0] // tm
    n_prompt = nbp // nb
    cpt = tm // tc
    r = jnp.arange(tm)
    perm = ((r % tc) * cpt + r // tc)[:, None] == jnp.arange(tm)[None, :]

    def row(si, i):
        return si * nb + i

    rows = lambda w: pl.BlockSpec((tm, w), lambda si, i: (row(si, i), 0))
    const = lambda shape: pl.BlockSpec(shape, lambda si, i: (0, 0))
    return pl.pallas_call(
        functools.partial(_outproj_kernel, n_prompt=n_prompt, tc=tc),
        grid=(s, nb),
        in_specs=[rows(d),
                  pl.BlockSpec((tm, aw), lambda si, i: (jnp.minimum(row(si, i), nbp - 1), 0)),
                  pl.BlockSpec((tm, aw), lambda si, i: (jnp.maximum(row(si, i) - nbp, 0), 0)),
                  pl.BlockSpec((cpt, tc * sw), lambda si, i: (row(si, i), 0)), const((tm, tm)), rows(cw),
                  pl.BlockSpec((1, 6, d), lambda si, i: (si, 0, 0)),
                  const((1, d)), const((sw, sw)), const((1, sw)), const((d, d)), const((1, d)), const((1, d)),
                  const((d, 2 * LANES)), const((1, LANES))],
        out_specs=[rows(d), rows(d), pl.BlockSpec((8, tm), lambda si, i: (0, row(si, i))), rows(LANES)],
        out_shape=[jax.ShapeDtypeStruct((t, d), F32), jax.ShapeDtypeStruct((t, d), F32),
                   jax.ShapeDtypeStruct((8, t), I32), jax.ShapeDtypeStruct((t, LANES), F32)],
        compiler_params=_cparams(("parallel", "arbitrary")),
        name="outproj_router",
    )(x, attn_p, attn_s, ys_c, perm.astype(BF16), cn, mod, bn, wglu_bf, bglu, wout_bf, npost, npre2, rw_pad, rb_pad)


def _route(topi, blk):
    t = topi.shape[1]
    n_assign = t * TOP_K
    flat_e = topi[:TOP_K].reshape(-1)
    eids = jnp.arange(N_EXPERTS, dtype=I32)
    onehot = (flat_e[:, None] == eids[None, :]).astype(I32)
    cb = math.gcd(n_assign, 512)
    tri = (jnp.arange(cb)[:, None] >= jnp.arange(cb)[None, :]).astype(BF16)
    within = jnp.einsum('ij,bjc->bic', tri, onehot.reshape(n_assign // cb, cb, N_EXPERTS).astype(BF16),
                        preferred_element_type=F32).astype(I32)
    block_tot = within[:, -1, :]
    csum = (within + (jnp.cumsum(block_tot, axis=0) - block_tot)[:, None, :]).reshape(n_assign, N_EXPERTS)
    counts = csum[-1]
    rank = jnp.sum((csum - onehot) * onehot, axis=1)
    padded = ((counts + blk - 1) // blk) * blk
    ends_p = jnp.cumsum(padded)
    starts_p = ends_p - padded
    dest = (jnp.sum(onehot * starts_p[None, :], axis=1) + rank).astype(I32)
    dest = jnp.pad(dest.reshape(TOP_K, t), ((0, 8 - TOP_K), (0, 0)))
    n_blocks = n_assign // blk + N_EXPERTS
    cap = n_blocks * blk
    n_used = (ends_p[-1] // blk).astype(I32)
    bidx = jnp.arange(n_blocks, dtype=I32)
    be_raw = jnp.sum((ends_p[None, :] <= (bidx * blk)[:, None]).astype(I32), axis=1)
    last_e = jnp.max(jnp.where(counts > 0, eids, 0))
    block_e = jnp.where(bidx < n_used, jnp.minimum(be_raw, N_EXPERTS - 1), last_e).astype(I32)
    fill_base = jnp.concatenate([starts_p + counts, ends_p[-1:]]).astype(I32)
    fill_n = jnp.concatenate([padded - counts, cap - ends_p[-1:]]).astype(I32)
    return dest, block_e, n_used.reshape(1), fill_base, fill_n


def _dispatch_kernel(fb_ref, fn_ref, dest_ref, h_ref, xs_hbm, zrow, sem, zsem):
    step = pl.program_id(0)
    tm = h_ref.shape[0]

    @pl.when(step == 0)
    def _():
        zrow[...] = jnp.zeros(zrow.shape, zrow.dtype)
        for e in range(N_EXPERTS + 1):
            base = fb_ref[e]
            n = fn_ref[e]

            def zissue(j, carry):
                pltpu.make_async_copy(zrow.at[pl.ds(0, 1)], xs_hbm.at[pl.ds(base + j, 1)], zsem).start()
                return carry

            def zwait(j, carry):
                pltpu.make_async_copy(zrow.at[pl.ds(0, 1)], xs_hbm.at[pl.ds(0, 1)], zsem).wait()
                return carry

            lax.fori_loop(0, n, zissue, 0)
            lax.fori_loop(0, n, zwait, 0)

    def issue(r, carry):
        for kk in range(TOP_K):
            pltpu.make_async_copy(h_ref.at[pl.ds(r, 1)],
                                  xs_hbm.at[pl.ds(dest_ref[kk, r], 1)], sem).start(priority=kk % 2)
        return carry

    lax.fori_loop(0, tm, issue, 0, unroll=4)
    for kk in range(TOP_K):
        pltpu.make_async_copy(h_ref, xs_hbm.at[pl.ds(0, tm)], sem).wait()


def _dispatch(h2, dest, fill_base, fill_n, *, cap, tm):
    t, d = h2.shape
    grid_spec = pltpu.PrefetchScalarGridSpec(
        num_scalar_prefetch=2,
        grid=(t // tm,),
        in_specs=[pl.BlockSpec((8, tm), lambda i, fb, fn: (0, i), memory_space=pltpu.SMEM),
                  pl.BlockSpec((tm, d), lambda i, fb, fn: (i, 0))],
        out_specs=pl.BlockSpec(memory_space=pl.ANY),
        scratch_shapes=[pltpu.VMEM((8, d), F32), pltpu.SemaphoreType.DMA(()), pltpu.SemaphoreType.DMA(())],
    )
    return pl.pallas_call(
        _dispatch_kernel,
        grid_spec=grid_spec,
        out_shape=jax.ShapeDtypeStruct((cap, d), F32),
        compiler_params=_cparams(("arbitrary",), disable_bounds_checks=True),
        name="moe_dispatch",
    )(fill_base, fill_n, dest, h2)


def _expert_kernel(be_ref, nu_ref, xs_ref, w1_ref, b1_ref, w2_ref, b2_ref, ys_ref, w1b, w2b):
    b = pl.program_id(0)
    de = w2_ref.shape[0]
    used = b < nu_ref[0]
    fresh = jnp.logical_or(b == 0, be_ref[b] != be_ref[jnp.maximum(b - 1, 0)])

    @pl.when(jnp.logical_and(used, fresh))
    def _():
        w1b[...] = w1_ref[...].astype(BF16)
        w2b[...] = w2_ref[...].astype(BF16)

    @pl.when(used)
    def _():
        x = xs_ref[...].astype(BF16)
        y = None
        for c0 in range(0, de, EXPERT_COLS):
            c1 = c0 + EXPERT_COLS
            gt = jnp.dot(x, w1b[:, c0:c1], preferred_element_type=F32) + b1_ref[:, c0:c1]
            up = jnp.dot(x, w1b[:, de + c0:de + c1], preferred_element_type=F32) + b1_ref[:, de + c0:de + c1]
            gt = jnp.minimum(gt, SWIGLU_LIMIT)
            up = jnp.clip(up, -SWIGLU_LIMIT, SWIGLU_LIMIT)
            act = (up + 1.0) * (gt * jax.nn.sigmoid(SWIGLU_ALPHA * gt))
            part = jnp.dot(act.astype(BF16), w2b[c0:c1, :], preferred_element_type=F32)
            y = part if y is None else y + part
        ys_ref[...] = y + b2_ref[...]

    @pl.when(jnp.logical_not(used))
    def _():
        ys_ref[...] = jnp.zeros(ys_ref.shape, ys_ref.dtype)


def _experts(xs, block_e, n_used, w1, b1, w2, b2, *, layer, blk):
    cap, d = xs.shape
    nl, ne, _, de2 = w1.shape
    de = w2.shape[2]
    nblk = cap // blk

    def xrow(b, be, nu):
        return (jnp.minimum(b, nu[0] - 1), 0)

    grid_spec = pltpu.PrefetchScalarGridSpec(
        num_scalar_prefetch=2,
        grid=(nblk,),
        in_specs=[pl.BlockSpec((blk, d), xrow),
                  pl.BlockSpec((None, None, d, de2), lambda b, be, nu: (layer, be[b], 0, 0)),
                  pl.BlockSpec((None, None, 1, de2), lambda b, be, nu: (layer, be[b], 0, 0)),
                  pl.BlockSpec((None, None, de, d), lambda b, be, nu: (layer, be[b], 0, 0)),
                  pl.BlockSpec((None, None, 1, d), lambda b, be, nu: (layer, be[b], 0, 0))],
        out_specs=pl.BlockSpec((blk, d), lambda b, be, nu: (b, 0)),
        scratch_shapes=[pltpu.VMEM((d, de2), BF16), pltpu.VMEM((de, d), BF16)],
    )
    return pl.pallas_call(
        _expert_kernel,
        grid_spec=grid_spec,
        out_shape=jax.ShapeDtypeStruct((cap, d), F32),
        compiler_params=_cparams(("arbitrary",), vmem_limit_bytes=EXPERT_VMEM_LIMIT),
        name="moe_experts",
    )(block_e, n_used, xs, w1, b1.reshape(nl, ne, 1, de2), w2, b2.reshape(nl, ne, 1, d))


def _combine_kernel(dest_ref, gate_ref, x1_ref, mod_ref, npost_ref, ys_hbm, o_ref, buf, sem):
    tm = x1_ref.shape[0]

    def issue(r, carry):
        for kk in range(TOP_K):
            pltpu.make_async_copy(ys_hbm.at[pl.ds(dest_ref[kk, r], 1)],
                                  buf.at[kk, pl.ds(r, 1)], sem).start(priority=kk % 2)
        return carry

    lax.fori_loop(0, tm, issue, 0, unroll=4)
    for kk in range(TOP_K):
        pltpu.make_async_copy(ys_hbm.at[pl.ds(0, tm)], buf.at[kk], sem).wait()
    gate = gate_ref[...]
    ff = gate[:, 0:1] * buf[0]
    for kk in range(1, TOP_K):
        ff = ff + gate[:, kk:kk + 1] * buf[kk]
    g2 = mod_ref[0, 5:6, :]
    o_ref[...] = x1_ref[...] + g2 * (ff * _rsqrt_mean_sq(ff) * npost_ref[...])


def _combine(ys, dest, gate, x1, mod, npost, *, lc, tm, seq0=0, n_seq=None):
    t, d = x1.shape
    nb = lc // tm
    s = t // lc - seq0 if n_seq is None else n_seq

    def row(si, i):
        return (seq0 + si) * nb + i

    return pl.pallas_call(
        _combine_kernel,
        grid=(s, nb),
        in_specs=[pl.BlockSpec((8, tm), lambda si, i: (0, row(si, i)), memory_space=pltpu.SMEM),
                  pl.BlockSpec((tm, LANES), lambda si, i: (row(si, i), 0)),
                  pl.BlockSpec((tm, d), lambda si, i: (row(si, i), 0)),
                  pl.BlockSpec((1, 6, d), lambda si, i: (seq0 + si, 0, 0)),
                  pl.BlockSpec((1, d), lambda si, i: (0, 0)),
                  pl.BlockSpec(memory_space=pl.ANY)],
        out_specs=pl.BlockSpec((tm, d), lambda si, i: (si * nb + i, 0)),
        out_shape=jax.ShapeDtypeStruct((s * lc, d), F32),
        scratch_shapes=[pltpu.VMEM((TOP_K, tm, d), F32), pltpu.SemaphoreType.DMA(())],
        compiler_params=_cparams(("arbitrary", "arbitrary"), disable_bounds_checks=True),
        name="moe_combine",
    )(dest, gate, x1, mod, npost, ys)


def _rope_tables(n_pos):
    rows = n_pos // GRID_W
    row_idx = jnp.repeat(jnp.arange(rows, dtype=F32), GRID_W)
    col_idx = jnp.tile(jnp.arange(GRID_W, dtype=F32), rows)
    inv = 1.0 / (ROPE_THETA ** (jnp.arange(0, ROPE_AXIS_DIM, 2, dtype=F32) / ROPE_AXIS_DIM))
    ar = row_idx[:, None] * inv[None, :]
    ac = col_idx[:, None] * inv[None, :]
    cos = jnp.concatenate([jnp.cos(ar), jnp.cos(ar), jnp.cos(ac), jnp.cos(ac)], axis=-1)
    sin = jnp.concatenate([-jnp.sin(ar), jnp.sin(ar), -jnp.sin(ac), jnp.sin(ac)], axis=-1)
    reps = LANES // HEAD_DIM
    return jnp.tile(cos, (1, reps)), jnp.tile(sin, (1, reps))


def _tiles(lc):
    tm = min(512, lc)
    tq = min(256, lc)
    return tm, tq


def kernel(x_prompt, x_sample, c_prompt, c_sample, w_ada, b_ada, norm_pre_mix, norm_post_mix, norm_pre_ffn, norm_post_ffn, w_in, q_norm, k_norm, ssm_lam_re, ssm_lam_im, ssm_log_dt, ssm_b_re, ssm_b_im, ssm_c_re, ssm_c_im, ssm_d, ssm_w_glu, ssm_b_glu, conv_w_dw, conv_b_dw, conv_ln_g, conv_ln_b, conv_w_pw, branch_norm, w_out, router_w, router_b, expert_w1, expert_b1, expert_w2, expert_b2):
    bsz, lc, d = x_prompt.shape
    dbsz, dl, _ = x_sample.shape
    assert dbsz == 1 and dl % lc == 0 and lc % S5_CHUNK == 0 and lc % GRID_W == 0
    n_sample = dl // lc
    n_seq = bsz + n_sample
    t_prompt = bsz * lc
    t = t_prompt + dl
    depth = w_ada.shape[0]
    ssm_w = ssm_d.shape[1]
    conv_w = conv_b_dw.shape[1]
    n_groups = ssm_w // SSM_GROUP
    tm, tq = _tiles(lc)
    tc = S5_CHUNK
    nc = t // tc
    cps = lc // tc
    kw = tc * SSM_GROUP
    assert t_prompt % dl == 0
    blk = min(MOE_BLOCK, t * TOP_K // N_EXPERTS)

    x = jnp.concatenate([x_prompt.reshape(t_prompt, d), x_sample.reshape(dl, d)], axis=0)
    c_all = jnp.concatenate([c_prompt, jnp.repeat(c_sample, n_sample, axis=0)], axis=0)
    cos_t, sin_t = _rope_tables(max(lc, dl))
    bd = (jnp.arange(LANES)[:, None] // HEAD_DIM == jnp.arange(LANES)[None, :] // HEAD_DIM).astype(BF16)
    reps = LANES // HEAD_DIM
    vec = lambda a: a.astype(F32).reshape(1, -1)
    s5_w = jax.vmap(lambda *p: _s5_weights(*p, tc))(ssm_lam_re, ssm_lam_im, ssm_log_dt, ssm_b_re, ssm_b_im,
                                                    ssm_c_re, ssm_c_im, ssm_d)

    for li in range(depth):
        mod = _modulation(c_all, w_ada[li], b_ada[li]).reshape(n_seq, 6, d)
        qkn = jnp.concatenate([jnp.tile(vec(q_norm[li]) * (HEAD_DIM ** -0.5 * math.log2(math.e)),
                                        (ATTN_W // LANES, reps)),
                               jnp.tile(vec(k_norm[li]), (1, reps)),
                               jnp.zeros((8 - ATTN_W // LANES - 1, LANES), F32)], axis=0)
        q_t, k, v_t, u, hc = _inproj(x, mod, vec(norm_pre_mix[li]), w_in[li].astype(BF16), qkn, cos_t, sin_t, bd,
                                     n_prompt=bsz, lc=lc, tm=tm, ssm_w=ssm_w, conv_w=conv_w)

        attn_p = _attention(q_t, k, v_t, row0=0, n_seq=bsz, lkv=lc, tq=min(2 * tq, lc), tk=tm)
        attn_s = _attention(q_t, k, v_t, row0=t_prompt, n_seq=1, lkv=dl, tq=tq, tk=tm)

        toep, ws, wc, decay = [w[li] for w in s5_w]
        xt = u.reshape(nc, tc * ssm_w).T.reshape(tc, n_groups, SSM_GROUP, nc)
        tr = nc if nc % LANES else math.gcd(nc, 768)
        y_t, *contribs_t = _s5_intra(xt, toep, ws, tr=tr)
        contribs = [a.T for a in contribs_t]
        st_p = _s5_scan(decay, contribs, chunk0=0, n_seq=bsz, cps=cps)
        st_s = _s5_scan(decay, contribs, chunk0=t_prompt // tc, n_seq=1, cps=dl // tc)
        states_t = [jnp.concatenate([a, b], axis=0).T for a, b in zip(st_p, st_s)]
        yg_t = _s5_out(y_t, states_t, wc, tr=tr)
        ys_c = yg_t.reshape(tc * ssm_w, nc).T

        bn = vec(branch_norm[li])
        cn = _conv_module(hc, conv_w_dw[li], conv_b_dw[li], conv_ln_g[li], conv_ln_b[li],
                          conv_w_pw[li].astype(BF16), bn[0, ATTN_W + ssm_w:], n_prompt=bsz, lc=lc, tm=tm)

        rw_pad = jnp.zeros((d, LANES), F32).at[:, :N_EXPERTS].set(router_w[li].astype(F32))
        rb_pad = jnp.full((1, LANES), NEG_BIG, F32).at[0, :N_EXPERTS].set(router_b[li].astype(F32))
        rw_hi = rw_pad.astype(BF16)
        rw_pad = jnp.concatenate([rw_hi, (rw_pad - rw_hi.astype(F32)).astype(BF16)], axis=1)
        x1, h2, topi, topg = _outproj(x, attn_p, attn_s, ys_c, cn, mod, bn, ssm_w_glu[li].astype(BF16),
                                      vec(ssm_b_glu[li]), w_out[li].astype(BF16), vec(norm_post_mix[li]),
                                      vec(norm_pre_ffn[li]), rw_pad, rb_pad, lc=lc, tm=tm, tc=tc)

        dest, block_e, n_used, fill_base, fill_n = _route(topi, blk)
        xs = _dispatch(h2, dest, fill_base, fill_n, cap=block_e.shape[0] * blk, tm=tm)
        ye = _experts(xs, block_e, n_used, expert_w1.astype(F32), expert_b1.astype(F32),
                      expert_w2.astype(F32), expert_b2.astype(F32), layer=li, blk=blk)
        if li + 1 < depth:
            x = _combine(ye, dest, topg, x1, mod, vec(norm_post_ffn[li]), lc=lc, tm=tm)
        else:
            out_p = _combine(ye, dest, topg, x1, mod, vec(norm_post_ffn[li]), lc=lc, tm=tm, seq0=0, n_seq=bsz)
            out_s = _combine(ye, dest, topg, x1, mod, vec(norm_post_ffn[li]), lc=lc, tm=tm, seq0=bsz,
                             n_seq=n_sample)

    return (out_p.reshape(bsz, lc, d), out_s.reshape(dbsz, dl, d))
```

```python
import functools
import math

import jax
import jax.numpy as jnp
from jax import lax
from jax.experimental import pallas as pl
from jax.experimental.pallas import tpu as pltpu

F32 = jnp.float32
BF16 = jnp.bfloat16
I32 = jnp.int32

HEAD_DIM = 64
N_HEADS = 8
N_KV_HEADS = 2
GQA_GROUP = N_HEADS // N_KV_HEADS
ATTN_W = N_HEADS * HEAD_DIM
KV_W = N_KV_HEADS * HEAD_DIM
SSM_GROUP = 16
SSM_STATE = 64
CONV_KERNEL = 31
CONV_PAD = CONV_KERNEL // 2
GRID_W = 64
ROPE_AXIS_DIM = HEAD_DIM // 2
ROPE_THETA = 10000.0
N_EXPERTS = 32
TOP_K = 4
SWIGLU_LIMIT = 7.0
SWIGLU_ALPHA = 1.702
EPS = 1e-6

LANES = 128
HALO = 16
V_ROWS = HEAD_DIM + 16
S5_CHUNK = 32
MOE_BLOCK = 512
EXPERT_COLS = 512
NEG_BIG = -1e30
VMEM_LIMIT = 48 * 1024 * 1024
EXPERT_VMEM_LIMIT = 56 * 1024 * 1024


def _cparams(sem, vmem_limit_bytes=VMEM_LIMIT, **kw):
    return pltpu.CompilerParams(dimension_semantics=sem, vmem_limit_bytes=vmem_limit_bytes, **kw)


def _rsqrt_mean_sq(x):
    return lax.rsqrt(jnp.mean(x * x, axis=-1, keepdims=True) + EPS)


def _mod_kernel(c_ref, w_ref, b_ref, o_ref):
    c = c_ref[...]
    s = c * jax.nn.sigmoid(c)
    o_ref[...] = jnp.dot(s, w_ref[...], preferred_element_type=F32,
                         precision=lax.Precision.HIGHEST) + b_ref[...]


def _modulation(c_all, w_ada, b_ada):
    s, d = c_all.shape
    n = w_ada.shape[1]
    tn = n // 4
    return pl.pallas_call(
        _mod_kernel,
        grid=(n // tn,),
        in_specs=[pl.BlockSpec((s, d), lambda j: (0, 0)),
                  pl.BlockSpec((d, tn), lambda j: (0, j)),
                  pl.BlockSpec((1, tn), lambda j: (0, j))],
        out_specs=pl.BlockSpec((s, tn), lambda j: (0, j)),
        out_shape=jax.ShapeDtypeStruct((s, n), F32),
        compiler_params=_cparams(("arbitrary",)),
        name="adaln_mod",
    )(c_all, w_ada, b_ada.reshape(1, n))


def _inproj_kernel(x_ref, mod_ref, npre_ref, w_ref, qkn_ref, cos_ref, sin_ref, bd_ref,
                   q_ref, k_ref, v_ref, u_ref, hc_ref):
    x = x_ref[...]
    sh = mod_ref[0, 0:1, :]
    sc = mod_ref[0, 1:2, :]
    h = x * _rsqrt_mean_sq(x) * npre_ref[...] * (1.0 + sc) + sh
    proj = jnp.dot(h.astype(BF16), w_ref[...], preferred_element_type=F32)
    tm = x.shape[0]
    cos = cos_ref[...]
    sin = sin_ref[...]
    lane = lax.broadcasted_iota(I32, (tm, LANES), 1)
    first_half = (lane % ROPE_AXIS_DIM) < (ROPE_AXIS_DIM // 2)
    n_qk = (ATTN_W + KV_W) // LANES
    for c in range(n_qk):
        xc = proj[:, c * LANES:(c + 1) * LANES]
        ms = jnp.dot((xc * xc).astype(BF16), bd_ref[...], preferred_element_type=F32) * (1.0 / HEAD_DIM)
        xn = xc * lax.rsqrt(ms + EPS) * qkn_ref[c:c + 1, :]
        half = ROPE_AXIS_DIM // 2
        swapped = jnp.where(first_half, pltpu.roll(xn, LANES - half, 1), pltpu.roll(xn, half, 1))
        rot = xn * cos + swapped * sin
        if c < ATTN_W // LANES:
            rot_t = rot.T.astype(BF16)
            q_ref[2 * c] = rot_t[:HEAD_DIM, :]
            q_ref[2 * c + 1] = rot_t[HEAD_DIM:, :]
        else:
            rot = rot.astype(BF16)
            k_ref[0] = rot[:, :HEAD_DIM]
            k_ref[1] = rot[:, HEAD_DIM:]
    o2 = ATTN_W + KV_W
    v_t = proj[:, o2:o2 + KV_W].T.astype(BF16)
    ones = jnp.ones((V_ROWS - HEAD_DIM, tm), BF16)
    for hh in range(N_KV_HEADS):
        v_ref[hh, 0, 0:HEAD_DIM, :] = v_t[hh * HEAD_DIM:(hh + 1) * HEAD_DIM, :]
        v_ref[hh, 0, HEAD_DIM:V_ROWS, :] = ones
    o3 = o2 + KV_W
    ssm_w = u_ref.shape[1]
    u_ref[...] = proj[:, o3:o3 + ssm_w].astype(BF16)
    o4 = o3 + ssm_w
    conv_w = hc_ref.shape[1]
    a = proj[:, o4:o4 + conv_w]
    gate = proj[:, o4 + conv_w:o4 + 2 * conv_w]
    hc_ref[...] = a * jax.nn.sigmoid(gate)


def _inproj(x, mod, norm_pre, w_in_bf, qkn, cos_t, sin_t, bd, *, n_prompt, lc, tm, ssm_w, conv_w):
    t, d = x.shape
    nb = lc // tm
    s = t // lc
    in_w = w_in_bf.shape[1]

    def row(si, i):
        return si * nb + i

    def pos(si, i):
        return jnp.where(si < n_prompt, i, (si - n_prompt) * nb + i)

    return pl.pallas_call(
        _inproj_kernel,
        grid=(s, nb),
        in_specs=[pl.BlockSpec((tm, d), lambda si, i: (row(si, i), 0)),
                  pl.BlockSpec((1, 6, d), lambda si, i: (si, 0, 0)),
                  pl.BlockSpec((1, d), lambda si, i: (0, 0)),
                  pl.BlockSpec((d, in_w), lambda si, i: (0, 0)),
                  pl.BlockSpec((8, LANES), lambda si, i: (0, 0)),
                  pl.BlockSpec((tm, LANES), lambda si, i: (pos(si, i), 0)),
                  pl.BlockSpec((tm, LANES), lambda si, i: (pos(si, i), 0)),
                  pl.BlockSpec((LANES, LANES), lambda si, i: (0, 0))],
        out_specs=[pl.BlockSpec((N_HEADS, HEAD_DIM, tm), lambda si, i: (0, 0, row(si, i))),
                   pl.BlockSpec((N_KV_HEADS, tm, HEAD_DIM), lambda si, i: (0, row(si, i), 0)),
                   pl.BlockSpec((N_KV_HEADS, 1, V_ROWS, tm), lambda si, i: (0, row(si, i), 0, 0)),
                   pl.BlockSpec((tm, ssm_w), lambda si, i: (row(si, i), 0)),
                   pl.BlockSpec((tm, conv_w), lambda si, i: (row(si, i), 0))],
        out_shape=[jax.ShapeDtypeStruct((N_HEADS, HEAD_DIM, t), BF16),
                   jax.ShapeDtypeStruct((N_KV_HEADS, t, HEAD_DIM), BF16),
                   jax.ShapeDtypeStruct((N_KV_HEADS, t // tm, V_ROWS, tm), BF16),
                   jax.ShapeDtypeStruct((t, ssm_w), BF16),
                   jax.ShapeDtypeStruct((t, conv_w), F32)],
        compiler_params=_cparams(("parallel", "arbitrary")),
        name="inproj",
    )(x, mod, norm_pre, w_in_bf, qkn, cos_t, sin_t, bd)


def _attn_kernel(q_ref, k_ref, v_ref, o_ref, s_ref, m_ref, acc_ref, *, tk):
    g, hd, tq = q_ref.shape
    q_t = jnp.concatenate([q_ref[i] for i in range(g)], axis=-1)
    n = k_ref.shape[1] // tk
    m_ref[...] = jnp.full(m_ref.shape, NEG_BIG, F32)
    acc_ref[...] = jnp.zeros(acc_ref.shape, F32)

    def scores(j):
        kc = k_ref[0, pl.ds(pl.multiple_of(j * tk, tk), tk), :]
        return jnp.dot(kc, q_t, preferred_element_type=F32)

    def step(j, slot, prefetch):
        if prefetch:
            s_ref[1 - slot] = scores(j + 1)
        s = s_ref[slot]
        m_prev = m_ref[...]
        m_next = jnp.maximum(m_prev, jnp.max(s, axis=0, keepdims=True))
        alpha = jnp.exp2(m_prev - m_next)
        p = jnp.exp2(s - m_next)
        acc_ref[...] = acc_ref[...] * alpha + jnp.dot(v_ref[0, j], p.astype(BF16), preferred_element_type=F32)
        m_ref[...] = m_next

    s_ref[0] = scores(0)
    n_pairs = (n - 1) // 2

    def body(jj, carry):
        step(2 * jj, 0, True)
        step(2 * jj + 1, 1, True)
        return carry

    lax.fori_loop(0, n_pairs, body, 0)
    if n - 2 * n_pairs == 2:
        step(2 * n_pairs, 0, True)
        step(2 * n_pairs + 1, 1, False)
    else:
        step(2 * n_pairs, 0, False)
    o_t = acc_ref[0:hd, :] * (1.0 / acc_ref[hd:hd + 1, :])
    o_ref[...] = jnp.concatenate([o_t[:, i * tq:(i + 1) * tq] for i in range(g)], axis=0).T.astype(o_ref.dtype)


def _attention(q_t, k, v_t, *, row0, n_seq, lkv, tq, tk):
    nq = lkv // tq
    qb0 = row0 // tq
    kb0 = row0 // lkv
    cols = GQA_GROUP * tq
    return pl.pallas_call(
        functools.partial(_attn_kernel, tk=tk),
        grid=(n_seq, N_KV_HEADS, nq),
        in_specs=[pl.BlockSpec((GQA_GROUP, HEAD_DIM, tq), lambda b, g, i: (g, 0, qb0 + b * nq + i)),
                  pl.BlockSpec((1, lkv, HEAD_DIM), lambda b, g, i: (g, kb0 + b, 0)),
                  pl.BlockSpec((1, lkv // tk, V_ROWS, tk), lambda b, g, i: (g, kb0 + b, 0, 0))],
        out_specs=pl.BlockSpec((tq, GQA_GROUP * HEAD_DIM), lambda b, g, i: (b * nq + i, g)),
        out_shape=jax.ShapeDtypeStruct((n_seq * lkv, ATTN_W), BF16),
        scratch_shapes=[pltpu.VMEM((2, tk, cols), F32), pltpu.VMEM((1, cols), F32),
                        pltpu.VMEM((V_ROWS, cols), F32)],
        compiler_params=_cparams(("parallel", "parallel", "arbitrary")),
        name="attention",
    )(q_t, k, v_t)


def _s5_weights(lam_re, lam_im, log_dt, b_re, b_im, c_re, c_im, d_skip, tc):
    hp = lax.Precision.HIGHEST
    lam = lax.complex(lam_re.astype(F32), lam_im.astype(F32))
    dt = jnp.exp(log_dt.astype(F32))[..., None]
    lam_dt = lam * dt
    lam_bar = jnp.exp(lam_dt)
    b_bar = ((lam_bar - 1.0) / lam)[..., None] * lax.complex(b_re.astype(F32), b_im.astype(F32))
    cmat = lax.complex(c_re.astype(F32), c_im.astype(F32))
    g, p = lam.shape[1], lam.shape[2]
    h = b_bar.shape[-1]
    taus = jnp.arange(tc + 1, dtype=F32)
    pw = jnp.exp(lam_dt[..., None] * taus)
    kern = jnp.einsum('dgop,dgpt,dgpi->dgtoi', cmat, pw, b_bar, precision=hp).real
    eye_h = jnp.eye(h, dtype=F32)
    kd = kern[0][:, 0] + kern[1][:, 0] + d_skip.astype(F32).reshape(g, h)[:, :, None] * eye_h
    by_lag = jnp.concatenate([jnp.flip(kern[1][:, 1:tc], axis=1), kd[:, None], kern[0][:, 1:tc]], axis=1)
    by_lag = by_lag.transpose(0, 2, 3, 1)
    padded = jnp.concatenate([by_lag, jnp.zeros((g, h, h, 1), F32)], axis=-1)
    skew = jnp.tile(padded, (1, 1, 1, tc))[..., :tc * (2 * tc - 1)].reshape(g, h, h, tc, 2 * tc - 1)
    full = skew[..., tc - 1:]
    toep = full.transpose(0, 3, 2, 4, 1).reshape(g, tc * h, tc * h)

    pf = jnp.flip(pw[0][:, :, :tc], axis=2)
    pb = pw[1][:, :, :tc]
    sf = pf[:, :, :, None] * b_bar[0][:, :, None, :]
    sb = pb[:, :, :, None] * b_bar[1][:, :, None, :]
    parts = [sf.real, sf.imag, sb.real, sb.imag]
    ws = jnp.stack([x.transpose(0, 2, 3, 1).reshape(g, tc * h, p) for x in parts], axis=2)
    odd = (jnp.arange(g) % 2)[:, None, None, None]
    zeros = jnp.zeros_like(ws)
    ws = jnp.where(odd == 0, jnp.concatenate([ws, zeros], -1), jnp.concatenate([zeros, ws], -1))

    ef = cmat[0].transpose(0, 2, 1)[:, :, None, :] * pw[0][:, :, 1:tc + 1][:, :, :, None]
    eb = cmat[1].transpose(0, 2, 1)[:, :, None, :] * jnp.flip(pw[1][:, :, 1:tc + 1], axis=2)[:, :, :, None]
    cparts = [ef.real, -ef.imag, eb.real, -eb.imag]
    wc = jnp.stack([x.reshape(g, p, tc * h) for x in cparts], axis=1)
    zc = jnp.zeros_like(wc)
    odd4 = (jnp.arange(g) % 2)[:, None, None, None]
    wc = jnp.where(odd4 == 0, jnp.concatenate([wc, zc], 2), jnp.concatenate([zc, wc], 2))
    wc = wc.reshape(g, 4 * 2 * p, tc * h)

    af = pw[0][:, :, tc].reshape(1, g * p)
    ab = pw[1][:, :, tc].reshape(1, g * p)
    decay = jnp.concatenate([af.real, af.imag, ab.real, ab.imag], axis=0)
    swap = lambda a: jnp.swapaxes(a, 1, 2).astype(BF16)
    return swap(toep), swap(ws.reshape(g, tc * h, 4 * 2 * p)), swap(wc), decay


def _s5_intra_kernel(x_ref, toep_ref, ws_ref, y_ref, s0_ref, s1_ref, s2_ref, s3_ref):
    tc, _, hh, tr = x_ref.shape
    xs = [x_ref[:, k].reshape(tc * hh, tr) for k in range(2)]
    for k in range(2):
        y_ref[:, k] = jnp.dot(toep_ref[k], xs[k], preferred_element_type=F32).reshape(tc, hh, tr)
    st = (jnp.dot(ws_ref[0], xs[0], preferred_element_type=F32)
          + jnp.dot(ws_ref[1], xs[1], preferred_element_type=F32))
    for k, ref in enumerate((s0_ref, s1_ref, s2_ref, s3_ref)):
        ref[...] = st[k * LANES:(k + 1) * LANES, :]


def _s5_intra(xt, toep_t, ws_t, *, tr):
    tc, g, hh, nc = xt.shape
    kw = tc * hh
    npair = g // 2
    st_shape = jax.ShapeDtypeStruct((npair * LANES, nc), F32)
    st_spec = pl.BlockSpec((LANES, tr), lambda pr, r: (pr, r))
    x_spec = pl.BlockSpec((tc, 2, hh, tr), lambda pr, r: (0, pr, 0, r))
    return pl.pallas_call(
        _s5_intra_kernel,
        grid=(npair, nc // tr),
        in_specs=[x_spec,
                  pl.BlockSpec((2, kw, kw), lambda pr, r: (pr, 0, 0)),
                  pl.BlockSpec((2, 4 * LANES, kw), lambda pr, r: (pr, 0, 0))],
        out_specs=[x_spec, st_spec, st_spec, st_spec, st_spec],
        out_shape=[jax.ShapeDtypeStruct((tc, g, hh, nc), F32), st_shape, st_shape, st_shape, st_shape],
        compiler_params=_cparams(("parallel", "arbitrary")),
        name="s5_intra",
    )(xt, toep_t, ws_t)


def _s5_scan_kernel(decay_ref, cfr_ref, cfi_ref, cbr_ref, cbi_ref, sfr_ref, sfi_ref, sbr_ref, sbi_ref):
    n = cfr_ref.shape[0]
    w = cfr_ref.shape[1]
    afr = decay_ref[0:1, :]
    afi = decay_ref[1:2, :]
    abr = decay_ref[2:3, :]
    abi = decay_ref[3:4, :]
    zero = jnp.zeros((1, w), F32)

    def fwd(c, st):
        re, im = st
        sfr_ref[pl.ds(c, 1), :] = re
        sfi_ref[pl.ds(c, 1), :] = im
        cr = cfr_ref[pl.ds(c, 1), :]
        ci = cfi_ref[pl.ds(c, 1), :]
        return afr * re - afi * im + cr, afr * im + afi * re + ci

    lax.fori_loop(0, n, fwd, (zero, zero))

    def bwd(kk, st):
        c = n - 1 - kk
        re, im = st
        sbr_ref[pl.ds(c, 1), :] = re
        sbi_ref[pl.ds(c, 1), :] = im
        cr = cbr_ref[pl.ds(c, 1), :]
        ci = cbi_ref[pl.ds(c, 1), :]
        return abr * re - abi * im + cr, abr * im + abi * re + ci

    lax.fori_loop(0, n, bwd, (zero, zero))


def _s5_scan(decay, contribs, *, chunk0, n_seq, cps):
    w = contribs[0].shape[1]
    b0 = chunk0 // cps
    spec_in = pl.BlockSpec((cps, w), lambda b: (b0 + b, 0))
    spec_out = pl.BlockSpec((cps, w), lambda b: (b, 0))
    shape = jax.ShapeDtypeStruct((n_seq * cps, w), F32)
    return pl.pallas_call(
        _s5_scan_kernel,
        grid=(n_seq,),
        in_specs=[pl.BlockSpec((4, w), lambda b: (0, 0))] + [spec_in] * 4,
        out_specs=[spec_out] * 4,
        out_shape=[shape] * 4,
        compiler_params=_cparams(("parallel",)),
        name="s5_scan",
    )(decay, *contribs)


def _s5_out_kernel(y_ref, s0_ref, s1_ref, s2_ref, s3_ref, wc_ref, o_ref):
    tc, _, hh, tr = y_ref.shape
    st = jnp.concatenate([s0_ref[...], s1_ref[...], s2_ref[...], s3_ref[...]], axis=0).astype(BF16)
    for k in range(2):
        y = y_ref[:, k] + jnp.dot(wc_ref[k], st, preferred_element_type=F32).reshape(tc, hh, tr)
        o_ref[:, k] = 0.5 * y * (1.0 + lax.erf(y * (2.0 ** -0.5)))


def _s5_out(y_t, states_t, wc_t, *, tr):
    tc, g, hh, nc = y_t.shape
    kw = tc * hh
    npair = g // 2
    st_spec = pl.BlockSpec((LANES, tr), lambda pr, r: (pr, r))
    y_spec = pl.BlockSpec((tc, 2, hh, tr), lambda pr, r: (0, pr, 0, r))
    return pl.pallas_call(
        _s5_out_kernel,
        grid=(npair, nc // tr),
        in_specs=[y_spec, st_spec, st_spec, st_spec, st_spec,
                  pl.BlockSpec((2, kw, 4 * LANES), lambda pr, r: (pr, 0, 0))],
        out_specs=y_spec,
        out_shape=jax.ShapeDtypeStruct((tc, g, hh, nc), F32),
        compiler_params=_cparams(("parallel", "arbitrary")),
        name="s5_out",
    )(y_t, *states_t, wc_t)


def _conv_kernel(prev_ref, cur_ref, next_ref, wdw_ref, bdw_ref, lng_ref, lnb_ref, wpw_ref, bn_ref, o_ref, scr,
                 *, n_prompt, n_seq, nb, rt):
    si = pl.program_id(0)
    i = pl.program_id(1)
    tm = cur_ref.shape[0]
    prev_ok = jnp.logical_or(i > 0, si > n_prompt)
    next_ok = jnp.logical_or(i < nb - 1, jnp.logical_and(si >= n_prompt, si < n_seq - 1))
    scr[0:HALO, :] = jnp.where(prev_ok, prev_ref[...], 0.0)
    scr[HALO:HALO + tm, :] = cur_ref[...]
    scr[HALO + tm:2 * HALO + tm, :] = jnp.where(next_ok, next_ref[...], 0.0)
    off = HALO - CONV_PAD
    sub = 8
    for r0 in range(0, tm, rt):
        slabs = [scr[r0 + sub * a:r0 + sub * a + rt + sub, :] for a in range((off + CONV_KERNEL - 1) // sub + 1)]
        acc = None
        for b in range(sub):
            part = None
            for a, slab in enumerate(slabs):
                kk = sub * a + b - off
                if 0 <= kk < CONV_KERNEL:
                    term = wdw_ref[kk:kk + 1, :] * slab
                    part = term if part is None else part + term
            part = part[b:b + rt, :]
            acc = part if acc is None else acc + part
        acc = acc + bdw_ref[...]
        mu = jnp.mean(acc, axis=-1, keepdims=True)
        cen = acc - mu
        var = jnp.mean(cen * cen, axis=-1, keepdims=True)
        y = cen * lax.rsqrt(var + EPS) * lng_ref[...] + lnb_ref[...]
        y = y * jax.nn.sigmoid(y)
        z = jnp.dot(y.astype(BF16), wpw_ref[...], preferred_element_type=F32)
        o_ref[r0:r0 + rt, :] = (z * _rsqrt_mean_sq(z) * bn_ref[...]).astype(o_ref.dtype)


def _conv_module(hc, w_dw, b_dw, ln_g, ln_b, w_pw_bf, bn_conv, *, n_prompt, lc, tm):
    t, cw = hc.shape
    nb = lc // tm
    s = t // lc
    hb = tm // HALO
    n_hblk = t // HALO
    kpad = 32

    def row(si, i):
        return si * nb + i

    wdw = jnp.zeros((kpad, cw), F32).at[:CONV_KERNEL].set(w_dw.astype(F32))
    vec = lambda a: a.astype(F32).reshape(1, cw)
    const = lambda shape: pl.BlockSpec(shape, lambda si, i: (0, 0))
    return pl.pallas_call(
        functools.partial(_conv_kernel, n_prompt=n_prompt, n_seq=s, nb=nb, rt=min(128, tm)),
        grid=(s, nb),
        in_specs=[pl.BlockSpec((HALO, cw), lambda si, i: (jnp.maximum(row(si, i) * hb - 1, 0), 0)),
                  pl.BlockSpec((tm, cw), lambda si, i: (row(si, i), 0)),
                  pl.BlockSpec((HALO, cw), lambda si, i: (jnp.minimum((row(si, i) + 1) * hb, n_hblk - 1), 0)),
                  const((kpad, cw)), const((1, cw)), const((1, cw)), const((1, cw)), const((cw, cw)),
                  const((1, cw))],
        out_specs=pl.BlockSpec((tm, cw), lambda si, i: (row(si, i), 0)),
        out_shape=jax.ShapeDtypeStruct((t, cw), BF16),
        scratch_shapes=[pltpu.VMEM((tm + 2 * HALO, cw), F32)],
        compiler_params=_cparams(("parallel", "arbitrary")),
        name="conv_module",
    )(hc, hc, hc, wdw, vec(b_dw), vec(ln_g), vec(ln_b), w_pw_bf, vec(bn_conv))


def _outproj_kernel(x_ref, attn_p_ref, attn_s_ref, ys_ref, cn_ref, mod_ref, bn_ref, wglu_ref, bglu_ref, wout_ref,
                    npost_ref, npre2_ref, rw_ref, rb_ref, x1_ref, h2_ref, ti_ref, tg_ref, *, n_prompt):
    aw = attn_p_ref.shape[1]
    sw = ys_ref.shape[1]
    a = jnp.where(pl.program_id(0) < n_prompt, attn_p_ref[...], attn_s_ref[...]).astype(F32)
    an = a * _rsqrt_mean_sq(a) * bn_ref[:, 0:aw]
    y = ys_ref[...]
    z = jnp.dot(y.astype(BF16), wglu_ref[...], preferred_element_type=F32) + bglu_ref[...]
    sm = y * jax.nn.sigmoid(z)
    sn = sm * _rsqrt_mean_sq(sm) * bn_ref[:, aw:aw + sw]
    mix = (jnp.dot(an.astype(BF16), wout_ref[0:aw, :], preferred_element_type=F32)
           + jnp.dot(sn.astype(BF16), wout_ref[aw:aw + sw, :], preferred_element_type=F32)
           + jnp.dot(cn_ref[...], wout_ref[aw + sw:, :], preferred_element_type=F32))
    g1 = mod_ref[0, 2:3, :]
    sh2 = mod_ref[0, 3:4, :]
    sc2 = mod_ref[0, 4:5, :]
    x1 = x_ref[...] + g1 * (mix * _rsqrt_mean_sq(mix) * npost_ref[...])
    x1_ref[...] = x1
    h2 = x1 * _rsqrt_mean_sq(x1) * npre2_ref[...] * (1.0 + sc2) + sh2
    h2_ref[...] = h2
    h_hi = h2.astype(BF16)
    h_lo = (h2 - h_hi.astype(F32)).astype(BF16)
    both = jnp.dot(h_hi, rw_ref[...], preferred_element_type=F32)
    logits = (both[:, :LANES] + both[:, LANES:]
              + jnp.dot(h_lo, rw_ref[:, :LANES], preferred_element_type=F32) + rb_ref[...])
    tm = logits.shape[0]
    lane = lax.broadcasted_iota(I32, (tm, LANES), 1)
    vals = []
    ti = jnp.zeros((tm, LANES), I32)
    for kk in range(TOP_K):
        m = jnp.max(logits, axis=1, keepdims=True)
        ik = jnp.min(jnp.where(logits == m, lane, LANES), axis=1, keepdims=True)
        vals.append(m)
        ti = jnp.where(lane == kk, ik, ti)
        logits = jnp.where(lane == ik, NEG_BIG * 2.0, logits)
    es = [jnp.exp(v - vals[0]) for v in vals]
    inv = 1.0 / (es[0] + es[1] + es[2] + es[3])
    tg = jnp.zeros((tm, LANES), F32)
    for kk in range(TOP_K):
        tg = jnp.where(lane == kk, es[kk] * inv, tg)
    ti_ref[...] = ti.T[0:8, :]
    tg_ref[...] = tg


def _outproj(x, attn_p, attn_s, ys, cn, mod, bn, wglu_bf, bglu, wout_bf, npost, npre2, rw_pad, rb_pad, *, lc, tm):
    t, d = x.shape
    nb = lc // tm
    s = t // lc
    aw, sw, cw = attn_p.shape[1], ys.shape[1], cn.shape[1]
    nbp = attn_p.shape[0] // tm
    n_prompt = nbp // nb

    def row(si, i):
        return si * nb + i

    rows = lambda w: pl.BlockSpec((tm, w), lambda si, i: (row(si, i), 0))
    const = lambda shape: pl.BlockSpec(shape, lambda si, i: (0, 0))
    return pl.pallas_call(
        functools.partial(_outproj_kernel, n_prompt=n_prompt),
        grid=(s, nb),
        in_specs=[rows(d),
                  pl.BlockSpec((tm, aw), lambda si, i: (jnp.minimum(row(si, i), nbp - 1), 0)),
                  pl.BlockSpec((tm, aw), lambda si, i: (jnp.maximum(row(si, i) - nbp, 0), 0)),
                  rows(sw), rows(cw),
                  pl.BlockSpec((1, 6, d), lambda si, i: (si, 0, 0)),
                  const((1, d)), const((sw, sw)), const((1, sw)), const((d, d)), const((1, d)), const((1, d)),
                  const((d, 2 * LANES)), const((1, LANES))],
        out_specs=[rows(d), rows(d), pl.BlockSpec((8, tm), lambda si, i: (0, row(si, i))), rows(LANES)],
        out_shape=[jax.ShapeDtypeStruct((t, d), F32), jax.ShapeDtypeStruct((t, d), F32),
                   jax.ShapeDtypeStruct((8, t), I32), jax.ShapeDtypeStruct((t, LANES), F32)],
        compiler_params=_cparams(("parallel", "arbitrary")),
        name="outproj_router",
    )(x, attn_p, attn_s, ys, cn, mod, bn, wglu_bf, bglu, wout_bf, npost, npre2, rw_pad, rb_pad)


def _route(topi, blk):
    t = topi.shape[1]
    n_assign = t * TOP_K
    flat_e = topi[:TOP_K].reshape(-1)
    eids = jnp.arange(N_EXPERTS, dtype=I32)
    onehot = (flat_e[:, None] == eids[None, :]).astype(I32)
    cb = math.gcd(n_assign, 512)
    tri = (jnp.arange(cb)[:, None] >= jnp.arange(cb)[None, :]).astype(BF16)
    within = jnp.einsum('ij,bjc->bic', tri, onehot.reshape(n_assign // cb, cb, N_EXPERTS).astype(BF16),
                        preferred_element_type=F32).astype(I32)
    block_tot = within[:, -1, :]
    csum = (within + (jnp.cumsum(block_tot, axis=0) - block_tot)[:, None, :]).reshape(n_assign, N_EXPERTS)
    counts = csum[-1]
    rank = jnp.sum((csum - onehot) * onehot, axis=1)
    padded = ((counts + blk - 1) // blk) * blk
    ends_p = jnp.cumsum(padded)
    starts_p = ends_p - padded
    dest = (jnp.sum(onehot * starts_p[None, :], axis=1) + rank).astype(I32)
    dest = jnp.pad(dest.reshape(TOP_K, t), ((0, 8 - TOP_K), (0, 0)))
    n_blocks = n_assign // blk + N_EXPERTS
    cap = n_blocks * blk
    n_used = (ends_p[-1] // blk).astype(I32)
    bidx = jnp.arange(n_blocks, dtype=I32)
    be_raw = jnp.sum((ends_p[None, :] <= (bidx * blk)[:, None]).astype(I32), axis=1)
    last_e = jnp.max(jnp.where(counts > 0, eids, 0))
    block_e = jnp.where(bidx < n_used, jnp.minimum(be_raw, N_EXPERTS - 1), last_e).astype(I32)
    fill_base = jnp.concatenate([starts_p + counts, ends_p[-1:]]).astype(I32)
    fill_n = jnp.concatenate([padded - counts, cap - ends_p[-1:]]).astype(I32)
    return dest, block_e, n_used.reshape(1), fill_base, fill_n


def _dispatch_kernel(fb_ref, fn_ref, dest_ref, h_ref, xs_hbm, zrow, sem, zsem):
    step = pl.program_id(0)
    tm = h_ref.shape[0]

    @pl.when(step == 0)
    def _():
        zrow[...] = jnp.zeros(zrow.shape, zrow.dtype)
        for e in range(N_EXPERTS + 1):
            base = fb_ref[e]
            n = fn_ref[e]

            def zissue(j, carry):
                pltpu.make_async_copy(zrow.at[pl.ds(0, 1)], xs_hbm.at[pl.ds(base + j, 1)], zsem).start()
                return carry

            def zwait(j, carry):
                pltpu.make_async_copy(zrow.at[pl.ds(0, 1)], xs_hbm.at[pl.ds(0, 1)], zsem).wait()
                return carry

            lax.fori_loop(0, n, zissue, 0)
            lax.fori_loop(0, n, zwait, 0)

    def issue(r, carry):
        for kk in range(TOP_K):
            pltpu.make_async_copy(h_ref.at[pl.ds(r, 1)],
                                  xs_hbm.at[pl.ds(dest_ref[kk, r], 1)], sem).start(priority=kk % 2)
        return carry

    lax.fori_loop(0, tm, issue, 0, unroll=4)
    for kk in range(TOP_K):
        pltpu.make_async_copy(h_ref, xs_hbm.at[pl.ds(0, tm)], sem).wait()


def _dispatch(h2, dest, fill_base, fill_n, *, cap, tm):
    t, d = h2.shape
    grid_spec = pltpu.PrefetchScalarGridSpec(
        num_scalar_prefetch=2,
        grid=(t // tm,),
        in_specs=[pl.BlockSpec((8, tm), lambda i, fb, fn: (0, i), memory_space=pltpu.SMEM),
                  pl.BlockSpec((tm, d), lambda i, fb, fn: (i, 0))],
        out_specs=pl.BlockSpec(memory_space=pl.ANY),
        scratch_shapes=[pltpu.VMEM((8, d), F32), pltpu.SemaphoreType.DMA(()), pltpu.SemaphoreType.DMA(())],
    )
    return pl.pallas_call(
        _dispatch_kernel,
        grid_spec=grid_spec,
        out_shape=jax.ShapeDtypeStruct((cap, d), F32),
        compiler_params=_cparams(("arbitrary",), disable_bounds_checks=True),
        name="moe_dispatch",
    )(fill_base, fill_n, dest, h2)


def _expert_kernel(be_ref, nu_ref, xs_ref, w1_ref, b1_ref, w2_ref, b2_ref, ys_ref, w1b, w2b):
    b = pl.program_id(0)
    de = w2_ref.shape[0]
    used = b < nu_ref[0]
    fresh = jnp.logical_or(b == 0, be_ref[b] != be_ref[jnp.maximum(b - 1, 0)])

    @pl.when(jnp.logical_and(used, fresh))
    def _():
        w1b[...] = w1_ref[...].astype(BF16)
        w2b[...] = w2_ref[...].astype(BF16)

    @pl.when(used)
    def _():
        x = xs_ref[...].astype(BF16)
        y = None
        for c0 in range(0, de, EXPERT_COLS):
            c1 = c0 + EXPERT_COLS
            gt = jnp.dot(x, w1b[:, c0:c1], preferred_element_type=F32) + b1_ref[:, c0:c1]
            up = jnp.dot(x, w1b[:, de + c0:de + c1], preferred_element_type=F32) + b1_ref[:, de + c0:de + c1]
            gt = jnp.minimum(gt, SWIGLU_LIMIT)
            up = jnp.clip(up, -SWIGLU_LIMIT, SWIGLU_LIMIT)
            act = (up + 1.0) * (gt * jax.nn.sigmoid(SWIGLU_ALPHA * gt))
            part = jnp.dot(act.astype(BF16), w2b[c0:c1, :], preferred_element_type=F32)
            y = part if y is None else y + part
        ys_ref[...] = y + b2_ref[...]

    @pl.when(jnp.logical_not(used))
    def _():
        ys_ref[...] = jnp.zeros(ys_ref.shape, ys_ref.dtype)


def _experts(xs, block_e, n_used, w1, b1, w2, b2, *, layer, blk):
    cap, d = xs.shape
    nl, ne, _, de2 = w1.shape
    de = w2.shape[2]
    nblk = cap // blk

    def xrow(b, be, nu):
        return (jnp.minimum(b, nu[0] - 1), 0)

    grid_spec = pltpu.PrefetchScalarGridSpec(
        num_scalar_prefetch=2,
        grid=(nblk,),
        in_specs=[pl.BlockSpec((blk, d), xrow),
                  pl.BlockSpec((None, None, d, de2), lambda b, be, nu: (layer, be[b], 0, 0)),
                  pl.BlockSpec((None, None, 1, de2), lambda b, be, nu: (layer, be[b], 0, 0)),
                  pl.BlockSpec((None, None, de, d), lambda b, be, nu: (layer, be[b], 0, 0)),
                  pl.BlockSpec((None, None, 1, d), lambda b, be, nu: (layer, be[b], 0, 0))],
        out_specs=pl.BlockSpec((blk, d), lambda b, be, nu: (b, 0)),
        scratch_shapes=[pltpu.VMEM((d, de2), BF16), pltpu.VMEM((de, d), BF16)],
    )
    return pl.pallas_call(
        _expert_kernel,
        grid_spec=grid_spec,
        out_shape=jax.ShapeDtypeStruct((cap, d), F32),
        compiler_params=_cparams(("arbitrary",), vmem_limit_bytes=EXPERT_VMEM_LIMIT),
        name="moe_experts",
    )(block_e, n_used, xs, w1, b1.reshape(nl, ne, 1, de2), w2, b2.reshape(nl, ne, 1, d))


def _combine_kernel(dest_ref, gate_ref, x1_ref, mod_ref, npost_ref, ys_hbm, o_ref, buf, sem):
    tm = x1_ref.shape[0]

    def issue(r, carry):
        for kk in range(TOP_K):
            pltpu.make_async_copy(ys_hbm.at[pl.ds(dest_ref[kk, r], 1)],
                                  buf.at[kk, pl.ds(r, 1)], sem).start(priority=kk % 2)
        return carry

    lax.fori_loop(0, tm, issue, 0, unroll=4)
    for kk in range(TOP_K):
        pltpu.make_async_copy(ys_hbm.at[pl.ds(0, tm)], buf.at[kk], sem).wait()
    gate = gate_ref[...]
    ff = gate[:, 0:1] * buf[0]
    for kk in range(1, TOP_K):
        ff = ff + gate[:, kk:kk + 1] * buf[kk]
    g2 = mod_ref[0, 5:6, :]
    o_ref[...] = x1_ref[...] + g2 * (ff * _rsqrt_mean_sq(ff) * npost_ref[...])


def _combine(ys, dest, gate, x1, mod, npost, *, lc, tm, seq0=0, n_seq=None):
    t, d = x1.shape
    nb = lc // tm
    s = t // lc - seq0 if n_seq is None else n_seq

    def row(si, i):
        return (seq0 + si) * nb + i

    return pl.pallas_call(
        _combine_kernel,
        grid=(s, nb),
        in_specs=[pl.BlockSpec((8, tm), lambda si, i: (0, row(si, i)), memory_space=pltpu.SMEM),
                  pl.BlockSpec((tm, LANES), lambda si, i: (row(si, i), 0)),
                  pl.BlockSpec((tm, d), lambda si, i: (row(si, i), 0)),
                  pl.BlockSpec((1, 6, d), lambda si, i: (seq0 + si, 0, 0)),
                  pl.BlockSpec((1, d), lambda si, i: (0, 0)),
                  pl.BlockSpec(memory_space=pl.ANY)],
        out_specs=pl.BlockSpec((tm, d), lambda si, i: (si * nb + i, 0)),
        out_shape=jax.ShapeDtypeStruct((s * lc, d), F32),
        scratch_shapes=[pltpu.VMEM((TOP_K, tm, d), F32), pltpu.SemaphoreType.DMA(())],
        compiler_params=_cparams(("arbitrary", "arbitrary"), disable_bounds_checks=True),
        name="moe_combine",
    )(dest, gate, x1, mod, npost, ys)


def _rope_tables(n_pos):
    rows = n_pos // GRID_W
    row_idx = jnp.repeat(jnp.arange(rows, dtype=F32), GRID_W)
    col_idx = jnp.tile(jnp.arange(GRID_W, dtype=F32), rows)
    inv = 1.0 / (ROPE_THETA ** (jnp.arange(0, ROPE_AXIS_DIM, 2, dtype=F32) / ROPE_AXIS_DIM))
    ar = row_idx[:, None] * inv[None, :]
    ac = col_idx[:, None] * inv[None, :]
    cos = jnp.concatenate([jnp.cos(ar), jnp.cos(ar), jnp.cos(ac), jnp.cos(ac)], axis=-1)
    sin = jnp.concatenate([-jnp.sin(ar), jnp.sin(ar), -jnp.sin(ac), jnp.sin(ac)], axis=-1)
    reps = LANES // HEAD_DIM
    return jnp.tile(cos, (1, reps)), jnp.tile(sin, (1, reps))


def _tiles(lc):
    tm = min(512, lc)
    tq = min(256, lc)
    return tm, tq


def kernel(x_prompt, x_sample, c_prompt, c_sample, w_ada, b_ada, norm_pre_mix, norm_post_mix, norm_pre_ffn, norm_post_ffn, w_in, q_norm, k_norm, ssm_lam_re, ssm_lam_im, ssm_log_dt, ssm_b_re, ssm_b_im, ssm_c_re, ssm_c_im, ssm_d, ssm_w_glu, ssm_b_glu, conv_w_dw, conv_b_dw, conv_ln_g, conv_ln_b, conv_w_pw, branch_norm, w_out, router_w, router_b, expert_w1, expert_b1, expert_w2, expert_b2):
    bsz, lc, d = x_prompt.shape
    dbsz, dl, _ = x_sample.shape
    assert dbsz == 1 and dl % lc == 0 and lc % S5_CHUNK == 0 and lc % GRID_W == 0
    n_sample = dl // lc
    n_seq = bsz + n_sample
    t_prompt = bsz * lc
    t = t_prompt + dl
    depth = w_ada.shape[0]
    ssm_w = ssm_d.shape[1]
    conv_w = conv_b_dw.shape[1]
    n_groups = ssm_w // SSM_GROUP
    tm, tq = _tiles(lc)
    tc = S5_CHUNK
    nc = t // tc
    cps = lc // tc
    kw = tc * SSM_GROUP
    assert t_prompt % dl == 0
    blk = min(MOE_BLOCK, t * TOP_K // N_EXPERTS)

    x = jnp.concatenate([x_prompt.reshape(t_prompt, d), x_sample.reshape(dl, d)], axis=0)
    c_all = jnp.concatenate([c_prompt, jnp.repeat(c_sample, n_sample, axis=0)], axis=0)
    cos_t, sin_t = _rope_tables(max(lc, dl))
    bd = (jnp.arange(LANES)[:, None] // HEAD_DIM == jnp.arange(LANES)[None, :] // HEAD_DIM).astype(BF16)
    reps = LANES // HEAD_DIM
    vec = lambda a: a.astype(F32).reshape(1, -1)
    s5_w = jax.vmap(lambda *p: _s5_weights(*p, tc))(ssm_lam_re, ssm_lam_im, ssm_log_dt, ssm_b_re, ssm_b_im,
                                                    ssm_c_re, ssm_c_im, ssm_d)

    for li in range(depth):
        mod = _modulation(c_all, w_ada[li], b_ada[li]).reshape(n_seq, 6, d)
        qkn = jnp.concatenate([jnp.tile(vec(q_norm[li]) * (HEAD_DIM ** -0.5 * math.log2(math.e)),
                                        (ATTN_W // LANES, reps)),
                               jnp.tile(vec(k_norm[li]), (1, reps)),
                               jnp.zeros((8 - ATTN_W // LANES - 1, LANES), F32)], axis=0)
        q_t, k, v_t, u, hc = _inproj(x, mod, vec(norm_pre_mix[li]), w_in[li].astype(BF16), qkn, cos_t, sin_t, bd,
                                     n_prompt=bsz, lc=lc, tm=tm, ssm_w=ssm_w, conv_w=conv_w)

        attn_p = _attention(q_t, k, v_t, row0=0, n_seq=bsz, lkv=lc, tq=min(2 * tq, lc), tk=tm)
        attn_s = _attention(q_t, k, v_t, row0=t_prompt, n_seq=1, lkv=dl, tq=tq, tk=tm)

        toep, ws, wc, decay = [w[li] for w in s5_w]
        xt = u.reshape(nc, tc * ssm_w).T.reshape(tc, n_groups, SSM_GROUP, nc)
        tr = nc if nc % LANES else math.gcd(nc, 768)
        y_t, *contribs_t = _s5_intra(xt, toep, ws, tr=tr)
        contribs = [a.T for a in contribs_t]
        st_p = _s5_scan(decay, contribs, chunk0=0, n_seq=bsz, cps=cps)
        st_s = _s5_scan(decay, contribs, chunk0=t_prompt // tc, n_seq=1, cps=dl // tc)
        states_t = [jnp.concatenate([a, b], axis=0).T for a, b in zip(st_p, st_s)]
        yg_t = _s5_out(y_t, states_t, wc, tr=tr)
        ys = yg_t.transpose(3, 0, 1, 2).reshape(t, ssm_w)

        bn = vec(branch_norm[li])
        cn = _conv_module(hc, conv_w_dw[li], conv_b_dw[li], conv_ln_g[li], conv_ln_b[li],
                          conv_w_pw[li].astype(BF16), bn[0, ATTN_W + ssm_w:], n_prompt=bsz, lc=lc, tm=tm)

        rw_pad = jnp.zeros((d, LANES), F32).at[:, :N_EXPERTS].set(router_w[li].astype(F32))
        rb_pad = jnp.full((1, LANES), NEG_BIG, F32).at[0, :N_EXPERTS].set(router_b[li].astype(F32))
        rw_hi = rw_pad.astype(BF16)
        rw_pad = jnp.concatenate([rw_hi, (rw_pad - rw_hi.astype(F32)).astype(BF16)], axis=1)
        x1, h2, topi, topg = _outproj(x, attn_p, attn_s, ys, cn, mod, bn, ssm_w_glu[li].astype(BF16), vec(ssm_b_glu[li]),
                                      w_out[li].astype(BF16), vec(norm_post_mix[li]), vec(norm_pre_ffn[li]),
                                      rw_pad, rb_pad, lc=lc, tm=tm)

        dest, block_e, n_used, fill_base, fill_n = _route(topi, blk)
        xs = _dispatch(h2, dest, fill_base, fill_n, cap=block_e.shape[0] * blk, tm=tm)
        ye = _experts(xs, block_e, n_used, expert_w1.astype(F32), expert_b1.astype(F32),
                      expert_w2.astype(F32), expert_b2.astype(F32), layer=li, blk=blk)
        if li + 1 < depth:
            x = _combine(ye, dest, topg, x1, mod, vec(norm_post_ffn[li]), lc=lc, tm=tm)
        else:
            out_p = _combine(ye, dest, topg, x1, mod, vec(norm_post_ffn[li]), lc=lc, tm=tm, seq0=0, n_seq=bsz)
            out_s = _combine(ye, dest, topg, x1, mod, vec(norm_post_ffn[li]), lc=lc, tm=tm, seq0=bsz,
                             n_seq=n_sample)

    return (out_p.reshape(bsz, lc, d), out_s.reshape(dbsz, dl, d))
```
